```python
import jax, jax.numpy as jnp
from jax import lax
import numpy as np

D_MODEL = 1024
BATCH = 1
SEQ = 16384
DEPTH = 1

N_META = 16
GRID_W = 64
CHUNK = 128
Q_BLOCK = 128
EPS = 1e-6
HG_HEADS = 4
HG_K = 128
HG_V = 128
HG_KW = HG_HEADS * HG_K
HG_VW = HG_HEADS * HG_V
AT_HEADS = 8
AT_KV_HEADS = 2
AT_HD = 64
AT_GROUP = AT_HEADS // AT_KV_HEADS
AT_W = AT_HEADS * AT_HD
AT_KVW = AT_KV_HEADS * AT_HD
ROPE_THETA = 10000.0
ROPE_AXIS = AT_HD // 2
D_FF = 2816
IN_SIZES = (HG_KW, HG_VW, HG_KW, HG_KW, HG_VW, AT_W, AT_KVW, AT_KVW, D_MODEL, D_MODEL)
D_IN = sum(IN_SIZES)

kernel_name = 'hybrid_hgrn2_axial_gqa_macaron_block'


def rms_norm(x, w):
    xf = x.astype(jnp.float32)
    y = xf * lax.rsqrt(jnp.mean(xf * xf, axis=-1, keepdims=True) + EPS)
    return (y * w.astype(jnp.float32)).astype(x.dtype)


def swiglu(x, w_gate, w_up, w_down):
    return (jax.nn.silu(x @ w_gate) * (x @ w_up)) @ w_down


def split_cols(z, sizes):
    outs = []
    start = 0
    for s in sizes:
        outs.append(z[..., start:start + s])
        start += s
    return outs


def axial_rope_tables(n_real):
    rows = n_real // GRID_W
    row = jnp.repeat(jnp.arange(rows, dtype=jnp.float32), GRID_W)
    col = jnp.tile(jnp.arange(GRID_W, dtype=jnp.float32), rows)
    zeros = jnp.zeros((N_META,), jnp.float32)
    row = jnp.concatenate([zeros, row])
    col = jnp.concatenate([zeros, col])
    inv = ROPE_THETA ** (-jnp.arange(0, ROPE_AXIS, 2, dtype=jnp.float32) / ROPE_AXIS)
    ang = jnp.concatenate([row[:, None] * inv, col[:, None] * inv], axis=-1)
    return jnp.cos(ang), jnp.sin(ang)


def apply_rope(x, cos, sin):
    xf = x.astype(jnp.float32).reshape(x.shape[:-1] + (AT_HD // 2, 2))
    x1, x2 = xf[..., 0], xf[..., 1]
    c = cos[None, :, None, :]
    s = sin[None, :, None, :]
    out = jnp.stack([x1 * c - x2 * s, x1 * s + x2 * c], axis=-1).reshape(x.shape)
    return out.astype(x.dtype)


def attend(qb, k, v):
    s = jnp.einsum('bkgqd,bksd->bkgqs', qb, k, preferred_element_type=jnp.float32) * (AT_HD ** -0.5)
    p = jax.nn.softmax(s, axis=-1)
    return jnp.einsum('bkgqs,bksd->bkgqd', p.astype(v.dtype), v)


def axial_gqa(zq, zk, zv, q_norm_w, k_norm_w, cos, sin):
    B, L, _ = zq.shape
    q = rms_norm(zq.reshape(B, L, AT_HEADS, AT_HD), q_norm_w)
    k = rms_norm(zk.reshape(B, L, AT_KV_HEADS, AT_HD), k_norm_w)
    v = zv.reshape(B, L, AT_KV_HEADS, AT_HD)
    q = apply_rope(q, cos, sin)
    k = apply_rope(k, cos, sin)
    q = q.transpose(0, 2, 1, 3).reshape(B, AT_KV_HEADS, AT_GROUP, L, AT_HD)
    k = k.transpose(0, 2, 1, 3)
    v = v.transpose(0, 2, 1, 3)
    n_real = L - N_META
    n_blk = n_real // Q_BLOCK
    o_meta = attend(q[:, :, :, :N_META], k, v)
    q_real = jnp.moveaxis(q[:, :, :, N_META:].reshape(B, AT_KV_HEADS, AT_GROUP, n_blk, Q_BLOCK, AT_HD), 3, 0)
    o_real = lax.map(lambda qb: attend(qb, k, v), q_real)
    o_real = jnp.moveaxis(o_real, 0, 3).reshape(B, AT_KV_HEADS, AT_GROUP, n_real, AT_HD)
    o = jnp.concatenate([o_meta, o_real], axis=3)
    return o.transpose(0, 3, 1, 2, 4).reshape(B, L, AT_W)


def gla_chunked(q, k, v, logf):
    B, T, H, K = q.shape
    n = T // CHUNK
    def to_chunks(a):
        return a.reshape(B, n, CHUNK, H, a.shape[-1]).transpose(1, 0, 3, 2, 4)
    mask = jnp.tril(jnp.ones((CHUNK, CHUNK), dtype=bool))

    def step(S, xs):
        qc, kc, vc, lc = xs
        b = jnp.cumsum(lc, axis=2)
        o_inter = jnp.einsum('bhck,bhkv->bhcv', qc * jnp.exp(b), S)
        diff = jnp.where(mask[None, None, :, :, None], b[:, :, :, None, :] - b[:, :, None, :, :], -jnp.inf)
        attn = jnp.einsum('bhtk,bhsk,bhtsk->bhts', qc, kc, jnp.exp(diff))
        o_intra = jnp.einsum('bhts,bhsv->bhtv', attn, vc)
        b_last = b[:, :, -1:, :]
        S_new = jnp.exp(b_last[:, :, 0, :])[..., None] * S + jnp.einsum('bhsk,bhsv->bhkv', kc * jnp.exp(b_last - b), vc)
        return S_new, o_inter + o_intra

    S0 = jnp.zeros((B, H, K, v.shape[-1]), jnp.float32)
    _, o = lax.scan(step, S0, (to_chunks(q), to_chunks(k), to_chunks(v), to_chunks(logf)))
    return o.transpose(1, 0, 3, 2, 4).reshape(B, T, H, v.shape[-1])


def hgrn2_bidir(zq, zi, zf_f, zf_b, zg, lb_f, lb_b, out_norm_w):
    B, L, _ = zq.shape
    q = jax.nn.silu(zq.astype(jnp.float32)).reshape(B, L, HG_HEADS, HG_K)
    v = zi.astype(jnp.float32).reshape(B, L, HG_HEADS, HG_V)

    def gates(zf, lb):
        lb = lb.reshape(HG_HEADS, HG_K)
        kk = (1.0 - lb) * jax.nn.sigmoid(-zf.astype(jnp.float32).reshape(B, L, HG_HEADS, HG_K))
        return kk, jnp.log1p(-kk)

    k_f, lf_f = gates(zf_f, lb_f)
    k_b, lf_b = gates(zf_b, lb_b)
    n_pad = CHUNK - N_META
    pad = lambda a: jnp.pad(a, ((0, 0), (n_pad, 0), (0, 0), (0, 0)))
    flip = lambda a: jnp.flip(a, axis=1)
    q_p, v_p = pad(q), pad(v)
    o_fwd = gla_chunked(q_p, pad(k_f), v_p, pad(lf_f))
    o_bwd = flip(gla_chunked(flip(q_p), flip(pad(k_b)), flip(v_p), flip(pad(lf_b))))
    o = (o_fwd + o_bwd)[:, n_pad:]
    o = o * lax.rsqrt(jnp.mean(o * o, axis=-1, keepdims=True) + EPS) * out_norm_w.astype(jnp.float32).reshape(HG_HEADS, HG_V)
    o = o.reshape(B, L, HG_VW) * jax.nn.silu(zg.astype(jnp.float32))
    return o.astype(zq.dtype)


def setup_inputs(seed: int = 0) -> dict:
    key = jax.random.key(seed)
    ks = jax.random.split(key, 24)
    nrm = lambda k, shape, fan_in: jax.random.normal(k, shape, jnp.float32) * (fan_in ** -0.5)
    gain = lambda k, shape: 1.0 + 0.01 * jax.random.normal(k, shape, jnp.float32)
    return {
        'x': jax.random.normal(ks[0], (BATCH, SEQ, D_MODEL), jnp.float32),
        'meta_tokens': jax.random.normal(ks[1], (N_META, D_MODEL), jnp.float32),
        'ffn1_norm': gain(ks[2], (DEPTH, D_MODEL)),
        'ffn1_w_gate': nrm(ks[3], (DEPTH, D_MODEL, D_FF), D_MODEL),
        'ffn1_w_up': nrm(ks[4], (DEPTH, D_MODEL, D_FF), D_MODEL),
        'ffn1_w_down': nrm(ks[5], (DEPTH, D_FF, D_MODEL), D_FF),
        'mix_norm': gain(ks[6], (DEPTH, D_MODEL)),
        'w_in': nrm(ks[7], (DEPTH, D_MODEL, D_IN), D_MODEL),
        'hg_lb_fwd': 0.1 * jax.random.normal(ks[8], (DEPTH + 1, HG_KW), jnp.float32),
        'hg_lb_bwd': 0.1 * jax.random.normal(ks[9], (DEPTH + 1, HG_KW), jnp.float32),
        'hg_out_norm': gain(ks[10], (DEPTH, HG_VW)),
        'q_norm': gain(ks[11], (DEPTH, AT_HD)),
        'k_norm': gain(ks[12], (DEPTH, AT_HD)),
        'w_up_a': nrm(ks[13], (DEPTH, HG_VW, D_MODEL), HG_VW),
        'w_up_b': nrm(ks[14], (DEPTH, AT_W, D_MODEL), AT_W),
        'w_out': nrm(ks[15], (DEPTH, D_MODEL, D_MODEL), D_MODEL),
        'ffn2_norm': gain(ks[16], (DEPTH, D_MODEL)),
        'ffn2_w_gate': nrm(ks[17], (DEPTH, D_MODEL, D_FF), D_MODEL),
        'ffn2_w_up': nrm(ks[18], (DEPTH, D_MODEL, D_FF), D_MODEL),
        'ffn2_w_down': nrm(ks[19], (DEPTH, D_FF, D_MODEL), D_FF),
    }


def reference(x, meta_tokens, ffn1_norm, ffn1_w_gate, ffn1_w_up, ffn1_w_down, mix_norm, w_in, hg_lb_fwd, hg_lb_bwd, hg_out_norm, q_norm, k_norm, w_up_a, w_up_b, w_out, ffn2_norm, ffn2_w_gate, ffn2_w_up, ffn2_w_down):
    B, n_real, _ = x.shape
    meta = jnp.broadcast_to(meta_tokens.astype(x.dtype)[None], (B, N_META, D_MODEL))
    h = jnp.concatenate([meta, x], axis=1)
    cos, sin = axial_rope_tables(n_real)
    lb_fwd_all = jnp.cumsum(jax.nn.softmax(hg_lb_fwd.astype(jnp.float32), axis=0), axis=0)
    lb_bwd_all = jnp.cumsum(jax.nn.softmax(hg_lb_bwd.astype(jnp.float32), axis=0), axis=0)
    for layer in range(DEPTH):
        h = h + 0.5 * swiglu(rms_norm(h, ffn1_norm[layer]), ffn1_w_gate[layer], ffn1_w_up[layer], ffn1_w_down[layer])
        u = rms_norm(h, mix_norm[layer])
        z = u @ w_in[layer]
        zq_a, zi_a, zf_f, zf_b, zg_a, zq_b, zk_b, zv_b, zgate_a, zgate_b = split_cols(z, IN_SIZES)
        y_a = hgrn2_bidir(zq_a, zi_a, zf_f, zf_b, zg_a, lb_fwd_all[layer], lb_bwd_all[layer], hg_out_norm[layer])
        y_b = axial_gqa(zq_b, zk_b, zv_b, q_norm[layer], k_norm[layer], cos, sin)
        mixed = jax.nn.sigmoid(zgate_a) * (y_a @ w_up_a[layer]) + jax.nn.sigmoid(zgate_b) * (y_b @ w_up_b[layer])
        h = h + mixed @ w_out[layer]
        h = h + 0.5 * swiglu(rms_norm(h, ffn2_norm[layer]), ffn2_w_gate[layer], ffn2_w_up[layer], ffn2_w_down[layer])
    return h[:, N_META:]
```

```python
import functools

import jax
import jax.numpy as jnp
from jax import lax
from jax.experimental import pallas as pl
from jax.experimental.pallas import tpu as pltpu

F32 = jnp.float32
BF16 = jnp.bfloat16

D_MODEL = 1024
D_FF = 2816
N_META = 16
GRID_W = 64
EPS = 1e-6
HG_HEADS = 4
HG_K = 128
HG_W = HG_HEADS * HG_K
AT_HEADS = 8
AT_KV_HEADS = 2
AT_HD = 64
AT_GROUP = AT_HEADS // AT_KV_HEADS
AT_W = AT_HEADS * AT_HD
AT_KVW = AT_KV_HEADS * AT_HD
ROPE_THETA = 10000.0
IN_SIZES = (HG_W, HG_W, HG_W, HG_W, HG_W, AT_W, AT_KVW, AT_KVW, D_MODEL, D_MODEL)
IN_OFFS = tuple(sum(IN_SIZES[:i]) for i in range(len(IN_SIZES)))
D_IN = sum(IN_SIZES)

ROW_TILE = 512
FF_CHUNK = 256
CHUNK = 128
Q_TILE = 512
KV_TILE = 512
VMEM_LIMIT = 56 * 1024 * 1024


def _dot(a, b):
    return jnp.dot(a, b, preferred_element_type=F32)


def _dot_nt(a, b):
    return lax.dot_general(a, b, (((1,), (1,)), ((), ())), preferred_element_type=F32)


def _dot_tn(a, b):
    return lax.dot_general(a, b, (((0,), (0,)), ((), ())), preferred_element_type=F32)


def _rms(x, w):
    ms = jnp.mean(x * x, axis=-1, keepdims=True)
    return x * lax.rsqrt(ms + EPS) * w


def _silu(x):
    return x * jax.nn.sigmoid(x)


def _swiglu_half_step(h, norm_w, wg_ref, wu_ref, wd_ref):
    hn = _rms(h, norm_w).astype(BF16)
    acc = None
    for c in range(D_FF // FF_CHUNK):
        sl = slice(c * FF_CHUNK, (c + 1) * FF_CHUNK)
        g = _dot(hn, wg_ref[:, sl])
        u = _dot(hn, wu_ref[:, sl])
        a = (_silu(g) * u).astype(BF16)
        d = _dot(a, wd_ref[sl, :])
        acc = d if acc is None else acc + d
    return h + 0.5 * acc


def _ffn_kernel(h_ref, nw_ref, wg_ref, wu_ref, wd_ref, o_ref):
    o_ref[...] = _swiglu_half_step(h_ref[...], nw_ref[...], wg_ref, wu_ref, wd_ref)


def _const_spec(shape):
    nd = len(shape)
    return pl.BlockSpec(shape, lambda *_: (0,) * nd, pipeline_mode=pl.Buffered(1))


def _ffn_call(h, nw, wg, wu, wd):
    rows = h.shape[0]
    return pl.pallas_call(
        _ffn_kernel,
        out_shape=jax.ShapeDtypeStruct((rows, D_MODEL), F32),
        grid=(rows // ROW_TILE,),
        in_specs=[
            pl.BlockSpec((ROW_TILE, D_MODEL), lambda i: (i, 0)),
            _const_spec((1, D_MODEL)),
            _const_spec((D_MODEL, D_FF)),
            _const_spec((D_MODEL, D_FF)),
            _const_spec((D_FF, D_MODEL)),
        ],
        out_specs=pl.BlockSpec((ROW_TILE, D_MODEL), lambda i: (i, 0)),
        compiler_params=pltpu.CompilerParams(
            dimension_semantics=("arbitrary",), vmem_limit_bytes=VMEM_LIMIT),
        name="ffn1",
    )(h, nw, wg, wu, wd)


def _lower_bound_gate(z, lbp, valid):
    m = jnp.max(lbp, axis=0, keepdims=True)
    e = jnp.exp(lbp - m)
    lb = e[0:1, :] / jnp.sum(e, axis=0, keepdims=True)
    kk = (1.0 - lb) * jax.nn.sigmoid(-z)
    kk = jnp.where(valid, kk, 0.0)
    return kk, jnp.log1p(-kk)


def _head_rms_rope(z, bd_ref, w, cos, sin):
    ms = _dot((z * z).astype(BF16), bd_ref[...])
    zn = z * lax.rsqrt(ms + EPS) * w
    width = z.shape[1]
    lane = lax.broadcasted_iota(jnp.int32, z.shape, 1)
    partner = jnp.where((lane & 1) == 0, pltpu.roll(zn, width - 1, 1), pltpu.roll(zn, 1, 1))
    return zn * cos + partner * sin


def _mix_proj_kernel(h_ref, nw_ref, win_ref, lbf_ref, lbb_ref, qnw_ref, knw_ref, cos_ref, sin_ref,
                     bdq_ref, bdk_ref,
                     hq_ref, hv_ref, kf_ref, lff_ref, kb_ref, lfb_ref, gs_ref, q_ref, kT_ref, v_ref,
                     ga_ref, gb_ref, *, n_valid):
    i = pl.program_id(0)
    un = _rms(h_ref[...], nw_ref[...]).astype(BF16)

    def proj(piece):
        off, size = IN_OFFS[piece], IN_SIZES[piece]
        return _dot(un, win_ref[:, off:off + size])

    row = i * ROW_TILE + lax.broadcasted_iota(jnp.int32, (ROW_TILE, 1), 0)
    valid = row < n_valid

    hq_ref[...] = _silu(proj(0))
    hv_ref[...] = proj(1).astype(BF16)
    kf, lff = _lower_bound_gate(proj(2), lbf_ref[...], valid)
    kf_ref[...] = kf
    lff_ref[...] = lff
    kb, lfb = _lower_bound_gate(proj(3), lbb_ref[...], valid)
    kb_ref[...] = kb
    lfb_ref[...] = lfb
    gs_ref[...] = _silu(proj(4))

    cos = cos_ref[...]
    sin = sin_ref[...]
    cos_q = jnp.concatenate([cos] * (AT_W // 128), axis=1)
    sin_q = jnp.concatenate([sin] * (AT_W // 128), axis=1)
    q = _head_rms_rope(proj(5), bdq_ref, qnw_ref[...], cos_q, sin_q) * (AT_HD ** -0.5)
    for h in range(AT_HEADS):
        q_ref[h] = q[:, h * AT_HD:(h + 1) * AT_HD].astype(BF16)
    k = _head_rms_rope(proj(6), bdk_ref, knw_ref[...], cos, sin)
    kT_ref[...] = jnp.transpose(k).astype(BF16)

    zv = proj(7)
    lane = lax.broadcasted_iota(jnp.int32, zv.shape, 1)
    v_ref[0] = jnp.where(lane < AT_HD, zv, 1.0).astype(BF16)
    v_ref[1] = jnp.where(lane < AT_HD, pltpu.roll(zv, AT_HD, 1), 1.0).astype(BF16)

    ga_ref[...] = jax.nn.sigmoid(proj(8))
    gb_ref[...] = jax.nn.sigmoid(proj(9))


def _mix_proj_call(h1, nw, win, lbf, lbb, qnw, knw, cos, sin, bdq, bdk, n_valid):
    rows = h1.shape[0]
    row_spec = lambda w: pl.BlockSpec((ROW_TILE, w), lambda i: (i, 0))
    out_shape = [
        jax.ShapeDtypeStruct((rows, HG_W), F32),
        jax.ShapeDtypeStruct((rows, HG_W), BF16),
        jax.ShapeDtypeStruct((rows, HG_W), F32),
        jax.ShapeDtypeStruct((rows, HG_W), F32),
        jax.ShapeDtypeStruct((rows, HG_W), F32),
        jax.ShapeDtypeStruct((rows, HG_W), F32),
        jax.ShapeDtypeStruct((rows, HG_W), F32),
        jax.ShapeDtypeStruct((AT_HEADS, rows, AT_HD), BF16),
        jax.ShapeDtypeStruct((AT_KVW, rows), BF16),
        jax.ShapeDtypeStruct((AT_KV_HEADS, rows, 128), BF16),
        jax.ShapeDtypeStruct((rows, D_MODEL), F32),
        jax.ShapeDtypeStruct((rows, D_MODEL), F32),
    ]
    out_specs = [
        row_spec(HG_W), row_spec(HG_W), row_spec(HG_W), row_spec(HG_W), row_spec(HG_W), row_spec(HG_W),
        row_spec(HG_W),
        pl.BlockSpec((AT_HEADS, ROW_TILE, AT_HD), lambda i: (0, i, 0)),
        pl.BlockSpec((AT_KVW, ROW_TILE), lambda i: (0, i)),
        pl.BlockSpec((AT_KV_HEADS, ROW_TILE, 128), lambda i: (0, i, 0)),
        row_spec(D_MODEL), row_spec(D_MODEL),
    ]
    in_specs = [
        row_spec(D_MODEL),
        _const_spec((1, D_MODEL)),
        _const_spec((D_MODEL, D_IN)),
        _const_spec(lbf.shape),
        _const_spec(lbb.shape),
        _const_spec((1, AT_W)),
        _const_spec((1, AT_KVW)),
        row_spec(128),
        row_spec(128),
        _const_spec((AT_W, AT_W)),
        _const_spec((AT_KVW, AT_KVW)),
    ]
    return pl.pallas_call(
        functools.partial(_mix_proj_kernel, n_valid=n_valid),
        out_shape=out_shape,
        grid=(rows // ROW_TILE,),
        in_specs=in_specs,
        out_specs=out_specs,
        compiler_params=pltpu.CompilerParams(
            dimension_semantics=("arbitrary",), vmem_limit_bytes=VMEM_LIMIT),
        name="mix_proj",
    )(h1, nw, win, lbf, lbb, qnw, knw, cos, sin, bdq, bdk)


def _split3(x):
    hi = x.astype(BF16)
    r1 = x - hi.astype(F32)
    mid = r1.astype(BF16)
    lo = (r1 - mid.astype(F32)).astype(BF16)
    return hi, mid, lo


def _hgrn_direction(q_ref, v_ref, k_ref, lf_ref, o_ref, st_ref, reverse):
    c = CHUNK
    q = q_ref[...]
    k = k_ref[...]
    v = v_ref[...]
    lf = lf_ref[...]

    r_i = lax.broadcasted_iota(jnp.int32, (c, c), 0)
    c_i = lax.broadcasted_iota(jnp.int32, (c, c), 1)
    tri = (c_i >= r_i) if reverse else (c_i <= r_i)
    tri = tri.astype(BF16)
    hi, mid, lo = _split3(lf)
    b = _dot(tri, hi) + _dot(tri, mid) + _dot(tri, lo)

    row = lax.broadcasted_iota(jnp.int32, (c, 1), 0)
    up = pltpu.roll(lf, c - 1, 0)
    dn = pltpu.roll(lf, 1, 0)

    ys = []
    valids = []
    for blk in (64, 32, 16, 8, 4, 2, 1):
        sh = blk.bit_length() - 1
        later = ((row >> sh) & 1) == 1
        q_role = jnp.logical_not(later) if reverse else later
        if blk >= 4:
            ref_off = blk if reverse else blk - 1
            pieces = []
            for g in range(c // (2 * blk)):
                r0 = g * 2 * blk + ref_off
                pieces.append(jnp.broadcast_to(b[r0:r0 + 1, :], (2 * blk, HG_W)))
            ref = pieces[0] if len(pieces) == 1 else jnp.concatenate(pieces, axis=0)
            gl = -jnp.abs(b - ref)
        elif blk == 2:
            m4 = row & 3
            if reverse:
                gl = jnp.where(m4 == 0, lf + up, jnp.where(m4 == 1, lf, jnp.where(m4 == 2, 0.0, dn)))
            else:
                gl = jnp.where(m4 == 0, up, jnp.where(m4 == 1, 0.0, jnp.where(m4 == 2, lf, dn + lf)))
        else:
            odd = (row & 1) == 1
            gl = jnp.where(odd, 0.0, lf) if reverse else jnp.where(odd, lf, 0.0)
        ys.append((jnp.where(q_role, q, k) * jnp.exp(gl)).astype(BF16))
        same = (r_i >> (sh + 1)) == (c_i >> (sh + 1))
        t_later = ((r_i >> sh) & 1) == 1
        s_later = ((c_i >> sh) & 1) == 1
        if reverse:
            valids.append(same & jnp.logical_not(t_later) & s_later)
        else:
            valids.append(same & t_later & jnp.logical_not(s_later))

    b_last = b[0:1, :] if reverse else b[c - 1:c, :]
    q_inter = (q * jnp.exp(b)).astype(BF16)
    k_state = (k * jnp.exp(b_last - b)).astype(BF16)
    e_last = jnp.exp(b_last)
    qk = q * k
    eye = r_i == c_i

    for h in range(HG_HEADS):
        hs = slice(h * HG_K, (h + 1) * HG_K)
        a = jnp.where(eye, jnp.sum(qk[:, hs], axis=1, keepdims=True), 0.0)
        for y, valid in zip(ys, valids):
            yh = y[:, hs]
            a = jnp.where(valid, _dot_nt(yh, yh), a)
        st = st_ref[h]
        o = _dot(a.astype(BF16), v[:, hs]) + _dot_nt(q_inter[:, hs], st.astype(BF16))
        o_ref[:, hs] = o
        st_ref[h] = st * e_last[:, hs] + _dot_tn(v[:, hs], k_state[:, hs])


def _hgrn_kernel(qf_ref, vf_ref, kf_ref, lff_ref, qb_ref, vb_ref, kb_ref, lfb_ref,
                 of_ref, ob_ref, sf_ref, sb_ref):
    @pl.when(pl.program_id(0) == 0)
    def _():
        sf_ref[...] = jnp.zeros_like(sf_ref)
        sb_ref[...] = jnp.zeros_like(sb_ref)

    _hgrn_direction(qf_ref, vf_ref, kf_ref, lff_ref, of_ref, sf_ref, reverse=False)
    _hgrn_direction(qb_ref, vb_ref, kb_ref, lfb_ref, ob_ref, sb_ref, reverse=True)


def _hgrn_call(hq, hv, kf, lff, kb, lfb, n_real):
    nb = n_real // CHUNK
    fwd = lambda s: (jnp.where(s == 0, nb, s - 1), 0)
    bwd = lambda s: (jnp.where(s == nb, nb, nb - 1 - s), 0)
    spec = lambda m: pl.BlockSpec((CHUNK, HG_W), m)
    out = jax.ShapeDtypeStruct(((nb + 1) * CHUNK, HG_W), F32)
    return pl.pallas_call(
        _hgrn_kernel,
        out_shape=[out, out],
        grid=(nb + 1,),
        in_specs=[spec(fwd), spec(fwd), spec(fwd), spec(fwd), spec(bwd), spec(bwd), spec(bwd), spec(bwd)],
        out_specs=[spec(fwd), spec(bwd)],
        scratch_shapes=[pltpu.VMEM((HG_HEADS, HG_K, HG_K), F32), pltpu.VMEM((HG_HEADS, HG_K, HG_K), F32)],
        compiler_params=pltpu.CompilerParams(
            dimension_semantics=("arbitrary",), vmem_limit_bytes=VMEM_LIMIT),
        name="hgrn",
    )(hq, hv, kf, lff, hq, hv, kb, lfb)


def _flash_kernel(q_ref, kT_ref, v_ref, kTt_ref, vt_ref, o_ref, m_ref, acc_ref, *, n_kv, n_tail_valid):
    tq = q_ref.shape[1]
    col = lax.broadcasted_iota(jnp.int32, (1, 128), 1)

    for h in range(AT_GROUP):
        s = _dot(q_ref[h], kTt_ref[...])
        s = jnp.where(col < n_tail_valid, s, -jnp.inf)
        m = jnp.max(s, axis=1, keepdims=True)
        p = jnp.exp(s - m).astype(BF16)
        acc_ref[h] = _dot(p, vt_ref[...])
        m_ref[h] = jnp.broadcast_to(m, (tq, 128))

    def body(j, carry):
        off = pl.multiple_of(j * KV_TILE, KV_TILE)
        kT = kT_ref[:, pl.ds(off, KV_TILE)]
        v = v_ref[pl.ds(off, KV_TILE), :]
        for h in range(AT_GROUP):
            s = _dot(q_ref[h], kT)
            m_prev = m_ref[h][:, 0:1]
            m_new = jnp.maximum(m_prev, jnp.max(s, axis=1, keepdims=True))
            alpha = jnp.exp(m_prev - m_new)
            p = jnp.exp(s - m_new).astype(BF16)
            acc_ref[h] = alpha * acc_ref[h] + _dot(p, v)
            m_ref[h] = jnp.broadcast_to(m_new, (tq, 128))
        return carry

    lax.fori_loop(0, n_kv, body, 0)

    outs = []
    for h in range(AT_GROUP):
        acc = acc_ref[h]
        outs.append(acc[:, :AT_HD] / acc[:, AT_HD:AT_HD + 1])
    o_ref[...] = jnp.concatenate(outs, axis=1).astype(BF16)


def _flash_call(q, kT, v, n_real):
    n_kv = n_real // KV_TILE
    tail_blk = n_real // 128
    gw = AT_GROUP * AT_HD
    return pl.pallas_call(
        functools.partial(_flash_kernel, n_kv=n_kv, n_tail_valid=N_META),
        out_shape=jax.ShapeDtypeStruct((n_real, AT_W), BF16),
        grid=(AT_KV_HEADS, n_real // Q_TILE),
        in_specs=[
            pl.BlockSpec((AT_GROUP, Q_TILE, AT_HD), lambda g, i: (g, i, 0)),
            pl.BlockSpec((AT_HD, n_real), lambda g, i: (g, 0)),
            pl.BlockSpec((None, n_real, 128), lambda g, i: (g, 0, 0)),
            pl.BlockSpec((AT_HD, 128), lambda g, i: (g, tail_blk)),
            pl.BlockSpec((None, 128, 128), lambda g, i: (g, tail_blk, 0)),
        ],
        out_specs=pl.BlockSpec((Q_TILE, gw), lambda g, i: (i, g)),
        scratch_shapes=[pltpu.VMEM((AT_GROUP, Q_TILE, 128), F32), pltpu.VMEM((AT_GROUP, Q_TILE, 128), F32)],
        compiler_params=pltpu.CompilerParams(
            dimension_semantics=("arbitrary", "arbitrary"), vmem_limit_bytes=VMEM_LIMIT),
        name="flash",
    )(q, kT, v, kT, v)


def _merge_ffn_kernel(h1_ref, of_ref, ob_ref, gs_ref, yb_ref, ga_ref, gb_ref, hgw_ref,
                      wua_ref, wub_ref, wout_ref, nw_ref, wg_ref, wu_ref, wd_ref, o_ref):
    o = of_ref[...] + ob_ref[...]
    normed = []
    for h in range(HG_HEADS):
        oh = o[:, h * HG_K:(h + 1) * HG_K]
        normed.append(oh * lax.rsqrt(jnp.mean(oh * oh, axis=-1, keepdims=True) + EPS))
    ya = (jnp.concatenate(normed, axis=1) * hgw_ref[...] * gs_ref[...]).astype(BF16)
    mixed = ga_ref[...] * _dot(ya, wua_ref[...]) + gb_ref[...] * _dot(yb_ref[...], wub_ref[...])
    h2 = h1_ref[...] + _dot(mixed.astype(BF16), wout_ref[...])
    o_ref[...] = _swiglu_half_step(h2, nw_ref[...], wg_ref, wu_ref, wd_ref)


def _merge_ffn_call(h1, o_f, o_b, gs, yb, ga, gb, hgw, wua, wub, wout, nw, wg, wu, wd, n_real):
    row_spec = lambda w: pl.BlockSpec((ROW_TILE, w), lambda i: (i, 0))
    return pl.pallas_call(
        _merge_ffn_kernel,
        out_shape=jax.ShapeDtypeStruct((n_real, D_MODEL), F32),
        grid=(n_real // ROW_TILE,),
        in_specs=[
            row_spec(D_MODEL), row_spec(HG_W), row_spec(HG_W), row_spec(HG_W), row_spec(AT_W),
            row_spec(D_MODEL), row_spec(D_MODEL),
            _const_spec((1, HG_W)),
            _const_spec((HG_W, D_MODEL)),
            _const_spec((AT_W, D_MODEL)),
            _const_spec((D_MODEL, D_MODEL)),
            _const_spec((1, D_MODEL)),
            _const_spec((D_MODEL, D_FF)),
            _const_spec((D_MODEL, D_FF)),
            _const_spec((D_FF, D_MODEL)),
        ],
        out_specs=row_spec(D_MODEL),
        compiler_params=pltpu.CompilerParams(
            dimension_semantics=("arbitrary",), vmem_limit_bytes=VMEM_LIMIT),
        name="merge_ffn2",
    )(h1, o_f, o_b, gs, yb, ga, gb, hgw, wua, wub, wout, nw, wg, wu, wd)


def _rope_tables(n_real, rows):
    half = AT_HD // 2
    grid_rows = n_real // GRID_W
    r = jnp.repeat(jnp.arange(grid_rows, dtype=F32), GRID_W)
    c = jnp.tile(jnp.arange(GRID_W, dtype=F32), grid_rows)
    zeros = jnp.zeros((rows - n_real,), F32)
    r = jnp.concatenate([r, zeros])
    c = jnp.concatenate([c, zeros])
    inv = ROPE_THETA ** (-jnp.arange(0, half, 2, dtype=F32) / half)
    ang = jnp.concatenate([r[:, None] * inv, c[:, None] * inv], axis=-1)
    cos = jnp.repeat(jnp.cos(ang), 2, axis=-1)
    sin = jnp.repeat(jnp.sin(ang), 2, axis=-1) * jnp.tile(jnp.array([-1.0, 1.0], F32), half)
    reps = 128 // AT_HD
    return jnp.tile(cos, (1, reps)), jnp.tile(sin, (1, reps))


def _head_mean_matrix(width):
    heads = width // AT_HD
    return jnp.kron(jnp.eye(heads, dtype=F32), jnp.full((AT_HD, AT_HD), 1.0 / AT_HD, F32)).astype(BF16)


def kernel(x, meta_tokens, ffn1_norm, ffn1_w_gate, ffn1_w_up, ffn1_w_down, mix_norm, w_in, hg_lb_fwd, hg_lb_bwd, hg_out_norm, q_norm, k_norm, w_up_a, w_up_b, w_out, ffn2_norm, ffn2_w_gate, ffn2_w_up, ffn2_w_down):
    batch, n_real, _ = x.shape
    assert batch == 1 and n_real % ROW_TILE == 0 and n_real % GRID_W == 0
    rows = n_real + ROW_TILE
    n_valid = n_real + N_META

    h0 = jnp.concatenate(
        [x[0], meta_tokens.astype(x.dtype), jnp.zeros((ROW_TILE - N_META, D_MODEL), x.dtype)], axis=0)
    bf = lambda w: w.astype(BF16)
    row = lambda w: w.reshape(1, -1).astype(F32)

    h1 = _ffn_call(h0, row(ffn1_norm[0]), bf(ffn1_w_gate[0]), bf(ffn1_w_up[0]), bf(ffn1_w_down[0]))

    cos, sin = _rope_tables(n_real, rows)
    hq, hv, kf, lff, kb, lfb, gs, q, kT, v, ga, gb = _mix_proj_call(
        h1, row(mix_norm[0]), bf(w_in[0]), hg_lb_fwd.astype(F32), hg_lb_bwd.astype(F32),
        row(jnp.tile(q_norm[0], AT_HEADS)), row(jnp.tile(k_norm[0], AT_KV_HEADS)), cos, sin,
        _head_mean_matrix(AT_W), _head_mean_matrix(AT_KVW), n_valid)

    o_f, o_b = _hgrn_call(hq, hv, kf, lff, kb, lfb, n_real)
    yb = _flash_call(q, kT, v, n_real)

    out = _merge_ffn_call(
        h1, o_f, o_b, gs, yb, ga, gb, row(hg_out_norm[0]), bf(w_up_a[0]), bf(w_up_b[0]), bf(w_out[0]),
        row(ffn2_norm[0]), bf(ffn2_w_gate[0]), bf(ffn2_w_up[0]), bf(ffn2_w_down[0]), n_real)
    return out.reshape(batch, n_real, D_MODEL)
```

```python
import functools

import jax
import jax.numpy as jnp
from jax import lax
from jax.experimental import pallas as pl
from jax.experimental.pallas import tpu as pltpu

F32 = jnp.float32
BF16 = jnp.bfloat16

D_MODEL = 1024
D_FF = 2816
N_META = 16
GRID_W = 64
EPS = 1e-6
HG_HEADS = 4
HG_K = 128
HG_W = HG_HEADS * HG_K
AT_HEADS = 8
AT_KV_HEADS = 2
AT_HD = 64
AT_GROUP = AT_HEADS // AT_KV_HEADS
AT_W = AT_HEADS * AT_HD
AT_KVW = AT_KV_HEADS * AT_HD
ROPE_THETA = 10000.0
IN_SIZES = (HG_W, HG_W, HG_W, HG_W, HG_W, AT_W, AT_KVW, AT_KVW, D_MODEL, D_MODEL)
IN_OFFS = tuple(sum(IN_SIZES[:i]) for i in range(len(IN_SIZES)))
D_IN = sum(IN_SIZES)

ROW_TILE = 512
FF_CHUNK = 256
CHUNK = 128
Q_TILE = 256
KV_TILE = 256
V_ROWS = AT_HD + 16
Q_SCALE = AT_HD ** -0.5 * 1.4426950408889634
FLASH_UNROLL = 4
VMEM_LIMIT = 56 * 1024 * 1024


def _dot(a, b):
    return jnp.dot(a, b, preferred_element_type=F32)


def _dot_nt(a, b):
    return lax.dot_general(a, b, (((1,), (1,)), ((), ())), preferred_element_type=F32)


def _dot_tn(a, b):
    return lax.dot_general(a, b, (((0,), (0,)), ((), ())), preferred_element_type=F32)


def _rms(x, w):
    ms = jnp.mean(x * x, axis=-1, keepdims=True)
    return x * lax.rsqrt(ms + EPS) * w


def _silu(x):
    return x * jax.nn.sigmoid(x)


def _swiglu_half_step(h, norm_w, wg_ref, wu_ref, wd_ref):
    hn = _rms(h, norm_w).astype(BF16)
    acc = None
    for c in range(D_FF // FF_CHUNK):
        sl = slice(c * FF_CHUNK, (c + 1) * FF_CHUNK)
        g = _dot(hn, wg_ref[:, sl])
        u = _dot(hn, wu_ref[:, sl])
        a = (_silu(g) * u).astype(BF16)
        d = _dot(a, wd_ref[sl, :])
        acc = d if acc is None else acc + d
    return h + 0.5 * acc


def _ffn_kernel(h_ref, nw_ref, wg_ref, wu_ref, wd_ref, o_ref):
    o_ref[...] = _swiglu_half_step(h_ref[...], nw_ref[...], wg_ref, wu_ref, wd_ref)


def _const_spec(shape):
    nd = len(shape)
    return pl.BlockSpec(shape, lambda *_: (0,) * nd, pipeline_mode=pl.Buffered(1))


def _ffn_call(h, nw, wg, wu, wd):
    rows = h.shape[0]
    return pl.pallas_call(
        _ffn_kernel,
        out_shape=jax.ShapeDtypeStruct((rows, D_MODEL), F32),
        grid=(rows // ROW_TILE,),
        in_specs=[
            pl.BlockSpec((ROW_TILE, D_MODEL), lambda i: (i, 0)),
            _const_spec((1, D_MODEL)),
            _const_spec((D_MODEL, D_FF)),
            _const_spec((D_MODEL, D_FF)),
            _const_spec((D_FF, D_MODEL)),
        ],
        out_specs=pl.BlockSpec((ROW_TILE, D_MODEL), lambda i: (i, 0)),
        compiler_params=pltpu.CompilerParams(
            dimension_semantics=("arbitrary",), vmem_limit_bytes=VMEM_LIMIT),
        name="ffn1",
    )(h, nw, wg, wu, wd)


def _lower_bound_gate(z, lbp, valid):
    m = jnp.max(lbp, axis=0, keepdims=True)
    e = jnp.exp(lbp - m)
    lb = e[0:1, :] / jnp.sum(e, axis=0, keepdims=True)
    kk = (1.0 - lb) * jax.nn.sigmoid(-z)
    kk = jnp.where(valid, kk, 0.0)
    return kk, jnp.log1p(-kk)


def _head_rms_rope(z, bd_ref, w, cos, sin):
    ms = _dot((z * z).astype(BF16), bd_ref[...])
    zn = z * lax.rsqrt(ms + EPS) * w
    width = z.shape[1]
    lane = lax.broadcasted_iota(jnp.int32, z.shape, 1)
    partner = jnp.where((lane & 1) == 0, pltpu.roll(zn, width - 1, 1), pltpu.roll(zn, 1, 1))
    return zn * cos + partner * sin


def _mix_proj_kernel(h_ref, nw_ref, win_ref, lbf_ref, lbb_ref, qnw_ref, knw_ref, cos_ref, sin_ref,
                     bdq_ref, bdk_ref,
                     hq_ref, hv_ref, kf_ref, lff_ref, kb_ref, lfb_ref, gs_ref, qT_ref, k_ref, vT_ref,
                     ga_ref, gb_ref, *, n_valid):
    i = pl.program_id(0)
    un = _rms(h_ref[...], nw_ref[...]).astype(BF16)

    def proj(piece):
        off, size = IN_OFFS[piece], IN_SIZES[piece]
        return _dot(un, win_ref[:, off:off + size])

    row = i * ROW_TILE + lax.broadcasted_iota(jnp.int32, (ROW_TILE, 1), 0)
    valid = row < n_valid

    hq_ref[...] = _silu(proj(0))
    hv_ref[...] = proj(1).astype(BF16)
    kf, lff = _lower_bound_gate(proj(2), lbf_ref[...], valid)
    kf_ref[...] = kf
    lff_ref[...] = lff
    kb, lfb = _lower_bound_gate(proj(3), lbb_ref[...], valid)
    kb_ref[...] = kb
    lfb_ref[...] = lfb
    gs_ref[...] = _silu(proj(4))

    cos = cos_ref[...]
    sin = sin_ref[...]
    cos_q = jnp.concatenate([cos] * (AT_W // 128), axis=1)
    sin_q = jnp.concatenate([sin] * (AT_W // 128), axis=1)
    q = _head_rms_rope(proj(5), bdq_ref, qnw_ref[...], cos_q, sin_q) * Q_SCALE
    qT_ref[...] = jnp.transpose(q).astype(BF16)
    k_ref[...] = _head_rms_rope(proj(6), bdk_ref, knw_ref[...], cos, sin).astype(BF16)

    vT = jnp.transpose(proj(7)).astype(BF16)
    ones = jnp.ones((V_ROWS - AT_HD, ROW_TILE), BF16)
    for g in range(AT_KV_HEADS):
        vT_ref[g, 0:AT_HD, :] = vT[g * AT_HD:(g + 1) * AT_HD, :]
        vT_ref[g, AT_HD:V_ROWS, :] = ones

    ga_ref[...] = jax.nn.sigmoid(proj(8))
    gb_ref[...] = jax.nn.sigmoid(proj(9))


def _mix_proj_call(h1, nw, win, lbf, lbb, qnw, knw, cos, sin, bdq, bdk, n_valid):
    rows = h1.shape[0]
    row_spec = lambda w: pl.BlockSpec((ROW_TILE, w), lambda i: (i, 0))
    out_shape = [
        jax.ShapeDtypeStruct((rows, HG_W), F32),
        jax.ShapeDtypeStruct((rows, HG_W), BF16),
        jax.ShapeDtypeStruct((rows, HG_W), F32),
        jax.ShapeDtypeStruct((rows, HG_W), F32),
        jax.ShapeDtypeStruct((rows, HG_W), F32),
        jax.ShapeDtypeStruct((rows, HG_W), F32),
        jax.ShapeDtypeStruct((rows, HG_W), F32),
        jax.ShapeDtypeStruct((AT_W, rows), BF16),
        jax.ShapeDtypeStruct((rows, AT_KVW), BF16),
        jax.ShapeDtypeStruct((AT_KV_HEADS, V_ROWS, rows), BF16),
        jax.ShapeDtypeStruct((rows, D_MODEL), F32),
        jax.ShapeDtypeStruct((rows, D_MODEL), F32),
    ]
    out_specs = [
        row_spec(HG_W), row_spec(HG_W), row_spec(HG_W), row_spec(HG_W), row_spec(HG_W), row_spec(HG_W),
        row_spec(HG_W),
        pl.BlockSpec((AT_W, ROW_TILE), lambda i: (0, i)),
        row_spec(AT_KVW),
        pl.BlockSpec((AT_KV_HEADS, V_ROWS, ROW_TILE), lambda i: (0, 0, i)),
        row_spec(D_MODEL), row_spec(D_MODEL),
    ]
    in_specs = [
        row_spec(D_MODEL),
        _const_spec((1, D_MODEL)),
        _const_spec((D_MODEL, D_IN)),
        _const_spec(lbf.shape),
        _const_spec(lbb.shape),
        _const_spec((1, AT_W)),
        _const_spec((1, AT_KVW)),
        row_spec(128),
        row_spec(128),
        _const_spec((AT_W, AT_W)),
        _const_spec((AT_KVW, AT_KVW)),
    ]
    return pl.pallas_call(
        functools.partial(_mix_proj_kernel, n_valid=n_valid),
        out_shape=out_shape,
        grid=(rows // ROW_TILE,),
        in_specs=in_specs,
        out_specs=out_specs,
        compiler_params=pltpu.CompilerParams(
            dimension_semantics=("arbitrary",), vmem_limit_bytes=VMEM_LIMIT),
        name="mix_proj",
    )(h1, nw, win, lbf, lbb, qnw, knw, cos, sin, bdq, bdk)


def _split3(x):
    hi = x.astype(BF16)
    r1 = x - hi.astype(F32)
    mid = r1.astype(BF16)
    lo = (r1 - mid.astype(F32)).astype(BF16)
    return hi, mid, lo


def _hgrn_direction(q_ref, v_ref, k_ref, lf_ref, o_ref, st_ref, reverse):
    c = CHUNK
    q = q_ref[...]
    k = k_ref[...]
    v = v_ref[...]
    lf = lf_ref[...]

    r_i = lax.broadcasted_iota(jnp.int32, (c, c), 0)
    c_i = lax.broadcasted_iota(jnp.int32, (c, c), 1)
    tri = (c_i >= r_i) if reverse else (c_i <= r_i)
    tri = tri.astype(BF16)
    hi, mid, lo = _split3(lf)
    b = _dot(tri, hi) + _dot(tri, mid) + _dot(tri, lo)

    row = lax.broadcasted_iota(jnp.int32, (c, 1), 0)
    up = pltpu.roll(lf, c - 1, 0)
    dn = pltpu.roll(lf, 1, 0)

    ys = []
    valids = []
    for blk in (64, 32, 16, 8, 4, 2, 1):
        sh = blk.bit_length() - 1
        later = ((row >> sh) & 1) == 1
        q_role = jnp.logical_not(later) if reverse else later
        if blk >= 4:
            ref_off = blk if reverse else blk - 1
            pieces = []
            for g in range(c // (2 * blk)):
                r0 = g * 2 * blk + ref_off
                pieces.append(jnp.broadcast_to(b[r0:r0 + 1, :], (2 * blk, HG_W)))
            ref = pieces[0] if len(pieces) == 1 else jnp.concatenate(pieces, axis=0)
            gl = -jnp.abs(b - ref)
        elif blk == 2:
            m4 = row & 3
            if reverse:
                gl = jnp.where(m4 == 0, lf + up, jnp.where(m4 == 1, lf, jnp.where(m4 == 2, 0.0, dn)))
            else:
                gl = jnp.where(m4 == 0, up, jnp.where(m4 == 1, 0.0, jnp.where(m4 == 2, lf, dn + lf)))
        else:
            odd = (row & 1) == 1
            gl = jnp.where(odd, 0.0, lf) if reverse else jnp.where(odd, lf, 0.0)
        ys.append((jnp.where(q_role, q, k) * jnp.exp(gl)).astype(BF16))
        same = (r_i >> (sh + 1)) == (c_i >> (sh + 1))
        t_later = ((r_i >> sh) & 1) == 1
        s_later = ((c_i >> sh) & 1) == 1
        if reverse:
            valids.append(same & jnp.logical_not(t_later) & s_later)
        else:
            valids.append(same & t_later & jnp.logical_not(s_later))

    b_last = b[0:1, :] if reverse else b[c - 1:c, :]
    q_inter = (q * jnp.exp(b)).astype(BF16)
    k_state = (k * jnp.exp(b_last - b)).astype(BF16)
    e_last = jnp.exp(b_last)
    qk = q * k
    eye = r_i == c_i

    for h in range(HG_HEADS):
        hs = slice(h * HG_K, (h + 1) * HG_K)
        a = jnp.where(eye, jnp.sum(qk[:, hs], axis=1, keepdims=True), 0.0)
        for y, valid in zip(ys, valids):
            yh = y[:, hs]
            a = jnp.where(valid, _dot_nt(yh, yh), a)
        st = st_ref[h]
        o = _dot(a.astype(BF16), v[:, hs]) + _dot_nt(q_inter[:, hs], st.astype(BF16))
        o_ref[:, hs] = o
        st_ref[h] = st * e_last[:, hs] + _dot_tn(v[:, hs], k_state[:, hs])


def _hgrn_kernel(qf_ref, vf_ref, kf_ref, lff_ref, qb_ref, vb_ref, kb_ref, lfb_ref,
                 of_ref, ob_ref, sf_ref, sb_ref):
    @pl.when(pl.program_id(0) == 0)
    def _():
        sf_ref[...] = jnp.zeros_like(sf_ref)
        sb_ref[...] = jnp.zeros_like(sb_ref)

    _hgrn_direction(qf_ref, vf_ref, kf_ref, lff_ref, of_ref, sf_ref, reverse=False)
    _hgrn_direction(qb_ref, vb_ref, kb_ref, lfb_ref, ob_ref, sb_ref, reverse=True)


def _hgrn_call(hq, hv, kf, lff, kb, lfb, n_real):
    nb = n_real // CHUNK
    fwd = lambda s: (jnp.where(s == 0, nb, s - 1), 0)
    bwd = lambda s: (jnp.where(s == nb, nb, nb - 1 - s), 0)
    spec = lambda m: pl.BlockSpec((CHUNK, HG_W), m)
    out = jax.ShapeDtypeStruct(((nb + 1) * CHUNK, HG_W), F32)
    return pl.pallas_call(
        _hgrn_kernel,
        out_shape=[out, out],
        grid=(nb + 1,),
        in_specs=[spec(fwd), spec(fwd), spec(fwd), spec(fwd), spec(bwd), spec(bwd), spec(bwd), spec(bwd)],
        out_specs=[spec(fwd), spec(bwd)],
        scratch_shapes=[pltpu.VMEM((HG_HEADS, HG_K, HG_K), F32), pltpu.VMEM((HG_HEADS, HG_K, HG_K), F32)],
        compiler_params=pltpu.CompilerParams(
            dimension_semantics=("arbitrary",), vmem_limit_bytes=VMEM_LIMIT),
        name="hgrn",
    )(hq, hv, kf, lff, hq, hv, kb, lfb)


def _flash_kernel(qT_ref, k_ref, vT_ref, kt_ref, vTt_ref, o_ref, qp_ref, m_ref, acc_ref, s_ref, *,
                  n_kv, n_tail_valid):
    g = pl.program_id(0)
    tq = qT_ref.shape[1]

    half = lax.broadcasted_iota(jnp.int32, (AT_KVW, tq), 0) >> 6
    for h in range(AT_GROUP):
        qh = qT_ref[h * AT_HD:(h + 1) * AT_HD, :].astype(F32)
        q2 = jnp.concatenate([qh, qh], axis=0)
        qp_ref[h] = jnp.where(half == g, q2, 0.0).astype(BF16)

    krow = lax.broadcasted_iota(jnp.int32, (128, 1), 0)
    for h in range(AT_GROUP):
        sT = _dot(kt_ref[...], qp_ref[h])
        sT = jnp.where(krow < n_tail_valid, sT, -jnp.inf)
        m = jnp.max(sT, axis=0, keepdims=True)
        pT = jnp.exp2(sT - m).astype(BF16)
        acc_ref[h] = _dot(vTt_ref[...], pT)
        m_ref[h] = m

    def scores(chunk, slot):
        off = pl.multiple_of(chunk * KV_TILE, KV_TILE)
        kc = k_ref[pl.ds(off, KV_TILE), :]
        for h in range(AT_GROUP):
            s_ref[slot, h] = _dot(kc, qp_ref[h])

    def consume(chunk, slot):
        off = pl.multiple_of(chunk * KV_TILE, KV_TILE)
        vc = vT_ref[:, pl.ds(off, KV_TILE)]
        for h in range(AT_GROUP):
            sT = s_ref[slot, h]
            m_prev = m_ref[h]
            m_new = jnp.maximum(m_prev, jnp.max(sT, axis=0, keepdims=True))
            alpha = jnp.exp2(m_prev - m_new)
            pT = jnp.exp2(sT - m_new).astype(BF16)
            acc_ref[h] = alpha * acc_ref[h] + _dot(vc, pT)
            m_ref[h] = m_new

    scores(0, 0)

    def body(i, carry):
        scores(2 * i + 1, 1)
        consume(2 * i, 0)
        scores(jnp.minimum(2 * i + 2, n_kv - 1), 0)
        consume(2 * i + 1, 1)
        return carry

    lax.fori_loop(0, n_kv // 2, body, 0, unroll=FLASH_UNROLL)

    outs = []
    for h in range(AT_GROUP):
        acc = acc_ref[h]
        outs.append(acc[0:AT_HD, :] / acc[AT_HD:AT_HD + 1, :])
    o_ref[...] = jnp.transpose(jnp.concatenate(outs, axis=0)).astype(BF16)


def _flash_call(qT, k, vT, n_real):
    n_kv = n_real // KV_TILE
    tail_blk = n_real // 128
    gw = AT_GROUP * AT_HD
    return pl.pallas_call(
        functools.partial(_flash_kernel, n_kv=n_kv, n_tail_valid=N_META),
        out_shape=jax.ShapeDtypeStruct((n_real, AT_W), BF16),
        grid=(AT_KV_HEADS, n_real // Q_TILE),
        in_specs=[
            pl.BlockSpec((gw, Q_TILE), lambda g, i: (g, i)),
            pl.BlockSpec((n_real, AT_KVW), lambda g, i: (0, 0)),
            pl.BlockSpec((None, V_ROWS, n_real), lambda g, i: (g, 0, 0)),
            pl.BlockSpec((128, AT_KVW), lambda g, i: (tail_blk, 0)),
            pl.BlockSpec((None, V_ROWS, 128), lambda g, i: (g, 0, tail_blk)),
        ],
        out_specs=pl.BlockSpec((Q_TILE, gw), lambda g, i: (i, g)),
        scratch_shapes=[
            pltpu.VMEM((AT_GROUP, AT_KVW, Q_TILE), BF16),
            pltpu.VMEM((AT_GROUP, 1, Q_TILE), F32),
            pltpu.VMEM((AT_GROUP, V_ROWS, Q_TILE), F32),
            pltpu.VMEM((2, AT_GROUP, KV_TILE, Q_TILE), F32),
        ],
        compiler_params=pltpu.CompilerParams(
            dimension_semantics=("arbitrary", "arbitrary"), vmem_limit_bytes=VMEM_LIMIT),
        name="flash",
    )(qT, k, vT, k, vT)


def _merge_ffn_kernel(h1_ref, of_ref, ob_ref, gs_ref, yb_ref, ga_ref, gb_ref, hgw_ref,
                      wua_ref, wub_ref, wout_ref, nw_ref, wg_ref, wu_ref, wd_ref, o_ref):
    o = of_ref[...] + ob_ref[...]
    normed = []
    for h in range(HG_HEADS):
        oh = o[:, h * HG_K:(h + 1) * HG_K]
        normed.append(oh * lax.rsqrt(jnp.mean(oh * oh, axis=-1, keepdims=True) + EPS))
    ya = (jnp.concatenate(normed, axis=1) * hgw_ref[...] * gs_ref[...]).astype(BF16)
    mixed = ga_ref[...] * _dot(ya, wua_ref[...]) + gb_ref[...] * _dot(yb_ref[...], wub_ref[...])
    h2 = h1_ref[...] + _dot(mixed.astype(BF16), wout_ref[...])
    o_ref[...] = _swiglu_half_step(h2, nw_ref[...], wg_ref, wu_ref, wd_ref)


def _merge_ffn_call(h1, o_f, o_b, gs, yb, ga, gb, hgw, wua, wub, wout, nw, wg, wu, wd, n_real):
    row_spec = lambda w: pl.BlockSpec((ROW_TILE, w), lambda i: (i, 0))
    return pl.pallas_call(
        _merge_ffn_kernel,
        out_shape=jax.ShapeDtypeStruct((n_real, D_MODEL), F32),
        grid=(n_real // ROW_TILE,),
        in_specs=[
            row_spec(D_MODEL), row_spec(HG_W), row_spec(HG_W), row_spec(HG_W), row_spec(AT_W),
            row_spec(D_MODEL), row_spec(D_MODEL),
            _const_spec((1, HG_W)),
            _const_spec((HG_W, D_MODEL)),
            _const_spec((AT_W, D_MODEL)),
            _const_spec((D_MODEL, D_MODEL)),
            _const_spec((1, D_MODEL)),
            _const_spec((D_MODEL, D_FF)),
            _const_spec((D_MODEL, D_FF)),
            _const_spec((D_FF, D_MODEL)),
        ],
        out_specs=row_spec(D_MODEL),
        compiler_params=pltpu.CompilerParams(
            dimension_semantics=("arbitrary",), vmem_limit_bytes=VMEM_LIMIT),
        name="merge_ffn2",
    )(h1, o_f, o_b, gs, yb, ga, gb, hgw, wua, wub, wout, nw, wg, wu, wd)


def _rope_tables(n_real, rows):
    half = AT_HD // 2
    grid_rows = n_real // GRID_W
    r = jnp.repeat(jnp.arange(grid_rows, dtype=F32), GRID_W)
    c = jnp.tile(jnp.arange(GRID_W, dtype=F32), grid_rows)
    zeros = jnp.zeros((rows - n_real,), F32)
    r = jnp.concatenate([r, zeros])
    c = jnp.concatenate([c, zeros])
    inv = ROPE_THETA ** (-jnp.arange(0, half, 2, dtype=F32) / half)
    ang = jnp.concatenate([r[:, None] * inv, c[:, None] * inv], axis=-1)
    cos = jnp.repeat(jnp.cos(ang), 2, axis=-1)
    sin = jnp.repeat(jnp.sin(ang), 2, axis=-1) * jnp.tile(jnp.array([-1.0, 1.0], F32), half)
    reps = 128 // AT_HD
    return jnp.tile(cos, (1, reps)), jnp.tile(sin, (1, reps))


def _head_mean_matrix(width):
    heads = width // AT_HD
    return jnp.kron(jnp.eye(heads, dtype=F32), jnp.full((AT_HD, AT_HD), 1.0 / AT_HD, F32)).astype(BF16)


def kernel(x, meta_tokens, ffn1_norm, ffn1_w_gate, ffn1_w_up, ffn1_w_down, mix_norm, w_in, hg_lb_fwd, hg_lb_bwd, hg_out_norm, q_norm, k_norm, w_up_a, w_up_b, w_out, ffn2_norm, ffn2_w_gate, ffn2_w_up, ffn2_w_down):
    batch, n_real, _ = x.shape
    assert batch == 1 and n_real % ROW_TILE == 0 and n_real % GRID_W == 0
    rows = n_real + ROW_TILE
    n_valid = n_real + N_META

    h0 = jnp.concatenate(
        [x[0], meta_tokens.astype(x.dtype), jnp.zeros((ROW_TILE - N_META, D_MODEL), x.dtype)], axis=0)
    bf = lambda w: w.astype(BF16)
    row = lambda w: w.reshape(1, -1).astype(F32)

    h1 = _ffn_call(h0, row(ffn1_norm[0]), bf(ffn1_w_gate[0]), bf(ffn1_w_up[0]), bf(ffn1_w_down[0]))

    cos, sin = _rope_tables(n_real, rows)
    hq, hv, kf, lff, kb, lfb, gs, qT, k, vT, ga, gb = _mix_proj_call(
        h1, row(mix_norm[0]), bf(w_in[0]), hg_lb_fwd.astype(F32), hg_lb_bwd.astype(F32),
        row(jnp.tile(q_norm[0], AT_HEADS)), row(jnp.tile(k_norm[0], AT_KV_HEADS)), cos, sin,
        _head_mean_matrix(AT_W), _head_mean_matrix(AT_KVW), n_valid)

    o_f, o_b = _hgrn_call(hq, hv, kf, lff, kb, lfb, n_real)
    yb = _flash_call(qT, k, vT, n_real)

    out = _merge_ffn_call(
        h1, o_f, o_b, gs, yb, ga, gb, row(hg_out_norm[0]), bf(w_up_a[0]), bf(w_up_b[0]), bf(w_out[0]),
        row(ffn2_norm[0]), bf(ffn2_w_gate[0]), bf(ffn2_w_up[0]), bf(ffn2_w_down[0]), n_real)
    return out.reshape(batch, n_real, D_MODEL)
```

```python
import functools

import jax
import jax.numpy as jnp
from jax import lax
from jax.experimental import pallas as pl
from jax.experimental.pallas import tpu as pltpu

F32 = jnp.float32
BF16 = jnp.bfloat16

D_MODEL = 1024
D_FF = 2816
N_META = 16
GRID_W = 64
EPS = 1e-6
HG_HEADS = 4
HG_K = 128
HG_W = HG_HEADS * HG_K
AT_HEADS = 8
AT_KV_HEADS = 2
AT_HD = 64
AT_GROUP = AT_HEADS // AT_KV_HEADS
AT_W = AT_HEADS * AT_HD
AT_KVW = AT_KV_HEADS * AT_HD
ROPE_THETA = 10000.0
IN_SIZES = (HG_W, HG_W, HG_W, HG_W, HG_W, AT_W, AT_KVW, AT_KVW, D_MODEL, D_MODEL)
IN_OFFS = tuple(sum(IN_SIZES[:i]) for i in range(len(IN_SIZES)))
D_IN = sum(IN_SIZES)

ROW_TILE = 512
FF_CHUNK = 256
CHUNK = 128
Q_TILE = 256
KV_TILE = 256
V_ROWS = AT_HD + 16
LOG2E = 1.4426950408889634
Q_SCALE = AT_HD ** -0.5 * LOG2E
FLASH_UNROLL = 8
VMEM_LIMIT = 56 * 1024 * 1024


def _dot(a, b):
    return jnp.dot(a, b, preferred_element_type=F32)


def _dot_nt(a, b):
    return lax.dot_general(a, b, (((1,), (1,)), ((), ())), preferred_element_type=F32)


def _dot_tn(a, b):
    return lax.dot_general(a, b, (((0,), (0,)), ((), ())), preferred_element_type=F32)


def _rms(x, w):
    ms = jnp.mean(x * x, axis=-1, keepdims=True)
    return x * lax.rsqrt(ms + EPS) * w


def _sigmoid(x):
    return 0.5 * jnp.tanh(0.5 * x) + 0.5


def _silu(x):
    h = 0.5 * x
    return h * jnp.tanh(h) + h


def _swiglu_half_step(h, norm_w, wg_ref, wu_ref, wd_ref):
    hn = _rms(h, norm_w).astype(BF16)
    acc = None
    pending = None
    for c in range(D_FF // FF_CHUNK + 1):
        if c < D_FF // FF_CHUNK:
            sl = slice(c * FF_CHUNK, (c + 1) * FF_CHUNK)
            g = _dot(hn, wg_ref[:, sl])
            u = _dot(hn, wu_ref[:, sl])
        if pending is not None:
            d = _dot(pending[0], wd_ref[pending[1], :])
            acc = d if acc is None else acc + d
        if c < D_FF // FF_CHUNK:
            pending = ((_silu(g) * u).astype(BF16), sl)
    return h + 0.5 * acc


def _ffn_kernel(x_ref, tail_ref, nw_ref, wg_ref, wu_ref, wd_ref, o_ref, *, n_x_tiles):
    h = jnp.where(pl.program_id(0) < n_x_tiles, x_ref[...], tail_ref[...])
    o_ref[...] = _swiglu_half_step(h, nw_ref[...], wg_ref, wu_ref, wd_ref)


def _const_spec(shape):
    nd = len(shape)
    return pl.BlockSpec(shape, lambda *_: (0,) * nd, pipeline_mode=pl.Buffered(1))


def _ffn_call(x, tail, nw, wg, wu, wd):
    n_x_tiles = x.shape[0] // ROW_TILE
    rows = x.shape[0] + tail.shape[0]
    return pl.pallas_call(
        functools.partial(_ffn_kernel, n_x_tiles=n_x_tiles),
        out_shape=jax.ShapeDtypeStruct((rows, D_MODEL), F32),
        grid=(n_x_tiles + 1,),
        in_specs=[
            pl.BlockSpec((ROW_TILE, D_MODEL), lambda i: (jnp.minimum(i, n_x_tiles - 1), 0)),
            _const_spec((ROW_TILE, D_MODEL)),
            _const_spec((1, D_MODEL)),
            _const_spec((D_MODEL, D_FF)),
            _const_spec((D_MODEL, D_FF)),
            _const_spec((D_FF, D_MODEL)),
        ],
        out_specs=pl.BlockSpec((ROW_TILE, D_MODEL), lambda i: (i, 0)),
        compiler_params=pltpu.CompilerParams(
            dimension_semantics=("arbitrary",), vmem_limit_bytes=VMEM_LIMIT),
        name="ffn1",
    )(x, tail, nw, wg, wu, wd)


def _lower_bound_gate(z, lbp, valid):
    m = jnp.max(lbp, axis=0, keepdims=True)
    e = jnp.exp(lbp - m)
    lb = e[0:1, :] / jnp.sum(e, axis=0, keepdims=True)
    kk = (1.0 - lb) * _sigmoid(-z)
    kk = jnp.where(valid, kk, 0.0)
    return kk, jnp.log2(1.0 - kk)


def _head_rms_rope(z, ms, w, cos, sin):
    zn = z * lax.rsqrt(ms + EPS) * w
    width = z.shape[1]
    lane = lax.broadcasted_iota(jnp.int32, z.shape, 1)
    partner = jnp.where((lane & 1) == 0, pltpu.roll(zn, width - 1, 1), pltpu.roll(zn, 1, 1))
    return zn * cos + partner * sin


def _mix_proj_kernel(h_ref, nw_ref, win_ref, lbf_ref, lbb_ref, qnw_ref, knw_ref, cos_ref, sin_ref,
                     bdq_ref, bdk_ref,
                     hq_ref, hv_ref, kf_ref, lff_ref, kb_ref, lfb_ref, gs_ref, qT_ref, k_ref, vT_ref,
                     ga_ref, gb_ref, *, n_valid):
    i = pl.program_id(0)
    un = _rms(h_ref[...], nw_ref[...]).astype(BF16)

    def proj(piece):
        off, size = IN_OFFS[piece], IN_SIZES[piece]
        return _dot(un, win_ref[:, off:off + size])

    row = i * ROW_TILE + lax.broadcasted_iota(jnp.int32, (ROW_TILE, 1), 0)
    valid = row < n_valid

    zq = proj(5)
    zk = proj(6)
    cos = cos_ref[...]
    sin = sin_ref[...]
    cos_q = jnp.concatenate([cos] * (AT_W // 128), axis=1)
    sin_q = jnp.concatenate([sin] * (AT_W // 128), axis=1)
    ms_q = _dot((zq * zq).astype(BF16), bdq_ref[...])
    ms_k = _dot((zk * zk).astype(BF16), bdk_ref[...])
    q = _head_rms_rope(zq, ms_q, qnw_ref[...], cos_q, sin_q) * Q_SCALE
    qT_ref[...] = jnp.transpose(q).astype(BF16)
    k_ref[...] = _head_rms_rope(zk, ms_k, knw_ref[...], cos, sin).astype(BF16)

    hq_ref[...] = _silu(proj(0))
    hv_ref[...] = proj(1).astype(BF16)
    kf, lff = _lower_bound_gate(proj(2), lbf_ref[...], valid)
    kf_ref[...] = kf
    lff_ref[...] = lff
    kb, lfb = _lower_bound_gate(proj(3), lbb_ref[...], valid)
    kb_ref[...] = kb
    lfb_ref[...] = lfb
    gs_ref[...] = _silu(proj(4))

    vT = jnp.transpose(proj(7)).astype(BF16)
    ones = jnp.ones((V_ROWS - AT_HD, ROW_TILE), BF16)
    for g in range(AT_KV_HEADS):
        vT_ref[g, 0:AT_HD, :] = vT[g * AT_HD:(g + 1) * AT_HD, :]
        vT_ref[g, AT_HD:V_ROWS, :] = ones

    ga_ref[...] = _sigmoid(proj(8))
    gb_ref[...] = _sigmoid(proj(9))


def _mix_proj_call(h1, nw, win, lbf, lbb, qnw, knw, cos, sin, bdq, bdk, n_valid):
    rows = h1.shape[0]
    row_spec = lambda w: pl.BlockSpec((ROW_TILE, w), lambda i: (i, 0))
    out_shape = [
        jax.ShapeDtypeStruct((rows, HG_W), F32),
        jax.ShapeDtypeStruct((rows, HG_W), BF16),
        jax.ShapeDtypeStruct((rows, HG_W), F32),
        jax.ShapeDtypeStruct((rows, HG_W), F32),
        jax.ShapeDtypeStruct((rows, HG_W), F32),
        jax.ShapeDtypeStruct((rows, HG_W), F32),
        jax.ShapeDtypeStruct((rows, HG_W), F32),
        jax.ShapeDtypeStruct((AT_W, rows), BF16),
        jax.ShapeDtypeStruct((rows, AT_KVW), BF16),
        jax.ShapeDtypeStruct((AT_KV_HEADS, V_ROWS, rows), BF16),
        jax.ShapeDtypeStruct((rows, D_MODEL), F32),
        jax.ShapeDtypeStruct((rows, D_MODEL), F32),
    ]
    out_specs = [
        row_spec(HG_W), row_spec(HG_W), row_spec(HG_W), row_spec(HG_W), row_spec(HG_W), row_spec(HG_W),
        row_spec(HG_W),
        pl.BlockSpec((AT_W, ROW_TILE), lambda i: (0, i)),
        row_spec(AT_KVW),
        pl.BlockSpec((AT_KV_HEADS, V_ROWS, ROW_TILE), lambda i: (0, 0, i)),
        row_spec(D_MODEL), row_spec(D_MODEL),
    ]
    in_specs = [
        row_spec(D_MODEL),
        _const_spec((1, D_MODEL)),
        _const_spec((D_MODEL, D_IN)),
        _const_spec(lbf.shape),
        _const_spec(lbb.shape),
        _const_spec((1, AT_W)),
        _const_spec((1, AT_KVW)),
        row_spec(128),
        row_spec(128),
        _const_spec((AT_W, AT_W)),
        _const_spec((AT_KVW, AT_KVW)),
    ]
    return pl.pallas_call(
        functools.partial(_mix_proj_kernel, n_valid=n_valid),
        out_shape=out_shape,
        grid=(rows // ROW_TILE,),
        in_specs=in_specs,
        out_specs=out_specs,
        compiler_params=pltpu.CompilerParams(
            dimension_semantics=("arbitrary",), vmem_limit_bytes=VMEM_LIMIT),
        name="mix_proj",
    )(h1, nw, win, lbf, lbb, qnw, knw, cos, sin, bdq, bdk)


def _split2(x):
    hi = x.astype(BF16)
    lo = (x - hi.astype(F32)).astype(BF16)
    return hi, lo


def _level_operand(q, k, b, lf, blk, reverse):
    c = CHUNK
    if blk >= 8:
        pieces = []
        for g in range(c // (2 * blk)):
            r0 = g * 2 * blk
            lo, hi = slice(r0, r0 + blk), slice(r0 + blk, r0 + 2 * blk)
            if reverse:
                ref = b[r0 + blk:r0 + blk + 1, :]
                pieces.append(q[lo] * jnp.exp2(b[lo] - ref))
                pieces.append(k[hi] * jnp.exp2(ref - b[hi]))
            else:
                ref = b[r0 + blk - 1:r0 + blk, :]
                pieces.append(k[lo] * jnp.exp2(ref - b[lo]))
                pieces.append(q[hi] * jnp.exp2(b[hi] - ref))
        return jnp.concatenate(pieces, axis=0).astype(BF16)

    q3, k3, b3, lf3 = (x.reshape(c // 8, 8, HG_W) for x in (q, k, b, lf))
    sub = lax.broadcasted_iota(jnp.int32, (1, 8, 1), 1)
    later = ((sub >> (blk.bit_length() - 1)) & 1) == 1
    q_role = jnp.logical_not(later) if reverse else later
    if blk == 4:
        ref = b3[:, 4:5, :] if reverse else b3[:, 3:4, :]
        gl = -jnp.abs(b3 - ref)
    elif blk == 2:
        up = pltpu.roll(lf3, 7, 1)
        dn = pltpu.roll(lf3, 1, 1)
        m4 = sub & 3
        if reverse:
            gl = jnp.where(m4 == 0, lf3 + up, jnp.where(m4 == 1, lf3, jnp.where(m4 == 2, 0.0, dn)))
        else:
            gl = jnp.where(m4 == 0, up, jnp.where(m4 == 1, 0.0, jnp.where(m4 == 2, lf3, dn + lf3)))
    else:
        odd = (sub & 1) == 1
        gl = jnp.where(odd, 0.0, lf3) if reverse else jnp.where(odd, lf3, 0.0)
    y3 = jnp.where(q_role, q3, k3) * jnp.exp2(gl)
    return y3.reshape(c, HG_W).astype(BF16)


def _hgrn_direction(q_ref, v_ref, k_ref, lf_ref, o_ref, st_ref, reverse):
    c = CHUNK
    q = q_ref[...]
    k = k_ref[...]
    v = v_ref[...]
    lf = lf_ref[...]

    r_i = lax.broadcasted_iota(jnp.int32, (c, c), 0)
    c_i = lax.broadcasted_iota(jnp.int32, (c, c), 1)
    tri = ((c_i >= r_i) if reverse else (c_i <= r_i)).astype(BF16)
    hi, lo = _split2(lf)
    b = _dot(jnp.concatenate([tri, tri], axis=1), jnp.concatenate([hi, lo], axis=0))

    ys = []
    valids = []
    for blk in (64, 32, 16, 8, 4, 2, 1):
        sh = blk.bit_length() - 1
        ys.append(_level_operand(q, k, b, lf, blk, reverse))
        same = (r_i >> (sh + 1)) == (c_i >> (sh + 1))
        t_later = ((r_i >> sh) & 1) == 1
        s_later = ((c_i >> sh) & 1) == 1
        if reverse:
            valids.append(same & jnp.logical_not(t_later) & s_later)
        else:
            valids.append(same & t_later & jnp.logical_not(s_later))

    b_last = b[0:1, :] if reverse else b[c - 1:c, :]
    q_inter = (q * jnp.exp2(b)).astype(BF16)
    k_state = (k * jnp.exp2(b_last - b)).astype(BF16)
    e_last = jnp.exp2(b_last)
    qk = q * k
    eye = r_i == c_i

    def issue(h):
        hs = slice(h * HG_K, (h + 1) * HG_K)
        products = [_dot_nt(y[:, hs], y[:, hs]) for y in ys]
        st = st_ref[h]
        inter = _dot_nt(q_inter[:, hs], st.astype(BF16))
        st_ref[h] = st * e_last[:, hs] + _dot_tn(v[:, hs], k_state[:, hs])
        return products, inter

    def finish(h, issued):
        products, inter = issued
        hs = slice(h * HG_K, (h + 1) * HG_K)
        a = jnp.where(eye, jnp.sum(qk[:, hs], axis=1, keepdims=True), 0.0)
        for p, valid in zip(products, valids):
            a = jnp.where(valid, p, a)
        o_ref[:, hs] = _dot(a.astype(BF16), v[:, hs]) + inter

    return issue, finish


def _hgrn_kernel(qf_ref, vf_ref, kf_ref, lff_ref, qb_ref, vb_ref, kb_ref, lfb_ref,
                 of_ref, ob_ref, sf_ref, sb_ref):
    @pl.when(pl.program_id(0) == 0)
    def _():
        sf_ref[...] = jnp.zeros_like(sf_ref)
        sb_ref[...] = jnp.zeros_like(sb_ref)

    units = []
    for args in ((qf_ref, vf_ref, kf_ref, lff_ref, of_ref, sf_ref, False),
                 (qb_ref, vb_ref, kb_ref, lfb_ref, ob_ref, sb_ref, True)):
        issue, finish = _hgrn_direction(*args)
        units += [(issue, finish, h) for h in range(HG_HEADS)]

    pending = None
    for issue, finish, h in units:
        issued = issue(h)
        if pending is not None:
            pending[0](pending[1], pending[2])
        pending = (finish, h, issued)
    pending[0](pending[1], pending[2])


def _hgrn_call(hq, hv, kf, lff, kb, lfb, n_real):
    nb = n_real // CHUNK
    fwd = lambda s: (jnp.where(s == 0, nb, s - 1), 0)
    bwd = lambda s: (jnp.where(s == nb, nb, nb - 1 - s), 0)
    spec = lambda m: pl.BlockSpec((CHUNK, HG_W), m)
    out = jax.ShapeDtypeStruct(((nb + 1) * CHUNK, HG_W), F32)
    return pl.pallas_call(
        _hgrn_kernel,
        out_shape=[out, out],
        grid=(nb + 1,),
        in_specs=[spec(fwd), spec(fwd), spec(fwd), spec(fwd), spec(bwd), spec(bwd), spec(bwd), spec(bwd)],
        out_specs=[spec(fwd), spec(bwd)],
        scratch_shapes=[pltpu.VMEM((HG_HEADS, HG_K, HG_K), F32), pltpu.VMEM((HG_HEADS, HG_K, HG_K), F32)],
        compiler_params=pltpu.CompilerParams(
            dimension_semantics=("arbitrary",), vmem_limit_bytes=VMEM_LIMIT),
        name="hgrn",
    )(hq, hv, kf, lff, hq, hv, kb, lfb)


def _flash_kernel(qT_ref, k_ref, vT_ref, kt_ref, vTt_ref, o_ref, qp_ref, m_ref, acc_ref, s_ref, mc_ref,
                  st_ref, mct_ref, *,
                  n_kv, n_tail_valid):
    g = pl.program_id(0)
    tq = qT_ref.shape[1]

    half = lax.broadcasted_iota(jnp.int32, (AT_KVW, tq), 0) >> 6
    for h in range(AT_GROUP):
        qh = qT_ref[h * AT_HD:(h + 1) * AT_HD, :].astype(F32)
        q2 = jnp.concatenate([qh, qh], axis=0)
        qp_ref[h] = jnp.where(half == g, q2, 0.0).astype(BF16)

    m_ref[...] = jnp.full(m_ref.shape, -jnp.inf, F32)
    acc_ref[...] = jnp.zeros(acc_ref.shape, F32)

    def store_scores(sT, s_dst, mc_dst, h):
        s_dst[h] = sT
        mc_dst[h] = jnp.max(sT, axis=0, keepdims=True)

    def scores(chunk, slot):
        off = pl.multiple_of(chunk * KV_TILE, KV_TILE)
        kc = k_ref[pl.ds(off, KV_TILE), :]
        for h in range(AT_GROUP):
            store_scores(_dot(kc, qp_ref[h]), s_ref.at[slot], mc_ref.at[slot], h)

    def consume(vc, s_src, mc_src):
        for h in range(AT_GROUP):
            m_prev = m_ref[h]
            m_new = jnp.maximum(m_prev, mc_src[h])
            alpha = jnp.exp2(m_prev - m_new)
            pT = jnp.exp2(s_src[h] - m_new).astype(BF16)
            acc_ref[h] = alpha * acc_ref[h] + _dot(vc, pT)
            m_ref[h] = m_new

    def consume_chunk(chunk, slot):
        off = pl.multiple_of(chunk * KV_TILE, KV_TILE)
        consume(vT_ref[:, pl.ds(off, KV_TILE)], s_ref.at[slot], mc_ref.at[slot])

    krow = lax.broadcasted_iota(jnp.int32, (128, 1), 0)
    for h in range(AT_GROUP):
        sT = jnp.where(krow < n_tail_valid, _dot(kt_ref[...], qp_ref[h]), -jnp.inf)
        store_scores(sT, st_ref, mct_ref, h)
    scores(0, 0)
    consume(vTt_ref[...], st_ref, mct_ref)

    def body(i, carry):
        scores(2 * i + 1, 1)
        consume_chunk(2 * i, 0)
        scores(jnp.minimum(2 * i + 2, n_kv - 1), 0)
        consume_chunk(2 * i + 1, 1)
        return carry

    lax.fori_loop(0, n_kv // 2, body, 0, unroll=FLASH_UNROLL)

    outs = []
    for h in range(AT_GROUP):
        acc = acc_ref[h]
        outs.append(acc[0:AT_HD, :] / acc[AT_HD:AT_HD + 1, :])
    o_ref[...] = jnp.transpose(jnp.concatenate(outs, axis=0)).astype(BF16)


def _flash_call(qT, k, vT, n_real):
    n_kv = n_real // KV_TILE
    tail_blk = n_real // 128
    gw = AT_GROUP * AT_HD
    return pl.pallas_call(
        functools.partial(_flash_kernel, n_kv=n_kv, n_tail_valid=N_META),
        out_shape=jax.ShapeDtypeStruct((n_real, AT_W), BF16),
        grid=(AT_KV_HEADS, n_real // Q_TILE),
        in_specs=[
            pl.BlockSpec((gw, Q_TILE), lambda g, i: (g, i)),
            pl.BlockSpec((n_real, AT_KVW), lambda g, i: (0, 0)),
            pl.BlockSpec((None, V_ROWS, n_real), lambda g, i: (g, 0, 0)),
            pl.BlockSpec((128, AT_KVW), lambda g, i: (tail_blk, 0)),
            pl.BlockSpec((None, V_ROWS, 128), lambda g, i: (g, 0, tail_blk)),
        ],
        out_specs=pl.BlockSpec((Q_TILE, gw), lambda g, i: (i, g)),
        scratch_shapes=[
            pltpu.VMEM((AT_GROUP, AT_KVW, Q_TILE), BF16),
            pltpu.VMEM((AT_GROUP, 1, Q_TILE), F32),
            pltpu.VMEM((AT_GROUP, V_ROWS, Q_TILE), F32),
            pltpu.VMEM((2, AT_GROUP, KV_TILE, Q_TILE), F32),
            pltpu.VMEM((2, AT_GROUP, 1, Q_TILE), F32),
            pltpu.VMEM((AT_GROUP, 128, Q_TILE), F32),
            pltpu.VMEM((AT_GROUP, 1, Q_TILE), F32),
        ],
        compiler_params=pltpu.CompilerParams(
            dimension_semantics=("arbitrary", "arbitrary"), vmem_limit_bytes=VMEM_LIMIT),
        name="flash",
    )(qT, k, vT, k, vT)


def _merge_ffn_kernel(h1_ref, of_ref, ob_ref, gs_ref, yb_ref, ga_ref, gb_ref, hgw_ref,
                      wua_ref, wub_ref, wout_ref, nw_ref, wg_ref, wu_ref, wd_ref, o_ref):
    o = of_ref[...] + ob_ref[...]
    normed = []
    for h in range(HG_HEADS):
        oh = o[:, h * HG_K:(h + 1) * HG_K]
        normed.append(oh * lax.rsqrt(jnp.mean(oh * oh, axis=-1, keepdims=True) + EPS))
    ya = (jnp.concatenate(normed, axis=1) * hgw_ref[...] * gs_ref[...]).astype(BF16)
    mixed = ga_ref[...] * _dot(ya, wua_ref[...]) + gb_ref[...] * _dot(yb_ref[...], wub_ref[...])
    h2 = h1_ref[...] + _dot(mixed.astype(BF16), wout_ref[...])
    o_ref[...] = _swiglu_half_step(h2, nw_ref[...], wg_ref, wu_ref, wd_ref)


def _merge_ffn_call(h1, o_f, o_b, gs, yb, ga, gb, hgw, wua, wub, wout, nw, wg, wu, wd, n_real):
    row_spec = lambda w: pl.BlockSpec((ROW_TILE, w), lambda i: (i, 0))
    return pl.pallas_call(
        _merge_ffn_kernel,
        out_shape=jax.ShapeDtypeStruct((n_real, D_MODEL), F32),
        grid=(n_real // ROW_TILE,),
        in_specs=[
            row_spec(D_MODEL), row_spec(HG_W), row_spec(HG_W), row_spec(HG_W), row_spec(AT_W),
            row_spec(D_MODEL), row_spec(D_MODEL),
            _const_spec((1, HG_W)),
            _const_spec((HG_W, D_MODEL)),
            _const_spec((AT_W, D_MODEL)),
            _const_spec((D_MODEL, D_MODEL)),
            _const_spec((1, D_MODEL)),
            _const_spec((D_MODEL, D_FF)),
            _const_spec((D_MODEL, D_FF)),
            _const_spec((D_FF, D_MODEL)),
        ],
        out_specs=row_spec(D_MODEL),
        compiler_params=pltpu.CompilerParams(
            dimension_semantics=("arbitrary",), vmem_limit_bytes=VMEM_LIMIT),
        name="merge_ffn2",
    )(h1, o_f, o_b, gs, yb, ga, gb, hgw, wua, wub, wout, nw, wg, wu, wd)


def _rope_tables(n_real, rows):
    half = AT_HD // 2
    grid_rows = n_real // GRID_W
    r = jnp.repeat(jnp.arange(grid_rows, dtype=F32), GRID_W)
    c = jnp.tile(jnp.arange(GRID_W, dtype=F32), grid_rows)
    zeros = jnp.zeros((rows - n_real,), F32)
    r = jnp.concatenate([r, zeros])
    c = jnp.concatenate([c, zeros])
    inv = ROPE_THETA ** (-jnp.arange(0, half, 2, dtype=F32) / half)
    ang = jnp.concatenate([r[:, None] * inv, c[:, None] * inv], axis=-1)
    cos = jnp.repeat(jnp.cos(ang), 2, axis=-1)
    sin = jnp.repeat(jnp.sin(ang), 2, axis=-1) * jnp.tile(jnp.array([-1.0, 1.0], F32), half)
    reps = 128 // AT_HD
    return jnp.tile(cos, (1, reps)), jnp.tile(sin, (1, reps))


def _head_mean_matrix(width):
    heads = width // AT_HD
    return jnp.kron(jnp.eye(heads, dtype=F32), jnp.full((AT_HD, AT_HD), 1.0 / AT_HD, F32)).astype(BF16)


def kernel(x, meta_tokens, ffn1_norm, ffn1_w_gate, ffn1_w_up, ffn1_w_down, mix_norm, w_in, hg_lb_fwd, hg_lb_bwd, hg_out_norm, q_norm, k_norm, w_up_a, w_up_b, w_out, ffn2_norm, ffn2_w_gate, ffn2_w_up, ffn2_w_down):
    batch, n_real, _ = x.shape
    assert batch == 1 and n_real % ROW_TILE == 0 and n_real % GRID_W == 0
    rows = n_real + ROW_TILE
    n_valid = n_real + N_META

    tail = jnp.concatenate(
        [meta_tokens.astype(x.dtype), jnp.zeros((ROW_TILE - N_META, D_MODEL), x.dtype)], axis=0)
    bf = lambda w: w.astype(BF16)
    row = lambda w: w.reshape(1, -1).astype(F32)

    h1 = _ffn_call(x[0], tail, row(ffn1_norm[0]), bf(ffn1_w_gate[0]), bf(ffn1_w_up[0]), bf(ffn1_w_down[0]))

    cos, sin = _rope_tables(n_real, rows)
    hq, hv, kf, lff, kb, lfb, gs, qT, k, vT, ga, gb = _mix_proj_call(
        h1, row(mix_norm[0]), bf(w_in[0]), hg_lb_fwd.astype(F32), hg_lb_bwd.astype(F32),
        row(jnp.tile(q_norm[0], AT_HEADS)), row(jnp.tile(k_norm[0], AT_KV_HEADS)), cos, sin,
        _head_mean_matrix(AT_W), _head_mean_matrix(AT_KVW), n_valid)

    o_f, o_b = _hgrn_call(hq, hv, kf, lff, kb, lfb, n_real)
    yb = _flash_call(qT, k, vT, n_real)

    out = _merge_ffn_call(
        h1, o_f, o_b, gs, yb, ga, gb, row(hg_out_norm[0]), bf(w_up_a[0]), bf(w_up_b[0]), bf(w_out[0]),
        row(ffn2_norm[0]), bf(ffn2_w_gate[0]), bf(ffn2_w_up[0]), bf(ffn2_w_down[0]), n_real)
    return out.reshape(batch, n_real, D_MODEL)
```

```python
import functools

import jax
import jax.numpy as jnp
from jax import lax
from jax.experimental import pallas as pl
from jax.experimental.pallas import tpu as pltpu

F32 = jnp.float32
BF16 = jnp.bfloat16

D_MODEL = 1024
D_FF = 2816
N_META = 16
GRID_W = 64
EPS = 1e-6
HG_HEADS = 4
HG_K = 128
HG_W = HG_HEADS * HG_K
AT_HEADS = 8
AT_KV_HEADS = 2
AT_HD = 64
AT_GROUP = AT_HEADS // AT_KV_HEADS
AT_W = AT_HEADS * AT_HD
AT_KVW = AT_KV_HEADS * AT_HD
ROPE_THETA = 10000.0
IN_SIZES = (HG_W, HG_W, HG_W, HG_W, HG_W, AT_W, AT_KVW, AT_KVW, D_MODEL, D_MODEL)
IN_OFFS = tuple(sum(IN_SIZES[:i]) for i in range(len(IN_SIZES)))
D_IN = sum(IN_SIZES)

ROW_TILE = 512
FF_CHUNK = 256
CHUNK = 128
Q_TILE = 256
KV_TILE = 256
V_ROWS = AT_HD + 16
LOG2E = 1.4426950408889634
Q_SCALE = AT_HD ** -0.5 * LOG2E
FLASH_UNROLL = 8
VMEM_LIMIT = 56 * 1024 * 1024


def _dot(a, b):
    return jnp.dot(a, b, preferred_element_type=F32)


def _dot_nt(a, b):
    return lax.dot_general(a, b, (((1,), (1,)), ((), ())), preferred_element_type=F32)


def _dot_tn(a, b):
    return lax.dot_general(a, b, (((0,), (0,)), ((), ())), preferred_element_type=F32)


def _rms(x, w):
    ms = jnp.mean(x * x, axis=-1, keepdims=True)
    return x * lax.rsqrt(ms + EPS) * w


def _sigmoid(x):
    return 0.5 * jnp.tanh(0.5 * x) + 0.5


def _silu(x):
    h = 0.5 * x
    return h * jnp.tanh(h) + h


def _swiglu_half_step(h, norm_w, wg_ref, wu_ref, wd_ref):
    hn = _rms(h, norm_w).astype(BF16)
    acc = None
    pending = None
    for c in range(D_FF // FF_CHUNK + 1):
        if c < D_FF // FF_CHUNK:
            sl = slice(c * FF_CHUNK, (c + 1) * FF_CHUNK)
            g = _dot(hn, wg_ref[:, sl])
            u = _dot(hn, wu_ref[:, sl])
        if pending is not None:
            d = _dot(pending[0], wd_ref[pending[1], :])
            acc = d if acc is None else acc + d
        if c < D_FF // FF_CHUNK:
            pending = ((_silu(g) * u).astype(BF16), sl)
    return h + 0.5 * acc


CAST_STEPS = 16


def _cast_kernel(*refs):
    n = len(refs) // 2
    for src, dst in zip(refs[:n], refs[n:]):
        dst[...] = src[...].astype(BF16)


def _cast_weights(ws):
    specs = [pl.BlockSpec((w.shape[0] // CAST_STEPS, w.shape[1]), lambda i: (i, 0)) for w in ws]
    return pl.pallas_call(
        _cast_kernel,
        out_shape=[jax.ShapeDtypeStruct(w.shape, BF16) for w in ws],
        grid=(CAST_STEPS,),
        in_specs=specs,
        out_specs=specs,
        compiler_params=pltpu.CompilerParams(
            dimension_semantics=("arbitrary",), vmem_limit_bytes=VMEM_LIMIT),
        name="cast_weights",
    )(*ws)


def _ffn_kernel(x_ref, tail_ref, nw_ref, wg_ref, wu_ref, wd_ref, o_ref, *, n_x_tiles):
    h = jnp.where(pl.program_id(0) < n_x_tiles, x_ref[...], tail_ref[...])
    o_ref[...] = _swiglu_half_step(h, nw_ref[...], wg_ref, wu_ref, wd_ref)


def _const_spec(shape):
    nd = len(shape)
    return pl.BlockSpec(shape, lambda *_: (0,) * nd, pipeline_mode=pl.Buffered(1))


def _ffn_call(x, tail, nw, wg, wu, wd):
    n_x_tiles = x.shape[0] // ROW_TILE
    rows = x.shape[0] + tail.shape[0]
    return pl.pallas_call(
        functools.partial(_ffn_kernel, n_x_tiles=n_x_tiles),
        out_shape=jax.ShapeDtypeStruct((rows, D_MODEL), F32),
        grid=(n_x_tiles + 1,),
        in_specs=[
            pl.BlockSpec((ROW_TILE, D_MODEL), lambda i: (jnp.minimum(i, n_x_tiles - 1), 0)),
            _const_spec((ROW_TILE, D_MODEL)),
            _const_spec((1, D_MODEL)),
            _const_spec((D_MODEL, D_FF)),
            _const_spec((D_MODEL, D_FF)),
            _const_spec((D_FF, D_MODEL)),
        ],
        out_specs=pl.BlockSpec((ROW_TILE, D_MODEL), lambda i: (i, 0)),
        compiler_params=pltpu.CompilerParams(
            dimension_semantics=("arbitrary",), vmem_limit_bytes=VMEM_LIMIT),
        name="ffn1",
    )(x, tail, nw, wg, wu, wd)


def _lower_bound_gate(z, lbp, valid):
    m = jnp.max(lbp, axis=0, keepdims=True)
    e = jnp.exp(lbp - m)
    lb = e[0:1, :] / jnp.sum(e, axis=0, keepdims=True)
    kk = (1.0 - lb) * _sigmoid(-z)
    kk = jnp.where(valid, kk, 0.0)
    return kk, jnp.log2(1.0 - kk)


def _head_rms_rope(z, ms, w, cos, sin):
    zn = z * lax.rsqrt(ms + EPS) * w
    width = z.shape[1]
    lane = lax.broadcasted_iota(jnp.int32, z.shape, 1)
    partner = jnp.where((lane & 1) == 0, pltpu.roll(zn, width - 1, 1), pltpu.roll(zn, 1, 1))
    return zn * cos + partner * sin


def _mix_proj_kernel(h_ref, nw_ref, win_ref, lbf_ref, lbb_ref, qnw_ref, knw_ref, cos_ref, sin_ref,
                     bdq_ref, bdk_ref,
                     hq_ref, hv_ref, kf_ref, lff_ref, kb_ref, lfb_ref, gs_ref, qT_ref, k_ref, vT_ref,
                     ga_ref, gb_ref, *, n_valid):
    i = pl.program_id(0)
    un = _rms(h_ref[...], nw_ref[...]).astype(BF16)

    def proj(piece):
        off, size = IN_OFFS[piece], IN_SIZES[piece]
        return _dot(un, win_ref[:, off:off + size])

    row = i * ROW_TILE + lax.broadcasted_iota(jnp.int32, (ROW_TILE, 1), 0)
    valid = row < n_valid

    zq = proj(5)
    zk = proj(6)
    cos = cos_ref[...]
    sin = sin_ref[...]
    cos_q = jnp.concatenate([cos] * (AT_W // 128), axis=1)
    sin_q = jnp.concatenate([sin] * (AT_W // 128), axis=1)
    ms_q = _dot((zq * zq).astype(BF16), bdq_ref[...])
    ms_k = _dot((zk * zk).astype(BF16), bdk_ref[...])
    q = _head_rms_rope(zq, ms_q, qnw_ref[...], cos_q, sin_q) * Q_SCALE
    qT_ref[...] = jnp.transpose(q).astype(BF16)
    k_ref[...] = _head_rms_rope(zk, ms_k, knw_ref[...], cos, sin).astype(BF16)

    hq_ref[...] = _silu(proj(0))
    hv_ref[...] = proj(1).astype(BF16)
    kf, lff = _lower_bound_gate(proj(2), lbf_ref[...], valid)
    kf_ref[...] = kf
    lff_ref[...] = lff
    kb, lfb = _lower_bound_gate(proj(3), lbb_ref[...], valid)
    kb_ref[...] = kb
    lfb_ref[...] = lfb
    gs_ref[...] = _silu(proj(4))

    vT = jnp.transpose(proj(7)).astype(BF16)
    ones = jnp.ones((V_ROWS - AT_HD, ROW_TILE), BF16)
    for g in range(AT_KV_HEADS):
        vT_ref[g, 0:AT_HD, :] = vT[g * AT_HD:(g + 1) * AT_HD, :]
        vT_ref[g, AT_HD:V_ROWS, :] = ones

    ga_ref[...] = _sigmoid(proj(8))
    gb_ref[...] = _sigmoid(proj(9))


def _mix_proj_call(h1, nw, win, lbf, lbb, qnw, knw, cos, sin, bdq, bdk, n_valid):
    rows = h1.shape[0]
    row_spec = lambda w: pl.BlockSpec((ROW_TILE, w), lambda i: (i, 0))
    out_shape = [
        jax.ShapeDtypeStruct((rows, HG_W), F32),
        jax.ShapeDtypeStruct((rows, HG_W), BF16),
        jax.ShapeDtypeStruct((rows, HG_W), F32),
        jax.ShapeDtypeStruct((rows, HG_W), F32),
        jax.ShapeDtypeStruct((rows, HG_W), F32),
        jax.ShapeDtypeStruct((rows, HG_W), F32),
        jax.ShapeDtypeStruct((rows, HG_W), F32),
        jax.ShapeDtypeStruct((AT_W, rows), BF16),
        jax.ShapeDtypeStruct((rows, AT_KVW), BF16),
        jax.ShapeDtypeStruct((AT_KV_HEADS, V_ROWS, rows), BF16),
        jax.ShapeDtypeStruct((rows, D_MODEL), F32),
        jax.ShapeDtypeStruct((rows, D_MODEL), F32),
    ]
    out_specs = [
        row_spec(HG_W), row_spec(HG_W), row_spec(HG_W), row_spec(HG_W), row_spec(HG_W), row_spec(HG_W),
        row_spec(HG_W),
        pl.BlockSpec((AT_W, ROW_TILE), lambda i: (0, i)),
        row_spec(AT_KVW),
        pl.BlockSpec((AT_KV_HEADS, V_ROWS, ROW_TILE), lambda i: (0, 0, i)),
        row_spec(D_MODEL), row_spec(D_MODEL),
    ]
    in_specs = [
        row_spec(D_MODEL),
        _const_spec((1, D_MODEL)),
        _const_spec((D_MODEL, D_IN)),
        _const_spec(lbf.shape),
        _const_spec(lbb.shape),
        _const_spec((1, AT_W)),
        _const_spec((1, AT_KVW)),
        row_spec(128),
        row_spec(128),
        _const_spec((AT_W, AT_W)),
        _const_spec((AT_KVW, AT_KVW)),
    ]
    return pl.pallas_call(
        functools.partial(_mix_proj_kernel, n_valid=n_valid),
        out_shape=out_shape,
        grid=(rows // ROW_TILE,),
        in_specs=in_specs,
        out_specs=out_specs,
        compiler_params=pltpu.CompilerParams(
            dimension_semantics=("arbitrary",), vmem_limit_bytes=VMEM_LIMIT),
        name="mix_proj",
    )(h1, nw, win, lbf, lbb, qnw, knw, cos, sin, bdq, bdk)


def _split2(x):
    hi = x.astype(BF16)
    lo = (x - hi.astype(F32)).astype(BF16)
    return hi, lo


def _level_operand(q, k, b, lf, blk, reverse):
    c = CHUNK
    if blk >= 8:
        pieces = []
        for g in range(c // (2 * blk)):
            r0 = g * 2 * blk
            lo, hi = slice(r0, r0 + blk), slice(r0 + blk, r0 + 2 * blk)
            if reverse:
                ref = b[r0 + blk:r0 + blk + 1, :]
                pieces.append(q[lo] * jnp.exp2(b[lo] - ref))
                pieces.append(k[hi] * jnp.exp2(ref - b[hi]))
            else:
                ref = b[r0 + blk - 1:r0 + blk, :]
                pieces.append(k[lo] * jnp.exp2(ref - b[lo]))
                pieces.append(q[hi] * jnp.exp2(b[hi] - ref))
        return jnp.concatenate(pieces, axis=0).astype(BF16)

    q3, k3, b3, lf3 = (x.reshape(c // 8, 8, HG_W) for x in (q, k, b, lf))
    sub = lax.broadcasted_iota(jnp.int32, (1, 8, 1), 1)
    later = ((sub >> (blk.bit_length() - 1)) & 1) == 1
    q_role = jnp.logical_not(later) if reverse else later
    if blk == 4:
        ref = b3[:, 4:5, :] if reverse else b3[:, 3:4, :]
        gl = -jnp.abs(b3 - ref)
    elif blk == 2:
        up = pltpu.roll(lf3, 7, 1)
        dn = pltpu.roll(lf3, 1, 1)
        m4 = sub & 3
        if reverse:
            gl = jnp.where(m4 == 0, lf3 + up, jnp.where(m4 == 1, lf3, jnp.where(m4 == 2, 0.0, dn)))
        else:
            gl = jnp.where(m4 == 0, up, jnp.where(m4 == 1, 0.0, jnp.where(m4 == 2, lf3, dn + lf3)))
    else:
        odd = (sub & 1) == 1
        gl = jnp.where(odd, 0.0, lf3) if reverse else jnp.where(odd, lf3, 0.0)
    y3 = jnp.where(q_role, q3, k3) * jnp.exp2(gl)
    return y3.reshape(c, HG_W).astype(BF16)


def _hgrn_direction(q_ref, v_ref, k_ref, lf_ref, o_ref, st_ref, reverse):
    c = CHUNK
    q = q_ref[...]
    k = k_ref[...]
    v = v_ref[...]
    lf = lf_ref[...]

    r_i = lax.broadcasted_iota(jnp.int32, (c, c), 0)
    c_i = lax.broadcasted_iota(jnp.int32, (c, c), 1)
    tri = ((c_i >= r_i) if reverse else (c_i <= r_i)).astype(BF16)
    hi, lo = _split2(lf)
    b = _dot(jnp.concatenate([tri, tri], axis=1), jnp.concatenate([hi, lo], axis=0))

    levels = (64, 32, 16, 8, 4, 2, 1)
    ys = [_level_operand(q, k, b, lf, blk, reverse) for blk in levels]

    lane = lax.broadcasted_iota(jnp.int32, (1, c), 1)
    small_valid = {}
    for blk in (4, 2, 1):
        sh = blk.bit_length() - 1
        same = (r_i >> (sh + 1)) == (c_i >> (sh + 1))
        t_later = ((r_i >> sh) & 1) == 1
        s_later = ((c_i >> sh) & 1) == 1
        small_valid[blk] = (same & jnp.logical_not(t_later) & s_later) if reverse else (
            same & t_later & jnp.logical_not(s_later))

    def tile_mask(blk, tile):
        if blk < 8:
            return small_valid[blk][8 * tile:8 * tile + 8, :]
        block = (8 * tile) // blk
        if (block % 2 == 1) == reverse:
            return None
        col0 = (block + 1) * blk if reverse else (block - 1) * blk
        return (lane >= col0) & (lane < col0 + blk)

    b_last = b[0:1, :] if reverse else b[c - 1:c, :]
    q_inter = (q * jnp.exp2(b)).astype(BF16)
    k_state = (k * jnp.exp2(b_last - b)).astype(BF16)
    e_last = jnp.exp2(b_last)
    qk = q * k
    eye = r_i == c_i

    def issue(h):
        hs = slice(h * HG_K, (h + 1) * HG_K)
        products = [_dot_nt(y[:, hs], y[:, hs]) for y in ys]
        st = st_ref[h]
        inter = _dot_nt(q_inter[:, hs], st.astype(BF16))
        st_ref[h] = st * e_last[:, hs] + _dot_tn(v[:, hs], k_state[:, hs])
        return products, inter

    def finish(h, issued):
        products, inter = issued
        hs = slice(h * HG_K, (h + 1) * HG_K)
        diag = jnp.where(eye, jnp.sum(qk[:, hs], axis=1, keepdims=True), 0.0)
        tiles = []
        for t in range(c // 8):
            rows = slice(8 * t, 8 * t + 8)
            a_t = diag[rows]
            for blk, p in zip(levels, products):
                mask = tile_mask(blk, t)
                if mask is not None:
                    a_t = jnp.where(mask, p[rows], a_t)
            tiles.append(a_t)
        a = jnp.concatenate(tiles, axis=0)
        o_ref[:, hs] = _dot(a.astype(BF16), v[:, hs]) + inter

    return issue, finish


def _hgrn_kernel(qf_ref, vf_ref, kf_ref, lff_ref, qb_ref, vb_ref, kb_ref, lfb_ref,
                 of_ref, ob_ref, sf_ref, sb_ref):
    @pl.when(pl.program_id(0) == 0)
    def _():
        sf_ref[...] = jnp.zeros_like(sf_ref)
        sb_ref[...] = jnp.zeros_like(sb_ref)

    units = []
    for args in ((qf_ref, vf_ref, kf_ref, lff_ref, of_ref, sf_ref, False),
                 (qb_ref, vb_ref, kb_ref, lfb_ref, ob_ref, sb_ref, True)):
        issue, finish = _hgrn_direction(*args)
        units += [(issue, finish, h) for h in range(HG_HEADS)]

    pending = None
    for issue, finish, h in units:
        issued = issue(h)
        if pending is not None:
            pending[0](pending[1], pending[2])
        pending = (finish, h, issued)
    pending[0](pending[1], pending[2])


def _hgrn_call(hq, hv, kf, lff, kb, lfb, n_real):
    nb = n_real // CHUNK
    fwd = lambda s: (jnp.where(s == 0, nb, s - 1), 0)
    bwd = lambda s: (jnp.where(s == nb, nb, nb - 1 - s), 0)
    spec = lambda m: pl.BlockSpec((CHUNK, HG_W), m)
    out = jax.ShapeDtypeStruct(((nb + 1) * CHUNK, HG_W), F32)
    return pl.pallas_call(
        _hgrn_kernel,
        out_shape=[out, out],
        grid=(nb + 1,),
        in_specs=[spec(fwd), spec(fwd), spec(fwd), spec(fwd), spec(bwd), spec(bwd), spec(bwd), spec(bwd)],
        out_specs=[spec(fwd), spec(bwd)],
        scratch_shapes=[pltpu.VMEM((HG_HEADS, HG_K, HG_K), F32), pltpu.VMEM((HG_HEADS, HG_K, HG_K), F32)],
        compiler_params=pltpu.CompilerParams(
            dimension_semantics=("arbitrary",), vmem_limit_bytes=VMEM_LIMIT),
        name="hgrn",
    )(hq, hv, kf, lff, hq, hv, kb, lfb)


def _flash_kernel(qT_ref, k_ref, vT_ref, kt_ref, vTt_ref, o_ref, qp_ref, m_ref, acc_ref, s_ref, mc_ref,
                  st_ref, mct_ref, *,
                  n_kv, n_tail_valid):
    g = pl.program_id(0)
    tq = qT_ref.shape[1]

    half = lax.broadcasted_iota(jnp.int32, (AT_KVW, tq), 0) >> 6
    for h in range(AT_GROUP):
        qh = qT_ref[h * AT_HD:(h + 1) * AT_HD, :].astype(F32)
        q2 = jnp.concatenate([qh, qh], axis=0)
        qp_ref[h] = jnp.where(half == g, q2, 0.0).astype(BF16)

    m_ref[...] = jnp.full(m_ref.shape, -jnp.inf, F32)
    acc_ref[...] = jnp.zeros(acc_ref.shape, F32)

    def store_scores(sT, s_dst, mc_dst, h):
        s_dst[h] = sT
        mc_dst[h] = jnp.max(sT, axis=0, keepdims=True)

    def scores(chunk, slot):
        off = pl.multiple_of(chunk * KV_TILE, KV_TILE)
        kc = k_ref[pl.ds(off, KV_TILE), :]
        for h in range(AT_GROUP):
            store_scores(_dot(kc, qp_ref[h]), s_ref.at[slot], mc_ref.at[slot], h)

    def consume(vc, s_src, mc_src):
        for h in range(AT_GROUP):
            m_prev = m_ref[h]
            m_new = jnp.maximum(m_prev, mc_src[h])
            alpha = jnp.exp2(m_prev - m_new)
            pT = jnp.exp2(s_src[h] - m_new).astype(BF16)
            acc_ref[h] = alpha * acc_ref[h] + _dot(vc, pT)
            m_ref[h] = m_new

    def consume_chunk(chunk, slot):
        off = pl.multiple_of(chunk * KV_TILE, KV_TILE)
        consume(vT_ref[:, pl.ds(off, KV_TILE)], s_ref.at[slot], mc_ref.at[slot])

    krow = lax.broadcasted_iota(jnp.int32, (128, 1), 0)
    for h in range(AT_GROUP):
        sT = jnp.where(krow < n_tail_valid, _dot(kt_ref[...], qp_ref[h]), -jnp.inf)
        store_scores(sT, st_ref, mct_ref, h)
    scores(0, 0)
    consume(vTt_ref[...], st_ref, mct_ref)

    def body(i, carry):
        scores(2 * i + 1, 1)
        consume_chunk(2 * i, 0)
        scores(jnp.minimum(2 * i + 2, n_kv - 1), 0)
        consume_chunk(2 * i + 1, 1)
        return carry

    lax.fori_loop(0, n_kv // 2, body, 0, unroll=FLASH_UNROLL)

    outs = []
    for h in range(AT_GROUP):
        acc = acc_ref[h]
        outs.append(acc[0:AT_HD, :] / acc[AT_HD:AT_HD + 1, :])
    o_ref[...] = jnp.transpose(jnp.concatenate(outs, axis=0)).astype(BF16)


def _flash_call(qT, k, vT, n_real):
    n_kv = n_real // KV_TILE
    tail_blk = n_real // 128
    gw = AT_GROUP * AT_HD
    return pl.pallas_call(
        functools.partial(_flash_kernel, n_kv=n_kv, n_tail_valid=N_META),
        out_shape=jax.ShapeDtypeStruct((n_real, AT_W), BF16),
        grid=(AT_KV_HEADS, n_real // Q_TILE),
        in_specs=[
            pl.BlockSpec((gw, Q_TILE), lambda g, i: (g, i)),
            pl.BlockSpec((n_real, AT_KVW), lambda g, i: (0, 0)),
            pl.BlockSpec((None, V_ROWS, n_real), lambda g, i: (g, 0, 0)),
            pl.BlockSpec((128, AT_KVW), lambda g, i: (tail_blk, 0)),
            pl.BlockSpec((None, V_ROWS, 128), lambda g, i: (g, 0, tail_blk)),
        ],
        out_specs=pl.BlockSpec((Q_TILE, gw), lambda g, i: (i, g)),
        scratch_shapes=[
            pltpu.VMEM((AT_GROUP, AT_KVW, Q_TILE), BF16),
            pltpu.VMEM((AT_GROUP, 1, Q_TILE), F32),
            pltpu.VMEM((AT_GROUP, V_ROWS, Q_TILE), F32),
            pltpu.VMEM((2, AT_GROUP, KV_TILE, Q_TILE), F32),
            pltpu.VMEM((2, AT_GROUP, 1, Q_TILE), F32),
            pltpu.VMEM((AT_GROUP, 128, Q_TILE), F32),
            pltpu.VMEM((AT_GROUP, 1, Q_TILE), F32),
        ],
        compiler_params=pltpu.CompilerParams(
            dimension_semantics=("arbitrary", "arbitrary"), vmem_limit_bytes=VMEM_LIMIT),
        name="flash",
    )(qT, k, vT, k, vT)


def _merge_ffn_kernel(h1_ref, of_ref, ob_ref, gs_ref, yb_ref, ga_ref, gb_ref, hgw_ref,
                      wua_ref, wub_ref, wout_ref, nw_ref, wg_ref, wu_ref, wd_ref, o_ref):
    o = of_ref[...] + ob_ref[...]
    normed = []
    for h in range(HG_HEADS):
        oh = o[:, h * HG_K:(h + 1) * HG_K]
        normed.append(oh * lax.rsqrt(jnp.mean(oh * oh, axis=-1, keepdims=True) + EPS))
    ya = (jnp.concatenate(normed, axis=1) * hgw_ref[...] * gs_ref[...]).astype(BF16)
    mixed = ga_ref[...] * _dot(ya, wua_ref[...]) + gb_ref[...] * _dot(yb_ref[...], wub_ref[...])
    h2 = h1_ref[...] + _dot(mixed.astype(BF16), wout_ref[...])
    o_ref[...] = _swiglu_half_step(h2, nw_ref[...], wg_ref, wu_ref, wd_ref)


def _merge_ffn_call(h1, o_f, o_b, gs, yb, ga, gb, hgw, wua, wub, wout, nw, wg, wu, wd, n_real):
    row_spec = lambda w: pl.BlockSpec((ROW_TILE, w), lambda i: (i, 0))
    return pl.pallas_call(
        _merge_ffn_kernel,
        out_shape=jax.ShapeDtypeStruct((n_real, D_MODEL), F32),
        grid=(n_real // ROW_TILE,),
        in_specs=[
            row_spec(D_MODEL), row_spec(HG_W), row_spec(HG_W), row_spec(HG_W), row_spec(AT_W),
            row_spec(D_MODEL), row_spec(D_MODEL),
            _const_spec((1, HG_W)),
            _const_spec((HG_W, D_MODEL)),
            _const_spec((AT_W, D_MODEL)),
            _const_spec((D_MODEL, D_MODEL)),
            _const_spec((1, D_MODEL)),
            _const_spec((D_MODEL, D_FF)),
            _const_spec((D_MODEL, D_FF)),
            _const_spec((D_FF, D_MODEL)),
        ],
        out_specs=row_spec(D_MODEL),
        compiler_params=pltpu.CompilerParams(
            dimension_semantics=("arbitrary",), vmem_limit_bytes=VMEM_LIMIT),
        name="merge_ffn2",
    )(h1, o_f, o_b, gs, yb, ga, gb, hgw, wua, wub, wout, nw, wg, wu, wd)


def _rope_tables(n_real, rows):
    half = AT_HD // 2
    grid_rows = n_real // GRID_W
    r = jnp.repeat(jnp.arange(grid_rows, dtype=F32), GRID_W)
    c = jnp.tile(jnp.arange(GRID_W, dtype=F32), grid_rows)
    zeros = jnp.zeros((rows - n_real,), F32)
    r = jnp.concatenate([r, zeros])
    c = jnp.concatenate([c, zeros])
    inv = ROPE_THETA ** (-jnp.arange(0, half, 2, dtype=F32) / half)
    ang = jnp.concatenate([r[:, None] * inv, c[:, None] * inv], axis=-1)
    cos = jnp.repeat(jnp.cos(ang), 2, axis=-1)
    sin = jnp.repeat(jnp.sin(ang), 2, axis=-1) * jnp.tile(jnp.array([-1.0, 1.0], F32), half)
    reps = 128 // AT_HD
    return jnp.tile(cos, (1, reps)), jnp.tile(sin, (1, reps))


def _head_mean_matrix(width):
    heads = width // AT_HD
    return jnp.kron(jnp.eye(heads, dtype=F32), jnp.full((AT_HD, AT_HD), 1.0 / AT_HD, F32)).astype(BF16)


def kernel(x, meta_tokens, ffn1_norm, ffn1_w_gate, ffn1_w_up, ffn1_w_down, mix_norm, w_in, hg_lb_fwd, hg_lb_bwd, hg_out_norm, q_norm, k_norm, w_up_a, w_up_b, w_out, ffn2_norm, ffn2_w_gate, ffn2_w_up, ffn2_w_down):
    batch, n_real, _ = x.shape
    assert batch == 1 and n_real % ROW_TILE == 0 and n_real % GRID_W == 0
    rows = n_real + ROW_TILE
    n_valid = n_real + N_META

    tail = jnp.concatenate(
        [meta_tokens.astype(x.dtype), jnp.zeros((ROW_TILE - N_META, D_MODEL), x.dtype)], axis=0)
    row = lambda w: w.reshape(1, -1).astype(F32)
    (w1g, w1u, w1d, win, wua, wub, wout, w2g, w2u, w2d) = _cast_weights(
        [w[0].astype(F32) for w in (ffn1_w_gate, ffn1_w_up, ffn1_w_down, w_in, w_up_a, w_up_b, w_out,
                                    ffn2_w_gate, ffn2_w_up, ffn2_w_down)])

    h1 = _ffn_call(x[0], tail, row(ffn1_norm[0]), w1g, w1u, w1d)

    cos, sin = _rope_tables(n_real, rows)
    hq, hv, kf, lff, kb, lfb, gs, qT, k, vT, ga, gb = _mix_proj_call(
        h1, row(mix_norm[0]), win, hg_lb_fwd.astype(F32), hg_lb_bwd.astype(F32),
        row(jnp.tile(q_norm[0], AT_HEADS)), row(jnp.tile(k_norm[0], AT_KV_HEADS)), cos, sin,
        _head_mean_matrix(AT_W), _head_mean_matrix(AT_KVW), n_valid)

    o_f, o_b = _hgrn_call(hq, hv, kf, lff, kb, lfb, n_real)
    yb = _flash_call(qT, k, vT, n_real)

    out = _merge_ffn_call(
        h1, o_f, o_b, gs, yb, ga, gb, row(hg_out_norm[0]), wua, wub, wout,
        row(ffn2_norm[0]), w2g, w2u, w2d, n_real)
    return out.reshape(batch, n_real, D_MODEL)
```

```python
import functools

import jax
import jax.numpy as jnp
from jax import lax
from jax.experimental import pallas as pl
from jax.experimental.pallas import tpu as pltpu

F32 = jnp.float32
BF16 = jnp.bfloat16

D_MODEL = 1024
D_FF = 2816
N_META = 16
GRID_W = 64
EPS = 1e-6
HG_HEADS = 4
HG_K = 128
HG_W = HG_HEADS * HG_K
AT_HEADS = 8
AT_KV_HEADS = 2
AT_HD = 64
AT_GROUP = AT_HEADS // AT_KV_HEADS
AT_W = AT_HEADS * AT_HD
AT_KVW = AT_KV_HEADS * AT_HD
ROPE_THETA = 10000.0
IN_SIZES = (HG_W, HG_W, HG_W, HG_W, HG_W, AT_W, AT_KVW, AT_KVW, D_MODEL, D_MODEL)
IN_OFFS = tuple(sum(IN_SIZES[:i]) for i in range(len(IN_SIZES)))
D_IN = sum(IN_SIZES)

ROW_TILE = 512
FF_CHUNK = 256
CHUNK = 128
Q_TILE = 256
KV_TILE = 256
V_ROWS = AT_HD + 16
LOG2E = 1.4426950408889634
Q_SCALE = AT_HD ** -0.5 * LOG2E
FLASH_UNROLL = 8
VMEM_LIMIT = 56 * 1024 * 1024


def _dot(a, b):
    return jnp.dot(a, b, preferred_element_type=F32)


def _dot_nt(a, b):
    return lax.dot_general(a, b, (((1,), (1,)), ((), ())), preferred_element_type=F32)


def _dot_tn(a, b):
    return lax.dot_general(a, b, (((0,), (0,)), ((), ())), preferred_element_type=F32)


def _rms(x, w):
    ms = jnp.mean(x * x, axis=-1, keepdims=True)
    return x * lax.rsqrt(ms + EPS) * w


def _sigmoid(x):
    return 0.5 * jnp.tanh(0.5 * x) + 0.5


def _silu(x):
    h = 0.5 * x
    return h * jnp.tanh(h) + h


def _swiglu_half_step(h, norm_w, wg_ref, wu_ref, wd_ref):
    hn = _rms(h, norm_w).astype(BF16)
    acc = None
    pending = None
    for c in range(D_FF // FF_CHUNK + 1):
        if c < D_FF // FF_CHUNK:
            sl = slice(c * FF_CHUNK, (c + 1) * FF_CHUNK)
            g = _dot(hn, wg_ref[:, sl])
            u = _dot(hn, wu_ref[:, sl])
        if pending is not None:
            d = _dot(pending[0], wd_ref[pending[1], :])
            acc = d if acc is None else acc + d
        if c < D_FF // FF_CHUNK:
            pending = ((_silu(g) * u).astype(BF16), sl)
    return h + 0.5 * acc


CAST_STEPS = 16


def _cast_kernel(*refs):
    n = len(refs) // 2
    for src, dst in zip(refs[:n], refs[n:]):
        dst[...] = src[...].astype(BF16)


def _cast_weights(ws):
    specs = [pl.BlockSpec((w.shape[0] // CAST_STEPS, w.shape[1]), lambda i: (i, 0)) for w in ws]
    return pl.pallas_call(
        _cast_kernel,
        out_shape=[jax.ShapeDtypeStruct(w.shape, BF16) for w in ws],
        grid=(CAST_STEPS,),
        in_specs=specs,
        out_specs=specs,
        compiler_params=pltpu.CompilerParams(
            dimension_semantics=("arbitrary",), vmem_limit_bytes=VMEM_LIMIT),
        name="cast_weights",
    )(*ws)


def _ffn_kernel(x_ref, tail_ref, nw_ref, wg_ref, wu_ref, wd_ref, o_ref, *, n_x_tiles):
    h = jnp.where(pl.program_id(0) < n_x_tiles, x_ref[...], tail_ref[...])
    o_ref[...] = _swiglu_half_step(h, nw_ref[...], wg_ref, wu_ref, wd_ref)


def _const_spec(shape):
    nd = len(shape)
    return pl.BlockSpec(shape, lambda *_: (0,) * nd, pipeline_mode=pl.Buffered(1))


def _ffn_call(x, tail, nw, wg, wu, wd):
    n_x_tiles = x.shape[0] // ROW_TILE
    rows = x.shape[0] + tail.shape[0]
    return pl.pallas_call(
        functools.partial(_ffn_kernel, n_x_tiles=n_x_tiles),
        out_shape=jax.ShapeDtypeStruct((rows, D_MODEL), F32),
        grid=(n_x_tiles + 1,),
        in_specs=[
            pl.BlockSpec((ROW_TILE, D_MODEL), lambda i: (jnp.minimum(i, n_x_tiles - 1), 0)),
            _const_spec((ROW_TILE, D_MODEL)),
            _const_spec((1, D_MODEL)),
            _const_spec((D_MODEL, D_FF)),
            _const_spec((D_MODEL, D_FF)),
            _const_spec((D_FF, D_MODEL)),
        ],
        out_specs=pl.BlockSpec((ROW_TILE, D_MODEL), lambda i: (i, 0)),
        compiler_params=pltpu.CompilerParams(
            dimension_semantics=("arbitrary",), vmem_limit_bytes=VMEM_LIMIT),
        name="ffn1",
    )(x, tail, nw, wg, wu, wd)


def _lower_bound_gate(z, lbp, valid):
    m = jnp.max(lbp, axis=0, keepdims=True)
    e = jnp.exp(lbp - m)
    lb = e[0:1, :] / jnp.sum(e, axis=0, keepdims=True)
    kk = (1.0 - lb) * _sigmoid(-z)
    kk = jnp.where(valid, kk, 0.0)
    return kk, jnp.log2(1.0 - kk)


def _head_rms_rope(z, ms, w, cos, sin):
    zn = z * lax.rsqrt(ms + EPS) * w
    width = z.shape[1]
    lane = lax.broadcasted_iota(jnp.int32, z.shape, 1)
    partner = jnp.where((lane & 1) == 0, pltpu.roll(zn, width - 1, 1), pltpu.roll(zn, 1, 1))
    return zn * cos + partner * sin


def _rope_tile(by_row_ref, by_col_ref, is_tail, tail_value):
    groups = ROW_TILE // GRID_W
    rp = by_row_ref[...]
    by_row = jnp.concatenate([jnp.broadcast_to(rp[j:j + 1, :], (GRID_W, 128)) for j in range(groups)], axis=0)
    by_col = jnp.concatenate([by_col_ref[...]] * groups, axis=0)
    lane = lax.broadcasted_iota(jnp.int32, (1, 128), 1)
    t = jnp.where((lane & (AT_HD - 1)) < AT_HD // 2, by_row, by_col)
    return jnp.where(is_tail, tail_value, t)


def _mix_proj_kernel(h_ref, nw_ref, win_ref, lbf_ref, lbb_ref, qnw_ref, knw_ref, rcos_ref, rsin_ref,
                     ccos_ref, csin_ref, bdq_ref, bdk_ref,
                     hq_ref, hv_ref, kf_ref, lff_ref, kb_ref, lfb_ref, gs_ref, qT_ref, k_ref, vT_ref,
                     ga_ref, gb_ref, *, n_valid, n_x_tiles):
    i = pl.program_id(0)
    un = _rms(h_ref[...], nw_ref[...]).astype(BF16)

    def proj(piece):
        off, size = IN_OFFS[piece], IN_SIZES[piece]
        return _dot(un, win_ref[:, off:off + size])

    row = i * ROW_TILE + lax.broadcasted_iota(jnp.int32, (ROW_TILE, 1), 0)
    valid = row < n_valid

    zq = proj(5)
    zk = proj(6)
    cos = _rope_tile(rcos_ref, ccos_ref, i >= n_x_tiles, 1.0)
    sin = _rope_tile(rsin_ref, csin_ref, i >= n_x_tiles, 0.0)
    cos_q = jnp.concatenate([cos] * (AT_W // 128), axis=1)
    sin_q = jnp.concatenate([sin] * (AT_W // 128), axis=1)
    ms_q = _dot((zq * zq).astype(BF16), bdq_ref[...])
    ms_k = _dot((zk * zk).astype(BF16), bdk_ref[...])
    q = _head_rms_rope(zq, ms_q, qnw_ref[...], cos_q, sin_q) * Q_SCALE
    qT_ref[...] = jnp.transpose(q).astype(BF16)
    k_ref[...] = _head_rms_rope(zk, ms_k, knw_ref[...], cos, sin).astype(BF16)

    hq_ref[...] = _silu(proj(0))
    hv_ref[...] = proj(1).astype(BF16)
    kf, lff = _lower_bound_gate(proj(2), lbf_ref[...], valid)
    kf_ref[...] = kf
    lff_ref[...] = lff
    kb, lfb = _lower_bound_gate(proj(3), lbb_ref[...], valid)
    kb_ref[...] = kb
    lfb_ref[...] = lfb
    gs_ref[...] = _silu(proj(4))

    vT = jnp.transpose(proj(7)).astype(BF16)
    ones = jnp.ones((V_ROWS - AT_HD, ROW_TILE), BF16)
    for g in range(AT_KV_HEADS):
        vT_ref[g, 0:AT_HD, :] = vT[g * AT_HD:(g + 1) * AT_HD, :]
        vT_ref[g, AT_HD:V_ROWS, :] = ones

    ga_ref[...] = _sigmoid(proj(8))
    gb_ref[...] = _sigmoid(proj(9))


def _mix_proj_call(h1, nw, win, lbf, lbb, qnw, knw, rope, bdq, bdk, n_valid):
    rows = h1.shape[0]
    n_x_tiles = rows // ROW_TILE - 1
    groups = ROW_TILE // GRID_W
    rope_row_spec = pl.BlockSpec((groups, 128), lambda i: (jnp.minimum(i, n_x_tiles - 1), 0))
    row_spec = lambda w: pl.BlockSpec((ROW_TILE, w), lambda i: (i, 0))
    out_shape = [
        jax.ShapeDtypeStruct((rows, HG_W), F32),
        jax.ShapeDtypeStruct((rows, HG_W), BF16),
        jax.ShapeDtypeStruct((rows, HG_W), F32),
        jax.ShapeDtypeStruct((rows, HG_W), F32),
        jax.ShapeDtypeStruct((rows, HG_W), F32),
        jax.ShapeDtypeStruct((rows, HG_W), F32),
        jax.ShapeDtypeStruct((rows, HG_W), F32),
        jax.ShapeDtypeStruct((AT_W, rows), BF16),
        jax.ShapeDtypeStruct((rows, AT_KVW), BF16),
        jax.ShapeDtypeStruct((AT_KV_HEADS, V_ROWS, rows), BF16),
        jax.ShapeDtypeStruct((rows, D_MODEL), F32),
        jax.ShapeDtypeStruct((rows, D_MODEL), F32),
    ]
    out_specs = [
        row_spec(HG_W), row_spec(HG_W), row_spec(HG_W), row_spec(HG_W), row_spec(HG_W), row_spec(HG_W),
        row_spec(HG_W),
        pl.BlockSpec((AT_W, ROW_TILE), lambda i: (0, i)),
        row_spec(AT_KVW),
        pl.BlockSpec((AT_KV_HEADS, V_ROWS, ROW_TILE), lambda i: (0, 0, i)),
        row_spec(D_MODEL), row_spec(D_MODEL),
    ]
    in_specs = [
        row_spec(D_MODEL),
        _const_spec((1, D_MODEL)),
        _const_spec((D_MODEL, D_IN)),
        _const_spec(lbf.shape),
        _const_spec(lbb.shape),
        _const_spec((1, AT_W)),
        _const_spec((1, AT_KVW)),
        rope_row_spec,
        rope_row_spec,
        _const_spec((GRID_W, 128)),
        _const_spec((GRID_W, 128)),
        _const_spec((AT_W, AT_W)),
        _const_spec((AT_KVW, AT_KVW)),
    ]
    return pl.pallas_call(
        functools.partial(_mix_proj_kernel, n_valid=n_valid, n_x_tiles=n_x_tiles),
        out_shape=out_shape,
        grid=(rows // ROW_TILE,),
        in_specs=in_specs,
        out_specs=out_specs,
        compiler_params=pltpu.CompilerParams(
            dimension_semantics=("arbitrary",), vmem_limit_bytes=VMEM_LIMIT),
        name="mix_proj",
    )(h1, nw, win, lbf, lbb, qnw, knw, *rope, bdq, bdk)


def _split2(x):
    hi = x.astype(BF16)
    lo = (x - hi.astype(F32)).astype(BF16)
    return hi, lo


def _level_operand(q, k, b, lf, blk, reverse):
    c = CHUNK
    if blk >= 8:
        pieces = []
        for g in range(c // (2 * blk)):
            r0 = g * 2 * blk
            lo, hi = slice(r0, r0 + blk), slice(r0 + blk, r0 + 2 * blk)
            if reverse:
                ref = b[r0 + blk:r0 + blk + 1, :]
                pieces.append(q[lo] * jnp.exp2(b[lo] - ref))
                pieces.append(k[hi] * jnp.exp2(ref - b[hi]))
            else:
                ref = b[r0 + blk - 1:r0 + blk, :]
                pieces.append(k[lo] * jnp.exp2(ref - b[lo]))
                pieces.append(q[hi] * jnp.exp2(b[hi] - ref))
        return jnp.concatenate(pieces, axis=0).astype(BF16)

    q3, k3, b3, lf3 = (x.reshape(c // 8, 8, HG_W) for x in (q, k, b, lf))
    sub = lax.broadcasted_iota(jnp.int32, (1, 8, 1), 1)
    later = ((sub >> (blk.bit_length() - 1)) & 1) == 1
    q_role = jnp.logical_not(later) if reverse else later
    if blk == 4:
        ref = b3[:, 4:5, :] if reverse else b3[:, 3:4, :]
        gl = -jnp.abs(b3 - ref)
    elif blk == 2:
        up = pltpu.roll(lf3, 7, 1)
        dn = pltpu.roll(lf3, 1, 1)
        m4 = sub & 3
        if reverse:
            gl = jnp.where(m4 == 0, lf3 + up, jnp.where(m4 == 1, lf3, jnp.where(m4 == 2, 0.0, dn)))
        else:
            gl = jnp.where(m4 == 0, up, jnp.where(m4 == 1, 0.0, jnp.where(m4 == 2, lf3, dn + lf3)))
    else:
        odd = (sub & 1) == 1
        gl = jnp.where(odd, 0.0, lf3) if reverse else jnp.where(odd, lf3, 0.0)
    y3 = jnp.where(q_role, q3, k3) * jnp.exp2(gl)
    return y3.reshape(c, HG_W).astype(BF16)


def _hgrn_direction(q_ref, v_ref, k_ref, lf_ref, o_ref, st_ref, reverse):
    c = CHUNK
    q = q_ref[...]
    k = k_ref[...]
    v = v_ref[...]
    lf = lf_ref[...]

    r_i = lax.broadcasted_iota(jnp.int32, (c, c), 0)
    c_i = lax.broadcasted_iota(jnp.int32, (c, c), 1)
    tri = ((c_i >= r_i) if reverse else (c_i <= r_i)).astype(BF16)
    hi, lo = _split2(lf)
    b = _dot(jnp.concatenate([tri, tri], axis=1), jnp.concatenate([hi, lo], axis=0))

    levels = (64, 32, 16, 8, 4, 2, 1)
    ys = [_level_operand(q, k, b, lf, blk, reverse) for blk in levels]

    lane = lax.broadcasted_iota(jnp.int32, (1, c), 1)
    small_valid = {}
    for blk in (4, 2, 1):
        sh = blk.bit_length() - 1
        same = (r_i >> (sh + 1)) == (c_i >> (sh + 1))
        t_later = ((r_i >> sh) & 1) == 1
        s_later = ((c_i >> sh) & 1) == 1
        small_valid[blk] = (same & jnp.logical_not(t_later) & s_later) if reverse else (
            same & t_later & jnp.logical_not(s_later))

    def tile_mask(blk, tile):
        if blk < 8:
            return small_valid[blk][8 * tile:8 * tile + 8, :]
        block = (8 * tile) // blk
        if (block % 2 == 1) == reverse:
            return None
        col0 = (block + 1) * blk if reverse else (block - 1) * blk
        return (lane >= col0) & (lane < col0 + blk)

    b_last = b[0:1, :] if reverse else b[c - 1:c, :]
    q_inter = (q * jnp.exp2(b)).astype(BF16)
    k_state = (k * jnp.exp2(b_last - b)).astype(BF16)
    e_last = jnp.exp2(b_last)
    qk = q * k
    eye = r_i == c_i

    def issue(h):
        hs = slice(h * HG_K, (h + 1) * HG_K)
        products = [_dot_nt(y[:, hs], y[:, hs]) for y in ys]
        st = st_ref[h]
        inter = _dot_nt(q_inter[:, hs], st.astype(BF16))
        st_ref[h] = st * e_last[:, hs] + _dot_tn(v[:, hs], k_state[:, hs])
        return products, inter

    def finish(h, issued):
        products, inter = issued
        hs = slice(h * HG_K, (h + 1) * HG_K)
        diag = jnp.where(eye, jnp.sum(qk[:, hs], axis=1, keepdims=True), 0.0)
        tiles = []
        for t in range(c // 8):
            rows = slice(8 * t, 8 * t + 8)
            a_t = diag[rows]
            for blk, p in zip(levels, products):
                mask = tile_mask(blk, t)
                if mask is not None:
                    a_t = jnp.where(mask, p[rows], a_t)
            tiles.append(a_t)
        a = jnp.concatenate(tiles, axis=0)
        o_ref[:, hs] = _dot(a.astype(BF16), v[:, hs]) + inter

    return issue, finish


def _hgrn_kernel(qf_ref, vf_ref, kf_ref, lff_ref, qb_ref, vb_ref, kb_ref, lfb_ref,
                 of_ref, ob_ref, sf_ref, sb_ref):
    @pl.when(pl.program_id(0) == 0)
    def _():
        sf_ref[...] = jnp.zeros_like(sf_ref)
        sb_ref[...] = jnp.zeros_like(sb_ref)

    units = []
    for args in ((qf_ref, vf_ref, kf_ref, lff_ref, of_ref, sf_ref, False),
                 (qb_ref, vb_ref, kb_ref, lfb_ref, ob_ref, sb_ref, True)):
        issue, finish = _hgrn_direction(*args)
        units += [(issue, finish, h) for h in range(HG_HEADS)]

    pending = None
    for issue, finish, h in units:
        issued = issue(h)
        if pending is not None:
            pending[0](pending[1], pending[2])
        pending = (finish, h, issued)
    pending[0](pending[1], pending[2])


def _hgrn_call(hq, hv, kf, lff, kb, lfb, n_real):
    nb = n_real // CHUNK
    fwd = lambda s: (jnp.where(s == 0, nb, s - 1), 0)
    bwd = lambda s: (jnp.where(s == nb, nb, nb - 1 - s), 0)
    spec = lambda m: pl.BlockSpec((CHUNK, HG_W), m)
    out = jax.ShapeDtypeStruct(((nb + 1) * CHUNK, HG_W), F32)
    return pl.pallas_call(
        _hgrn_kernel,
        out_shape=[out, out],
        grid=(nb + 1,),
        in_specs=[spec(fwd), spec(fwd), spec(fwd), spec(fwd), spec(bwd), spec(bwd), spec(bwd), spec(bwd)],
        out_specs=[spec(fwd), spec(bwd)],
        scratch_shapes=[pltpu.VMEM((HG_HEADS, HG_K, HG_K), F32), pltpu.VMEM((HG_HEADS, HG_K, HG_K), F32)],
        compiler_params=pltpu.CompilerParams(
            dimension_semantics=("arbitrary",), vmem_limit_bytes=VMEM_LIMIT),
        name="hgrn",
    )(hq, hv, kf, lff, hq, hv, kb, lfb)


def _flash_kernel(qT_ref, k_ref, vT_ref, kt_ref, vTt_ref, o_ref, qp_ref, m_ref, acc_ref, s_ref, mc_ref,
                  st_ref, mct_ref, *,
                  n_kv, n_tail_valid):
    g = pl.program_id(0)
    tq = qT_ref.shape[1]

    half = lax.broadcasted_iota(jnp.int32, (AT_KVW, tq), 0) >> 6
    for h in range(AT_GROUP):
        qh = qT_ref[h * AT_HD:(h + 1) * AT_HD, :].astype(F32)
        q2 = jnp.concatenate([qh, qh], axis=0)
        qp_ref[h] = jnp.where(half == g, q2, 0.0).astype(BF16)

    m_ref[...] = jnp.full(m_ref.shape, -jnp.inf, F32)
    acc_ref[...] = jnp.zeros(acc_ref.shape, F32)

    def store_scores(sT, s_dst, mc_dst, h):
        s_dst[h] = sT
        mc_dst[h] = jnp.max(sT, axis=0, keepdims=True)

    def scores(chunk, slot):
        off = pl.multiple_of(chunk * KV_TILE, KV_TILE)
        kc = k_ref[pl.ds(off, KV_TILE), :]
        for h in range(AT_GROUP):
            store_scores(_dot(kc, qp_ref[h]), s_ref.at[slot], mc_ref.at[slot], h)

    def consume(vc, s_src, mc_src):
        for h in range(AT_GROUP):
            m_prev = m_ref[h]
            m_new = jnp.maximum(m_prev, mc_src[h])
            alpha = jnp.exp2(m_prev - m_new)
            pT = jnp.exp2(s_src[h] - m_new).astype(BF16)
            acc_ref[h] = alpha * acc_ref[h] + _dot(vc, pT)
            m_ref[h] = m_new

    def consume_chunk(chunk, slot):
        off = pl.multiple_of(chunk * KV_TILE, KV_TILE)
        consume(vT_ref[:, pl.ds(off, KV_TILE)], s_ref.at[slot], mc_ref.at[slot])

    krow = lax.broadcasted_iota(jnp.int32, (128, 1), 0)
    for h in range(AT_GROUP):
        sT = jnp.where(krow < n_tail_valid, _dot(kt_ref[...], qp_ref[h]), -jnp.inf)
        store_scores(sT, st_ref, mct_ref, h)
    scores(0, 0)
    consume(vTt_ref[...], st_ref, mct_ref)

    def body(i, carry):
        scores(2 * i + 1, 1)
        consume_chunk(2 * i, 0)
        scores(jnp.minimum(2 * i + 2, n_kv - 1), 0)
        consume_chunk(2 * i + 1, 1)
        return carry

    lax.fori_loop(0, n_kv // 2, body, 0, unroll=FLASH_UNROLL)

    outs = []
    for h in range(AT_GROUP):
        acc = acc_ref[h]
        outs.append(acc[0:AT_HD, :] / acc[AT_HD:AT_HD + 1, :])
    o_ref[...] = jnp.transpose(jnp.concatenate(outs, axis=0)).astype(BF16)


def _flash_call(qT, k, vT, n_real):
    n_kv = n_real // KV_TILE
    tail_blk = n_real // 128
    gw = AT_GROUP * AT_HD
    return pl.pallas_call(
        functools.partial(_flash_kernel, n_kv=n_kv, n_tail_valid=N_META),
        out_shape=jax.ShapeDtypeStruct((n_real, AT_W), BF16),
        grid=(AT_KV_HEADS, n_real // Q_TILE),
        in_specs=[
            pl.BlockSpec((gw, Q_TILE), lambda g, i: (g, i)),
            pl.BlockSpec((n_real, AT_KVW), lambda g, i: (0, 0)),
            pl.BlockSpec((None, V_ROWS, n_real), lambda g, i: (g, 0, 0)),
            pl.BlockSpec((128, AT_KVW), lambda g, i: (tail_blk, 0)),
            pl.BlockSpec((None, V_ROWS, 128), lambda g, i: (g, 0, tail_blk)),
        ],
        out_specs=pl.BlockSpec((Q_TILE, gw), lambda g, i: (i, g)),
        scratch_shapes=[
            pltpu.VMEM((AT_GROUP, AT_KVW, Q_TILE), BF16),
            pltpu.VMEM((AT_GROUP, 1, Q_TILE), F32),
            pltpu.VMEM((AT_GROUP, V_ROWS, Q_TILE), F32),
            pltpu.VMEM((2, AT_GROUP, KV_TILE, Q_TILE), F32),
            pltpu.VMEM((2, AT_GROUP, 1, Q_TILE), F32),
            pltpu.VMEM((AT_GROUP, 128, Q_TILE), F32),
            pltpu.VMEM((AT_GROUP, 1, Q_TILE), F32),
        ],
        compiler_params=pltpu.CompilerParams(
            dimension_semantics=("arbitrary", "arbitrary"), vmem_limit_bytes=VMEM_LIMIT),
        name="flash",
    )(qT, k, vT, k, vT)


def _merge_ffn_kernel(h1_ref, of_ref, ob_ref, gs_ref, yb_ref, ga_ref, gb_ref, hgw_ref,
                      wua_ref, wub_ref, wout_ref, nw_ref, wg_ref, wu_ref, wd_ref, o_ref):
    o = of_ref[...] + ob_ref[...]
    normed = []
    for h in range(HG_HEADS):
        oh = o[:, h * HG_K:(h + 1) * HG_K]
        normed.append(oh * lax.rsqrt(jnp.mean(oh * oh, axis=-1, keepdims=True) + EPS))
    ya = (jnp.concatenate(normed, axis=1) * hgw_ref[...] * gs_ref[...]).astype(BF16)
    mixed = ga_ref[...] * _dot(ya, wua_ref[...]) + gb_ref[...] * _dot(yb_ref[...], wub_ref[...])
    h2 = h1_ref[...] + _dot(mixed.astype(BF16), wout_ref[...])
    o_ref[...] = _swiglu_half_step(h2, nw_ref[...], wg_ref, wu_ref, wd_ref)


def _merge_ffn_call(h1, o_f, o_b, gs, yb, ga, gb, hgw, wua, wub, wout, nw, wg, wu, wd, n_real):
    row_spec = lambda w: pl.BlockSpec((ROW_TILE, w), lambda i: (i, 0))
    return pl.pallas_call(
        _merge_ffn_kernel,
        out_shape=jax.ShapeDtypeStruct((n_real, D_MODEL), F32),
        grid=(n_real // ROW_TILE,),
        in_specs=[
            row_spec(D_MODEL), row_spec(HG_W), row_spec(HG_W), row_spec(HG_W), row_spec(AT_W),
            row_spec(D_MODEL), row_spec(D_MODEL),
            _const_spec((1, HG_W)),
            _const_spec((HG_W, D_MODEL)),
            _const_spec((AT_W, D_MODEL)),
            _const_spec((D_MODEL, D_MODEL)),
            _const_spec((1, D_MODEL)),
            _const_spec((D_MODEL, D_FF)),
            _const_spec((D_MODEL, D_FF)),
            _const_spec((D_FF, D_MODEL)),
        ],
        out_specs=row_spec(D_MODEL),
        compiler_params=pltpu.CompilerParams(
            dimension_semantics=("arbitrary",), vmem_limit_bytes=VMEM_LIMIT),
        name="merge_ffn2",
    )(h1, o_f, o_b, gs, yb, ga, gb, hgw, wua, wub, wout, nw, wg, wu, wd)


def _rope_tables(n_real):
    half = AT_HD // 2
    inv = ROPE_THETA ** (-jnp.arange(0, half, 2, dtype=F32) / half)
    sign = jnp.tile(jnp.array([-1.0, 1.0], F32), half)

    def lanes(ang, first_half):
        cos, sin = jnp.cos(ang), jnp.sin(ang)
        one, zero = jnp.ones_like(cos), jnp.zeros_like(sin)
        cos = jnp.concatenate([cos, one] if first_half else [one, cos], axis=-1)
        sin = jnp.concatenate([sin, zero] if first_half else [zero, sin], axis=-1)
        cos = jnp.repeat(cos, 2, axis=-1)
        sin = jnp.repeat(sin, 2, axis=-1) * sign
        reps = 128 // AT_HD
        return jnp.tile(cos, (1, reps)), jnp.tile(sin, (1, reps))

    r = jnp.arange(n_real // GRID_W, dtype=F32)
    c = jnp.arange(GRID_W, dtype=F32)
    return lanes(r[:, None] * inv, True) + lanes(c[:, None] * inv, False)


def _head_mean_matrix(width):
    heads = width // AT_HD
    return jnp.kron(jnp.eye(heads, dtype=F32), jnp.full((AT_HD, AT_HD), 1.0 / AT_HD, F32)).astype(BF16)


def kernel(x, meta_tokens, ffn1_norm, ffn1_w_gate, ffn1_w_up, ffn1_w_down, mix_norm, w_in, hg_lb_fwd, hg_lb_bwd, hg_out_norm, q_norm, k_norm, w_up_a, w_up_b, w_out, ffn2_norm, ffn2_w_gate, ffn2_w_up, ffn2_w_down):
    batch, n_real, _ = x.shape
    assert batch == 1 and n_real % ROW_TILE == 0 and n_real % GRID_W == 0
    rows = n_real + ROW_TILE
    n_valid = n_real + N_META

    tail = jnp.concatenate(
        [meta_tokens.astype(x.dtype), jnp.zeros((ROW_TILE - N_META, D_MODEL), x.dtype)], axis=0)
    row = lambda w: w.reshape(1, -1).astype(F32)
    (w1g, w1u, w1d, win, wua, wub, wout, w2g, w2u, w2d) = _cast_weights(
        [w[0].astype(F32) for w in (ffn1_w_gate, ffn1_w_up, ffn1_w_down, w_in, w_up_a, w_up_b, w_out,
                                    ffn2_w_gate, ffn2_w_up, ffn2_w_down)])

    h1 = _ffn_call(x[0], tail, row(ffn1_norm[0]), w1g, w1u, w1d)

    rope = _rope_tables(n_real)
    hq, hv, kf, lff, kb, lfb, gs, qT, k, vT, ga, gb = _mix_proj_call(
        h1, row(mix_norm[0]), win, hg_lb_fwd.astype(F32), hg_lb_bwd.astype(F32),
        row(jnp.tile(q_norm[0], AT_HEADS)), row(jnp.tile(k_norm[0], AT_KV_HEADS)), rope,
        _head_mean_matrix(AT_W), _head_mean_matrix(AT_KVW), n_valid)

    o_f, o_b = _hgrn_call(hq, hv, kf, lff, kb, lfb, n_real)
    yb = _flash_call(qT, k, vT, n_real)

    out = _merge_ffn_call(
        h1, o_f, o_b, gs, yb, ga, gb, row(hg_out_norm[0]), wua, wub, wout,
        row(ffn2_norm[0]), w2g, w2u, w2d, n_real)
    return out.reshape(batch, n_real, D_MODEL)
```

```python
import functools

import jax
import jax.numpy as jnp
from jax import lax
from jax.experimental import pallas as pl
from jax.experimental.pallas import tpu as pltpu

F32 = jnp.float32
BF16 = jnp.bfloat16

D_MODEL = 1024
D_FF = 2816
N_META = 16
GRID_W = 64
EPS = 1e-6
HG_HEADS = 4
HG_K = 128
HG_W = HG_HEADS * HG_K
AT_HEADS = 8
AT_KV_HEADS = 2
AT_HD = 64
AT_GROUP = AT_HEADS // AT_KV_HEADS
AT_W = AT_HEADS * AT_HD
AT_KVW = AT_KV_HEADS * AT_HD
ROPE_THETA = 10000.0
IN_SIZES = (HG_W, HG_W, HG_W, HG_W, HG_W, AT_W, AT_KVW, AT_KVW, D_MODEL, D_MODEL)
IN_OFFS = tuple(sum(IN_SIZES[:i]) for i in range(len(IN_SIZES)))
D_IN = sum(IN_SIZES)

ROW_TILE = 512
FF_CHUNK = 256
CHUNK = 128
Q_TILE = 256
KV_TILE = 256
V_ROWS = AT_HD + 16
LOG2E = 1.4426950408889634
Q_SCALE = AT_HD ** -0.5 * LOG2E
FLASH_UNROLL = 16
VMEM_LIMIT = 56 * 1024 * 1024


def _dot(a, b):
    return jnp.dot(a, b, preferred_element_type=F32)


def _dot_nt(a, b):
    return lax.dot_general(a, b, (((1,), (1,)), ((), ())), preferred_element_type=F32)


def _dot_tn(a, b):
    return lax.dot_general(a, b, (((0,), (0,)), ((), ())), preferred_element_type=F32)


def _rms(x, w):
    ms = jnp.mean(x * x, axis=-1, keepdims=True)
    return x * lax.rsqrt(ms + EPS) * w


def _sigmoid(x):
    return 0.5 * jnp.tanh(0.5 * x) + 0.5


def _silu(x):
    h = 0.5 * x
    return h * jnp.tanh(h) + h


def _swiglu_half_step(h, norm_w, wg_ref, wu_ref, wd_ref):
    hn = _rms(h, norm_w).astype(BF16)
    acc = None
    pending = None
    for c in range(D_FF // FF_CHUNK + 1):
        if c < D_FF // FF_CHUNK:
            sl = slice(c * FF_CHUNK, (c + 1) * FF_CHUNK)
            g = _dot(hn, wg_ref[:, sl])
            u = _dot(hn, wu_ref[:, sl])
        if pending is not None:
            d = _dot(pending[0], wd_ref[pending[1], :])
            acc = d if acc is None else acc + d
        if c < D_FF // FF_CHUNK:
            pending = ((_silu(g) * u).astype(BF16), sl)
    return h + 0.5 * acc


CAST_STEPS = 16


def _cast_kernel(*refs):
    n = len(refs) // 2
    for src, dst in zip(refs[:n], refs[n:]):
        dst[...] = src[...].astype(BF16)


def _cast_weights(ws):
    specs = [pl.BlockSpec((w.shape[0] // CAST_STEPS, w.shape[1]), lambda i: (i, 0)) for w in ws]
    return pl.pallas_call(
        _cast_kernel,
        out_shape=[jax.ShapeDtypeStruct(w.shape, BF16) for w in ws],
        grid=(CAST_STEPS,),
        in_specs=specs,
        out_specs=specs,
        compiler_params=pltpu.CompilerParams(
            dimension_semantics=("arbitrary",), vmem_limit_bytes=VMEM_LIMIT),
        name="cast_weights",
    )(*ws)


def _ffn_kernel(x_ref, tail_ref, nw_ref, wg_ref, wu_ref, wd_ref, o_ref, *, n_x_tiles):
    h = jnp.where(pl.program_id(0) < n_x_tiles, x_ref[...], tail_ref[...])
    o_ref[...] = _swiglu_half_step(h, nw_ref[...], wg_ref, wu_ref, wd_ref)


def _const_spec(shape):
    nd = len(shape)
    return pl.BlockSpec(shape, lambda *_: (0,) * nd, pipeline_mode=pl.Buffered(1))


def _ffn_call(x, tail, nw, wg, wu, wd):
    n_x_tiles = x.shape[0] // ROW_TILE
    rows = x.shape[0] + tail.shape[0]
    return pl.pallas_call(
        functools.partial(_ffn_kernel, n_x_tiles=n_x_tiles),
        out_shape=jax.ShapeDtypeStruct((rows, D_MODEL), F32),
        grid=(n_x_tiles + 1,),
        in_specs=[
            pl.BlockSpec((ROW_TILE, D_MODEL), lambda i: (jnp.minimum(i, n_x_tiles - 1), 0)),
            _const_spec((ROW_TILE, D_MODEL)),
            _const_spec((1, D_MODEL)),
            _const_spec((D_MODEL, D_FF)),
            _const_spec((D_MODEL, D_FF)),
            _const_spec((D_FF, D_MODEL)),
        ],
        out_specs=pl.BlockSpec((ROW_TILE, D_MODEL), lambda i: (i, 0)),
        compiler_params=pltpu.CompilerParams(
            dimension_semantics=("arbitrary",), vmem_limit_bytes=VMEM_LIMIT),
        name="ffn1",
    )(x, tail, nw, wg, wu, wd)


def _lower_bound_gate(z, lbp, valid):
    m = jnp.max(lbp, axis=0, keepdims=True)
    e = jnp.exp(lbp - m)
    lb = e[0:1, :] / jnp.sum(e, axis=0, keepdims=True)
    kk = (1.0 - lb) * _sigmoid(-z)
    kk = jnp.where(valid, kk, 0.0)
    return kk, jnp.log2(1.0 - kk)


def _head_rms_rope(z, ms, w, cos, sin):
    zn = z * lax.rsqrt(ms + EPS) * w
    width = z.shape[1]
    lane = lax.broadcasted_iota(jnp.int32, z.shape, 1)
    partner = jnp.where((lane & 1) == 0, pltpu.roll(zn, width - 1, 1), pltpu.roll(zn, 1, 1))
    return zn * cos + partner * sin


def _rope_tile(by_row_ref, by_col_ref, is_tail, tail_value):
    groups = ROW_TILE // GRID_W
    rp = by_row_ref[...]
    by_row = jnp.concatenate([jnp.broadcast_to(rp[j:j + 1, :], (GRID_W, 128)) for j in range(groups)], axis=0)
    by_col = jnp.concatenate([by_col_ref[...]] * groups, axis=0)
    lane = lax.broadcasted_iota(jnp.int32, (1, 128), 1)
    t = jnp.where((lane & (AT_HD - 1)) < AT_HD // 2, by_row, by_col)
    return jnp.where(is_tail, tail_value, t)


def _mix_proj_kernel(h_ref, nw_ref, win_ref, lbf_ref, lbb_ref, qnw_ref, knw_ref, rcos_ref, rsin_ref,
                     ccos_ref, csin_ref, bdq_ref, bdk_ref,
                     hq_ref, hv_ref, kf_ref, lff_ref, kb_ref, lfb_ref, gs_ref, qT_ref, k_ref, vT_ref,
                     ga_ref, gb_ref, *, n_valid, n_x_tiles):
    i = pl.program_id(0)
    un = _rms(h_ref[...], nw_ref[...]).astype(BF16)

    def proj(piece):
        off, size = IN_OFFS[piece], IN_SIZES[piece]
        return _dot(un, win_ref[:, off:off + size])

    row = i * ROW_TILE + lax.broadcasted_iota(jnp.int32, (ROW_TILE, 1), 0)
    valid = row < n_valid

    zq = proj(5)
    zk = proj(6)
    cos = _rope_tile(rcos_ref, ccos_ref, i >= n_x_tiles, 1.0)
    sin = _rope_tile(rsin_ref, csin_ref, i >= n_x_tiles, 0.0)
    cos_q = jnp.concatenate([cos] * (AT_W // 128), axis=1)
    sin_q = jnp.concatenate([sin] * (AT_W // 128), axis=1)
    ms_q = _dot((zq * zq).astype(BF16), bdq_ref[...])
    ms_k = _dot((zk * zk).astype(BF16), bdk_ref[...])
    q = _head_rms_rope(zq, ms_q, qnw_ref[...], cos_q, sin_q) * Q_SCALE
    qT_ref[...] = jnp.transpose(q).astype(BF16)
    k_ref[...] = _head_rms_rope(zk, ms_k, knw_ref[...], cos, sin).astype(BF16)

    hq_ref[...] = _silu(proj(0))
    hv_ref[...] = proj(1).astype(BF16)
    kf, lff = _lower_bound_gate(proj(2), lbf_ref[...], valid)
    kf_ref[...] = kf
    lff_ref[...] = lff
    kb, lfb = _lower_bound_gate(proj(3), lbb_ref[...], valid)
    kb_ref[...] = kb
    lfb_ref[...] = lfb
    gs_ref[...] = _silu(proj(4))

    vT = jnp.transpose(proj(7)).astype(BF16)
    ones = jnp.ones((V_ROWS - AT_HD, ROW_TILE), BF16)
    for g in range(AT_KV_HEADS):
        vT_ref[g, 0:AT_HD, :] = vT[g * AT_HD:(g + 1) * AT_HD, :]
        vT_ref[g, AT_HD:V_ROWS, :] = ones

    ga_ref[...] = _sigmoid(proj(8))
    gb_ref[...] = _sigmoid(proj(9))


def _mix_proj_call(h1, nw, win, lbf, lbb, qnw, knw, rope, bdq, bdk, n_valid):
    rows = h1.shape[0]
    n_x_tiles = rows // ROW_TILE - 1
    groups = ROW_TILE // GRID_W
    rope_row_spec = pl.BlockSpec((groups, 128), lambda i: (jnp.minimum(i, n_x_tiles - 1), 0))
    row_spec = lambda w: pl.BlockSpec((ROW_TILE, w), lambda i: (i, 0))
    out_shape = [
        jax.ShapeDtypeStruct((rows, HG_W), F32),
        jax.ShapeDtypeStruct((rows, HG_W), BF16),
        jax.ShapeDtypeStruct((rows, HG_W), F32),
        jax.ShapeDtypeStruct((rows, HG_W), F32),
        jax.ShapeDtypeStruct((rows, HG_W), F32),
        jax.ShapeDtypeStruct((rows, HG_W), F32),
        jax.ShapeDtypeStruct((rows, HG_W), F32),
        jax.ShapeDtypeStruct((AT_W, rows), BF16),
        jax.ShapeDtypeStruct((rows, AT_KVW), BF16),
        jax.ShapeDtypeStruct((AT_KV_HEADS, V_ROWS, rows), BF16),
        jax.ShapeDtypeStruct((rows, D_MODEL), F32),
        jax.ShapeDtypeStruct((rows, D_MODEL), F32),
    ]
    out_specs = [
        row_spec(HG_W), row_spec(HG_W), row_spec(HG_W), row_spec(HG_W), row_spec(HG_W), row_spec(HG_W),
        row_spec(HG_W),
        pl.BlockSpec((AT_W, ROW_TILE), lambda i: (0, i)),
        row_spec(AT_KVW),
        pl.BlockSpec((AT_KV_HEADS, V_ROWS, ROW_TILE), lambda i: (0, 0, i)),
        row_spec(D_MODEL), row_spec(D_MODEL),
    ]
    in_specs = [
        row_spec(D_MODEL),
        _const_spec((1, D_MODEL)),
        _const_spec((D_MODEL, D_IN)),
        _const_spec(lbf.shape),
        _const_spec(lbb.shape),
        _const_spec((1, AT_W)),
        _const_spec((1, AT_KVW)),
        rope_row_spec,
        rope_row_spec,
        _const_spec((GRID_W, 128)),
        _const_spec((GRID_W, 128)),
        _const_spec((AT_W, AT_W)),
        _const_spec((AT_KVW, AT_KVW)),
    ]
    return pl.pallas_call(
        functools.partial(_mix_proj_kernel, n_valid=n_valid, n_x_tiles=n_x_tiles),
        out_shape=out_shape,
        grid=(rows // ROW_TILE,),
        in_specs=in_specs,
        out_specs=out_specs,
        compiler_params=pltpu.CompilerParams(
            dimension_semantics=("arbitrary",), vmem_limit_bytes=VMEM_LIMIT),
        name="mix_proj",
    )(h1, nw, win, lbf, lbb, qnw, knw, *rope, bdq, bdk)


def _split2(x):
    hi = x.astype(BF16)
    lo = (x - hi.astype(F32)).astype(BF16)
    return hi, lo


def _level_operand(q, k, b, lf, blk, reverse):
    c = CHUNK
    if blk >= 8:
        pieces = []
        for g in range(c // (2 * blk)):
            r0 = g * 2 * blk
            lo, hi = slice(r0, r0 + blk), slice(r0 + blk, r0 + 2 * blk)
            if reverse:
                ref = b[r0 + blk:r0 + blk + 1, :]
                pieces.append(q[lo] * jnp.exp2(b[lo] - ref))
                pieces.append(k[hi] * jnp.exp2(ref - b[hi]))
            else:
                ref = b[r0 + blk - 1:r0 + blk, :]
                pieces.append(k[lo] * jnp.exp2(ref - b[lo]))
                pieces.append(q[hi] * jnp.exp2(b[hi] - ref))
        return jnp.concatenate(pieces, axis=0).astype(BF16)

    q3, k3, b3, lf3 = (x.reshape(c // 8, 8, HG_W) for x in (q, k, b, lf))
    sub = lax.broadcasted_iota(jnp.int32, (1, 8, 1), 1)
    later = ((sub >> (blk.bit_length() - 1)) & 1) == 1
    q_role = jnp.logical_not(later) if reverse else later
    if blk == 4:
        ref = b3[:, 4:5, :] if reverse else b3[:, 3:4, :]
        gl = -jnp.abs(b3 - ref)
    elif blk == 2:
        up = pltpu.roll(lf3, 7, 1)
        dn = pltpu.roll(lf3, 1, 1)
        m4 = sub & 3
        if reverse:
            gl = jnp.where(m4 == 0, lf3 + up, jnp.where(m4 == 1, lf3, jnp.where(m4 == 2, 0.0, dn)))
        else:
            gl = jnp.where(m4 == 0, up, jnp.where(m4 == 1, 0.0, jnp.where(m4 == 2, lf3, dn + lf3)))
    else:
        odd = (sub & 1) == 1
        gl = jnp.where(odd, 0.0, lf3) if reverse else jnp.where(odd, lf3, 0.0)
    y3 = jnp.where(q_role, q3, k3) * jnp.exp2(gl)
    return y3.reshape(c, HG_W).astype(BF16)


def _hgrn_direction(q_ref, v_ref, k_ref, lf_ref, o_ref, st_ref, reverse):
    c = CHUNK
    q = q_ref[...]
    k = k_ref[...]
    v = v_ref[...]
    lf = lf_ref[...]

    r_i = lax.broadcasted_iota(jnp.int32, (c, c), 0)
    c_i = lax.broadcasted_iota(jnp.int32, (c, c), 1)
    tri = ((c_i >= r_i) if reverse else (c_i <= r_i)).astype(BF16)
    hi, lo = _split2(lf)
    b = _dot(jnp.concatenate([tri, tri], axis=1), jnp.concatenate([hi, lo], axis=0))

    levels = (64, 32, 16, 8, 4, 2, 1)
    ys = [_level_operand(q, k, b, lf, blk, reverse) for blk in levels]

    lane = lax.broadcasted_iota(jnp.int32, (1, c), 1)
    small_valid = {}
    for blk in (4, 2, 1):
        sh = blk.bit_length() - 1
        same = (r_i >> (sh + 1)) == (c_i >> (sh + 1))
        t_later = ((r_i >> sh) & 1) == 1
        s_later = ((c_i >> sh) & 1) == 1
        small_valid[blk] = (same & jnp.logical_not(t_later) & s_later) if reverse else (
            same & t_later & jnp.logical_not(s_later))

    def tile_mask(blk, tile):
        if blk < 8:
            return small_valid[blk][8 * tile:8 * tile + 8, :]
        block = (8 * tile) // blk
        if (block % 2 == 1) == reverse:
            return None
        col0 = (block + 1) * blk if reverse else (block - 1) * blk
        return (lane >= col0) & (lane < col0 + blk)

    b_last = b[0:1, :] if reverse else b[c - 1:c, :]
    q_inter = (q * jnp.exp2(b)).astype(BF16)
    k_state = (k * jnp.exp2(b_last - b)).astype(BF16)
    e_last = jnp.exp2(b_last)
    qk = q * k
    eye = r_i == c_i

    def issue(h):
        hs = slice(h * HG_K, (h + 1) * HG_K)
        products = [_dot_nt(y[:, hs], y[:, hs]) for y in ys]
        st = st_ref[h]
        inter = _dot_nt(q_inter[:, hs], st.astype(BF16))
        st_ref[h] = st * e_last[:, hs] + _dot_tn(v[:, hs], k_state[:, hs])
        return products, inter

    def finish(h, issued):
        products, inter = issued
        hs = slice(h * HG_K, (h + 1) * HG_K)
        diag = jnp.where(eye, jnp.sum(qk[:, hs], axis=1, keepdims=True), 0.0)
        tiles = []
        for t in range(c // 8):
            rows = slice(8 * t, 8 * t + 8)
            a_t = diag[rows]
            for blk, p in zip(levels, products):
                mask = tile_mask(blk, t)
                if mask is not None:
                    a_t = jnp.where(mask, p[rows], a_t)
            tiles.append(a_t)
        a = jnp.concatenate(tiles, axis=0)
        o_ref[:, hs] = _dot(a.astype(BF16), v[:, hs]) + inter

    return issue, finish


def _hgrn_kernel(qf_ref, vf_ref, kf_ref, lff_ref, qb_ref, vb_ref, kb_ref, lfb_ref,
                 of_ref, ob_ref, sf_ref, sb_ref):
    @pl.when(pl.program_id(0) == 0)
    def _():
        sf_ref[...] = jnp.zeros_like(sf_ref)
        sb_ref[...] = jnp.zeros_like(sb_ref)

    units = []
    for args in ((qf_ref, vf_ref, kf_ref, lff_ref, of_ref, sf_ref, False),
                 (qb_ref, vb_ref, kb_ref, lfb_ref, ob_ref, sb_ref, True)):
        issue, finish = _hgrn_direction(*args)
        units += [(issue, finish, h) for h in range(HG_HEADS)]

    pending = None
    for issue, finish, h in units:
        issued = issue(h)
        if pending is not None:
            pending[0](pending[1], pending[2])
        pending = (finish, h, issued)
    pending[0](pending[1], pending[2])


def _hgrn_call(hq, hv, kf, lff, kb, lfb, n_real):
    nb = n_real // CHUNK
    fwd = lambda s: (jnp.where(s == 0, nb, s - 1), 0)
    bwd = lambda s: (jnp.where(s == nb, nb, nb - 1 - s), 0)
    spec = lambda m: pl.BlockSpec((CHUNK, HG_W), m)
    out = jax.ShapeDtypeStruct(((nb + 1) * CHUNK, HG_W), F32)
    return pl.pallas_call(
        _hgrn_kernel,
        out_shape=[out, out],
        grid=(nb + 1,),
        in_specs=[spec(fwd), spec(fwd), spec(fwd), spec(fwd), spec(bwd), spec(bwd), spec(bwd), spec(bwd)],
        out_specs=[spec(fwd), spec(bwd)],
        scratch_shapes=[pltpu.VMEM((HG_HEADS, HG_K, HG_K), F32), pltpu.VMEM((HG_HEADS, HG_K, HG_K), F32)],
        compiler_params=pltpu.CompilerParams(
            dimension_semantics=("arbitrary",), vmem_limit_bytes=VMEM_LIMIT),
        name="hgrn",
    )(hq, hv, kf, lff, hq, hv, kb, lfb)


def _flash_kernel(qT_ref, k_ref, vT_ref, kt_ref, vTt_ref, o_ref, qp_ref, m_ref, acc_ref, s_ref, mc_ref,
                  st_ref, mct_ref, *,
                  n_kv, n_tail_valid):
    g = pl.program_id(0)
    tq = qT_ref.shape[1]

    half = lax.broadcasted_iota(jnp.int32, (AT_KVW, tq), 0) >> 6
    for h in range(AT_GROUP):
        qh = qT_ref[h * AT_HD:(h + 1) * AT_HD, :].astype(F32)
        q2 = jnp.concatenate([qh, qh], axis=0)
        qp_ref[h] = jnp.where(half == g, q2, 0.0).astype(BF16)

    m_ref[...] = jnp.full(m_ref.shape, -jnp.inf, F32)
    acc_ref[...] = jnp.zeros(acc_ref.shape, F32)

    def store_scores(sT, s_dst, mc_dst, h):
        s_dst[h] = sT
        mc_dst[h] = jnp.max(sT, axis=0, keepdims=True)

    def scores(chunk, slot):
        off = pl.multiple_of(chunk * KV_TILE, KV_TILE)
        kc = k_ref[pl.ds(off, KV_TILE), :]
        for h in range(AT_GROUP):
            store_scores(_dot(kc, qp_ref[h]), s_ref.at[slot], mc_ref.at[slot], h)

    def consume(vc, s_src, mc_src):
        for h in range(AT_GROUP):
            m_prev = m_ref[h]
            m_new = jnp.maximum(m_prev, mc_src[h])
            alpha = jnp.exp2(m_prev - m_new)
            pT = jnp.exp2(s_src[h] - m_new).astype(BF16)
            acc_ref[h] = alpha * acc_ref[h] + _dot(vc, pT)
            m_ref[h] = m_new

    def consume_chunk(chunk, slot):
        off = pl.multiple_of(chunk * KV_TILE, KV_TILE)
        consume(vT_ref[:, pl.ds(off, KV_TILE)], s_ref.at[slot], mc_ref.at[slot])

    krow = lax.broadcasted_iota(jnp.int32, (128, 1), 0)
    for h in range(AT_GROUP):
        sT = jnp.where(krow < n_tail_valid, _dot(kt_ref[...], qp_ref[h]), -jnp.inf)
        store_scores(sT, st_ref, mct_ref, h)
    scores(0, 0)
    consume(vTt_ref[...], st_ref, mct_ref)

    def body(i, carry):
        scores(2 * i + 1, 1)
        consume_chunk(2 * i, 0)
        scores(jnp.minimum(2 * i + 2, n_kv - 1), 0)
        consume_chunk(2 * i + 1, 1)
        return carry

    lax.fori_loop(0, n_kv // 2, body, 0, unroll=FLASH_UNROLL)

    outs = []
    for h in range(AT_GROUP):
        acc = acc_ref[h]
        outs.append(acc[0:AT_HD, :] / acc[AT_HD:AT_HD + 1, :])
    o_ref[...] = jnp.transpose(jnp.concatenate(outs, axis=0)).astype(BF16)


def _flash_call(qT, k, vT, n_real):
    n_kv = n_real // KV_TILE
    tail_blk = n_real // 128
    gw = AT_GROUP * AT_HD
    return pl.pallas_call(
        functools.partial(_flash_kernel, n_kv=n_kv, n_tail_valid=N_META),
        out_shape=jax.ShapeDtypeStruct((n_real, AT_W), BF16),
        grid=(AT_KV_HEADS, n_real // Q_TILE),
        in_specs=[
            pl.BlockSpec((gw, Q_TILE), lambda g, i: (g, i)),
            pl.BlockSpec((n_real, AT_KVW), lambda g, i: (0, 0)),
            pl.BlockSpec((None, V_ROWS, n_real), lambda g, i: (g, 0, 0)),
            pl.BlockSpec((128, AT_KVW), lambda g, i: (tail_blk, 0)),
            pl.BlockSpec((None, V_ROWS, 128), lambda g, i: (g, 0, tail_blk)),
        ],
        out_specs=pl.BlockSpec((Q_TILE, gw), lambda g, i: (i, g)),
        scratch_shapes=[
            pltpu.VMEM((AT_GROUP, AT_KVW, Q_TILE), BF16),
            pltpu.VMEM((AT_GROUP, 1, Q_TILE), F32),
            pltpu.VMEM((AT_GROUP, V_ROWS, Q_TILE), F32),
            pltpu.VMEM((2, AT_GROUP, KV_TILE, Q_TILE), F32),
            pltpu.VMEM((2, AT_GROUP, 1, Q_TILE), F32),
            pltpu.VMEM((AT_GROUP, 128, Q_TILE), F32),
            pltpu.VMEM((AT_GROUP, 1, Q_TILE), F32),
        ],
        compiler_params=pltpu.CompilerParams(
            dimension_semantics=("arbitrary", "arbitrary"), vmem_limit_bytes=VMEM_LIMIT),
        name="flash",
    )(qT, k, vT, k, vT)


def _merge_ffn_kernel(h1_ref, of_ref, ob_ref, gs_ref, yb_ref, ga_ref, gb_ref, hgw_ref,
                      wua_ref, wub_ref, wout_ref, nw_ref, wg_ref, wu_ref, wd_ref, o_ref):
    o = of_ref[...] + ob_ref[...]
    normed = []
    for h in range(HG_HEADS):
        oh = o[:, h * HG_K:(h + 1) * HG_K]
        normed.append(oh * lax.rsqrt(jnp.mean(oh * oh, axis=-1, keepdims=True) + EPS))
    ya = (jnp.concatenate(normed, axis=1) * hgw_ref[...] * gs_ref[...]).astype(BF16)
    mixed = ga_ref[...] * _dot(ya, wua_ref[...]) + gb_ref[...] * _dot(yb_ref[...], wub_ref[...])
    h2 = h1_ref[...] + _dot(mixed.astype(BF16), wout_ref[...])
    o_ref[...] = _swiglu_half_step(h2, nw_ref[...], wg_ref, wu_ref, wd_ref)


def _merge_ffn_call(h1, o_f, o_b, gs, yb, ga, gb, hgw, wua, wub, wout, nw, wg, wu, wd, n_real):
    row_spec = lambda w: pl.BlockSpec((ROW_TILE, w), lambda i: (i, 0))
    return pl.pallas_call(
        _merge_ffn_kernel,
        out_shape=jax.ShapeDtypeStruct((n_real, D_MODEL), F32),
        grid=(n_real // ROW_TILE,),
        in_specs=[
            row_spec(D_MODEL), row_spec(HG_W), row_spec(HG_W), row_spec(HG_W), row_spec(AT_W),
            row_spec(D_MODEL), row_spec(D_MODEL),
            _const_spec((1, HG_W)),
            _const_spec((HG_W, D_MODEL)),
            _const_spec((AT_W, D_MODEL)),
            _const_spec((D_MODEL, D_MODEL)),
            _const_spec((1, D_MODEL)),
            _const_spec((D_MODEL, D_FF)),
            _const_spec((D_MODEL, D_FF)),
            _const_spec((D_FF, D_MODEL)),
        ],
        out_specs=row_spec(D_MODEL),
        compiler_params=pltpu.CompilerParams(
            dimension_semantics=("arbitrary",), vmem_limit_bytes=VMEM_LIMIT),
        name="merge_ffn2",
    )(h1, o_f, o_b, gs, yb, ga, gb, hgw, wua, wub, wout, nw, wg, wu, wd)


def _rope_tables(n_real):
    half = AT_HD // 2
    inv = ROPE_THETA ** (-jnp.arange(0, half, 2, dtype=F32) / half)
    sign = jnp.tile(jnp.array([-1.0, 1.0], F32), half)

    def lanes(ang, first_half):
        cos, sin = jnp.cos(ang), jnp.sin(ang)
        one, zero = jnp.ones_like(cos), jnp.zeros_like(sin)
        cos = jnp.concatenate([cos, one] if first_half else [one, cos], axis=-1)
        sin = jnp.concatenate([sin, zero] if first_half else [zero, sin], axis=-1)
        cos = jnp.repeat(cos, 2, axis=-1)
        sin = jnp.repeat(sin, 2, axis=-1) * sign
        reps = 128 // AT_HD
        return jnp.tile(cos, (1, reps)), jnp.tile(sin, (1, reps))

    r = jnp.arange(n_real // GRID_W, dtype=F32)
    c = jnp.arange(GRID_W, dtype=F32)
    return lanes(r[:, None] * inv, True) + lanes(c[:, None] * inv, False)


def _head_mean_matrix(width):
    heads = width // AT_HD
    return jnp.kron(jnp.eye(heads, dtype=F32), jnp.full((AT_HD, AT_HD), 1.0 / AT_HD, F32)).astype(BF16)


def kernel(x, meta_tokens, ffn1_norm, ffn1_w_gate, ffn1_w_up, ffn1_w_down, mix_norm, w_in, hg_lb_fwd, hg_lb_bwd, hg_out_norm, q_norm, k_norm, w_up_a, w_up_b, w_out, ffn2_norm, ffn2_w_gate, ffn2_w_up, ffn2_w_down):
    batch, n_real, _ = x.shape
    assert batch == 1 and n_real % ROW_TILE == 0 and n_real % GRID_W == 0
    rows = n_real + ROW_TILE
    n_valid = n_real + N_META

    tail = jnp.concatenate(
        [meta_tokens.astype(x.dtype), jnp.zeros((ROW_TILE - N_META, D_MODEL), x.dtype)], axis=0)
    row = lambda w: w.reshape(1, -1).astype(F32)
    (w1g, w1u, w1d, win, wua, wub, wout, w2g, w2u, w2d) = _cast_weights(
        [w[0].astype(F32) for w in (ffn1_w_gate, ffn1_w_up, ffn1_w_down, w_in, w_up_a, w_up_b, w_out,
                                    ffn2_w_gate, ffn2_w_up, ffn2_w_down)])

    h1 = _ffn_call(x[0], tail, row(ffn1_norm[0]), w1g, w1u, w1d)

    rope = _rope_tables(n_real)
    hq, hv, kf, lff, kb, lfb, gs, qT, k, vT, ga, gb = _mix_proj_call(
        h1, row(mix_norm[0]), win, hg_lb_fwd.astype(F32), hg_lb_bwd.astype(F32),
        row(jnp.tile(q_norm[0], AT_HEADS)), row(jnp.tile(k_norm[0], AT_KV_HEADS)), rope,
        _head_mean_matrix(AT_W), _head_mean_matrix(AT_KVW), n_valid)

    o_f, o_b = _hgrn_call(hq, hv, kf, lff, kb, lfb, n_real)
    yb = _flash_call(qT, k, vT, n_real)

    out = _merge_ffn_call(
        h1, o_f, o_b, gs, yb, ga, gb, row(hg_out_norm[0]), wua, wub, wout,
        row(ffn2_norm[0]), w2g, w2u, w2d, n_real)
    return out.reshape(batch, n_real, D_MODEL)
```

```python
import functools

import jax
import jax.numpy as jnp
from jax import lax
from jax.experimental import pallas as pl
from jax.experimental.pallas import tpu as pltpu

F32 = jnp.float32
BF16 = jnp.bfloat16

D_MODEL = 1024
D_FF = 2816
N_META = 16
GRID_W = 64
EPS = 1e-6
HG_HEADS = 4
HG_K = 128
HG_W = HG_HEADS * HG_K
AT_HEADS = 8
AT_KV_HEADS = 2
AT_HD = 64
AT_GROUP = AT_HEADS // AT_KV_HEADS
AT_W = AT_HEADS * AT_HD
AT_KVW = AT_KV_HEADS * AT_HD
ROPE_THETA = 10000.0
IN_SIZES = (HG_W, HG_W, HG_W, HG_W, HG_W, AT_W, AT_KVW, AT_KVW, D_MODEL, D_MODEL)
IN_OFFS = tuple(sum(IN_SIZES[:i]) for i in range(len(IN_SIZES)))
D_IN = sum(IN_SIZES)

ROW_TILE = 512
FF_CHUNK = 256
CHUNK = 128
HGRN_STEP_CHUNKS = 8
Q_TILE = 256
KV_TILE = 256
V_ROWS = AT_HD + 16
LOG2E = 1.4426950408889634
Q_SCALE = AT_HD ** -0.5 * LOG2E
FLASH_UNROLL = 16
VMEM_LIMIT = 56 * 1024 * 1024


def _dot(a, b):
    return jnp.dot(a, b, preferred_element_type=F32)


def _dot_nt(a, b):
    return lax.dot_general(a, b, (((1,), (1,)), ((), ())), preferred_element_type=F32)


def _dot_tn(a, b):
    return lax.dot_general(a, b, (((0,), (0,)), ((), ())), preferred_element_type=F32)


def _rms(x, w):
    ms = jnp.mean(x * x, axis=-1, keepdims=True)
    return x * lax.rsqrt(ms + EPS) * w


def _sigmoid(x):
    return 0.5 * jnp.tanh(0.5 * x) + 0.5


def _silu(x):
    h = 0.5 * x
    return h * jnp.tanh(h) + h


def _swiglu_half_step(h, norm_w, wg_ref, wu_ref, wd_ref):
    hn = _rms(h, norm_w).astype(BF16)
    acc = None
    pending = None
    for c in range(D_FF // FF_CHUNK + 1):
        if c < D_FF // FF_CHUNK:
            sl = slice(c * FF_CHUNK, (c + 1) * FF_CHUNK)
            g = _dot(hn, wg_ref[:, sl])
            u = _dot(hn, wu_ref[:, sl])
        if pending is not None:
            d = _dot(pending[0], wd_ref[pending[1], :])
            acc = d if acc is None else acc + d
        if c < D_FF // FF_CHUNK:
            pending = ((_silu(g) * u).astype(BF16), sl)
    return h + 0.5 * acc


CAST_STEPS = 16


def _cast_kernel(*refs):
    n = len(refs) // 2
    for src, dst in zip(refs[:n], refs[n:]):
        dst[...] = src[...].astype(BF16)


def _cast_weights(ws):
    specs = [pl.BlockSpec((w.shape[0] // CAST_STEPS, w.shape[1]), lambda i: (i, 0)) for w in ws]
    return pl.pallas_call(
        _cast_kernel,
        out_shape=[jax.ShapeDtypeStruct(w.shape, BF16) for w in ws],
        grid=(CAST_STEPS,),
        in_specs=specs,
        out_specs=specs,
        compiler_params=pltpu.CompilerParams(
            dimension_semantics=("arbitrary",), vmem_limit_bytes=VMEM_LIMIT),
        name="cast_weights",
    )(*ws)


def _ffn_kernel(x_ref, tail_ref, nw_ref, wg_ref, wu_ref, wd_ref, o_ref, *, n_x_tiles):
    h = jnp.where(pl.program_id(0) < n_x_tiles, x_ref[...], tail_ref[...])
    o_ref[...] = _swiglu_half_step(h, nw_ref[...], wg_ref, wu_ref, wd_ref)


def _const_spec(shape):
    nd = len(shape)
    return pl.BlockSpec(shape, lambda *_: (0,) * nd, pipeline_mode=pl.Buffered(1))


def _ffn_call(x, tail, nw, wg, wu, wd):
    n_x_tiles = x.shape[0] // ROW_TILE
    rows = x.shape[0] + tail.shape[0]
    return pl.pallas_call(
        functools.partial(_ffn_kernel, n_x_tiles=n_x_tiles),
        out_shape=jax.ShapeDtypeStruct((rows, D_MODEL), F32),
        grid=(n_x_tiles + 1,),
        in_specs=[
            pl.BlockSpec((ROW_TILE, D_MODEL), lambda i: (jnp.minimum(i, n_x_tiles - 1), 0)),
            _const_spec((ROW_TILE, D_MODEL)),
            _const_spec((1, D_MODEL)),
            _const_spec((D_MODEL, D_FF)),
            _const_spec((D_MODEL, D_FF)),
            _const_spec((D_FF, D_MODEL)),
        ],
        out_specs=pl.BlockSpec((ROW_TILE, D_MODEL), lambda i: (i, 0)),
        compiler_params=pltpu.CompilerParams(
            dimension_semantics=("arbitrary",), vmem_limit_bytes=VMEM_LIMIT),
        name="ffn1",
    )(x, tail, nw, wg, wu, wd)


def _lower_bound_gate(z, lbp, valid):
    m = jnp.max(lbp, axis=0, keepdims=True)
    e = jnp.exp(lbp - m)
    lb = e[0:1, :] / jnp.sum(e, axis=0, keepdims=True)
    kk = (1.0 - lb) * _sigmoid(-z)
    kk = jnp.where(valid, kk, 0.0)
    return kk, jnp.log2(1.0 - kk)


def _head_rms_rope(z, ms, w, cos, sin):
    zn = z * lax.rsqrt(ms + EPS) * w
    width = z.shape[1]
    lane = lax.broadcasted_iota(jnp.int32, z.shape, 1)
    partner = jnp.where((lane & 1) == 0, pltpu.roll(zn, width - 1, 1), pltpu.roll(zn, 1, 1))
    return zn * cos + partner * sin


def _rope_tile(by_row_ref, by_col_ref, is_tail, tail_value):
    groups = ROW_TILE // GRID_W
    rp = by_row_ref[...]
    by_row = jnp.concatenate([jnp.broadcast_to(rp[j:j + 1, :], (GRID_W, 128)) for j in range(groups)], axis=0)
    by_col = jnp.concatenate([by_col_ref[...]] * groups, axis=0)
    lane = lax.broadcasted_iota(jnp.int32, (1, 128), 1)
    t = jnp.where((lane & (AT_HD - 1)) < AT_HD // 2, by_row, by_col)
    return jnp.where(is_tail, tail_value, t)


def _mix_proj_kernel(h_ref, nw_ref, win_ref, lbf_ref, lbb_ref, qnw_ref, knw_ref, rcos_ref, rsin_ref,
                     ccos_ref, csin_ref, bdq_ref, bdk_ref,
                     hq_ref, hv_ref, kf_ref, lff_ref, kb_ref, lfb_ref, gs_ref, qT_ref, k_ref, vT_ref,
                     ga_ref, gb_ref, *, n_valid, n_x_tiles):
    i = pl.program_id(0)
    un = _rms(h_ref[...], nw_ref[...]).astype(BF16)

    def proj(piece):
        off, size = IN_OFFS[piece], IN_SIZES[piece]
        return _dot(un, win_ref[:, off:off + size])

    row = i * ROW_TILE + lax.broadcasted_iota(jnp.int32, (ROW_TILE, 1), 0)
    valid = row < n_valid

    zq = proj(5)
    zk = proj(6)
    cos = _rope_tile(rcos_ref, ccos_ref, i >= n_x_tiles, 1.0)
    sin = _rope_tile(rsin_ref, csin_ref, i >= n_x_tiles, 0.0)
    cos_q = jnp.concatenate([cos] * (AT_W // 128), axis=1)
    sin_q = jnp.concatenate([sin] * (AT_W // 128), axis=1)
    ms_q = _dot((zq * zq).astype(BF16), bdq_ref[...])
    ms_k = _dot((zk * zk).astype(BF16), bdk_ref[...])
    q = _head_rms_rope(zq, ms_q, qnw_ref[...], cos_q, sin_q) * Q_SCALE
    qT_ref[...] = jnp.transpose(q).astype(BF16)
    k_ref[...] = _head_rms_rope(zk, ms_k, knw_ref[...], cos, sin).astype(BF16)

    hq_ref[...] = _silu(proj(0))
    hv_ref[...] = proj(1).astype(BF16)
    kf, lff = _lower_bound_gate(proj(2), lbf_ref[...], valid)
    kf_ref[...] = kf
    lff_ref[...] = lff
    kb, lfb = _lower_bound_gate(proj(3), lbb_ref[...], valid)
    kb_ref[...] = kb
    lfb_ref[...] = lfb
    gs_ref[...] = _silu(proj(4))

    vT = jnp.transpose(proj(7)).astype(BF16)
    ones = jnp.ones((V_ROWS - AT_HD, ROW_TILE), BF16)
    for g in range(AT_KV_HEADS):
        vT_ref[g, 0:AT_HD, :] = vT[g * AT_HD:(g + 1) * AT_HD, :]
        vT_ref[g, AT_HD:V_ROWS, :] = ones

    ga_ref[...] = _sigmoid(proj(8))
    gb_ref[...] = _sigmoid(proj(9))


def _mix_proj_call(h1, nw, win, lbf, lbb, qnw, knw, rope, bdq, bdk, n_valid):
    rows = h1.shape[0]
    n_x_tiles = rows // ROW_TILE - 1
    groups = ROW_TILE // GRID_W
    rope_row_spec = pl.BlockSpec((groups, 128), lambda i: (jnp.minimum(i, n_x_tiles - 1), 0))
    row_spec = lambda w: pl.BlockSpec((ROW_TILE, w), lambda i: (i, 0))
    out_shape = [
        jax.ShapeDtypeStruct((rows, HG_W), F32),
        jax.ShapeDtypeStruct((rows, HG_W), BF16),
        jax.ShapeDtypeStruct((rows, HG_W), F32),
        jax.ShapeDtypeStruct((rows, HG_W), F32),
        jax.ShapeDtypeStruct((rows, HG_W), F32),
        jax.ShapeDtypeStruct((rows, HG_W), F32),
        jax.ShapeDtypeStruct((rows, HG_W), F32),
        jax.ShapeDtypeStruct((AT_W, rows), BF16),
        jax.ShapeDtypeStruct((rows, AT_KVW), BF16),
        jax.ShapeDtypeStruct((AT_KV_HEADS, V_ROWS, rows), BF16),
        jax.ShapeDtypeStruct((rows, D_MODEL), F32),
        jax.ShapeDtypeStruct((rows, D_MODEL), F32),
    ]
    out_specs = [
        row_spec(HG_W), row_spec(HG_W), row_spec(HG_W), row_spec(HG_W), row_spec(HG_W), row_spec(HG_W),
        row_spec(HG_W),
        pl.BlockSpec((AT_W, ROW_TILE), lambda i: (0, i)),
        row_spec(AT_KVW),
        pl.BlockSpec((AT_KV_HEADS, V_ROWS, ROW_TILE), lambda i: (0, 0, i)),
        row_spec(D_MODEL), row_spec(D_MODEL),
    ]
    in_specs = [
        row_spec(D_MODEL),
        _const_spec((1, D_MODEL)),
        _const_spec((D_MODEL, D_IN)),
        _const_spec(lbf.shape),
        _const_spec(lbb.shape),
        _const_spec((1, AT_W)),
        _const_spec((1, AT_KVW)),
        rope_row_spec,
        rope_row_spec,
        _const_spec((GRID_W, 128)),
        _const_spec((GRID_W, 128)),
        _const_spec((AT_W, AT_W)),
        _const_spec((AT_KVW, AT_KVW)),
    ]
    return pl.pallas_call(
        functools.partial(_mix_proj_kernel, n_valid=n_valid, n_x_tiles=n_x_tiles),
        out_shape=out_shape,
        grid=(rows // ROW_TILE,),
        in_specs=in_specs,
        out_specs=out_specs,
        compiler_params=pltpu.CompilerParams(
            dimension_semantics=("arbitrary",), vmem_limit_bytes=VMEM_LIMIT),
        name="mix_proj",
    )(h1, nw, win, lbf, lbb, qnw, knw, *rope, bdq, bdk)


def _split2(x):
    hi = x.astype(BF16)
    lo = (x - hi.astype(F32)).astype(BF16)
    return hi, lo


def _level_operand(q, k, b, lf, blk, reverse):
    c = CHUNK
    if blk >= 8:
        pieces = []
        for g in range(c // (2 * blk)):
            r0 = g * 2 * blk
            lo, hi = slice(r0, r0 + blk), slice(r0 + blk, r0 + 2 * blk)
            if reverse:
                ref = b[r0 + blk:r0 + blk + 1, :]
                pieces.append(q[lo] * jnp.exp2(b[lo] - ref))
                pieces.append(k[hi] * jnp.exp2(ref - b[hi]))
            else:
                ref = b[r0 + blk - 1:r0 + blk, :]
                pieces.append(k[lo] * jnp.exp2(ref - b[lo]))
                pieces.append(q[hi] * jnp.exp2(b[hi] - ref))
        return jnp.concatenate(pieces, axis=0).astype(BF16)

    q3, k3, b3, lf3 = (x.reshape(c // 8, 8, HG_W) for x in (q, k, b, lf))
    sub = lax.broadcasted_iota(jnp.int32, (1, 8, 1), 1)
    later = ((sub >> (blk.bit_length() - 1)) & 1) == 1
    q_role = jnp.logical_not(later) if reverse else later
    if blk == 4:
        ref = b3[:, 4:5, :] if reverse else b3[:, 3:4, :]
        gl = -jnp.abs(b3 - ref)
    elif blk == 2:
        up = pltpu.roll(lf3, 7, 1)
        dn = pltpu.roll(lf3, 1, 1)
        m4 = sub & 3
        if reverse:
            gl = jnp.where(m4 == 0, lf3 + up, jnp.where(m4 == 1, lf3, jnp.where(m4 == 2, 0.0, dn)))
        else:
            gl = jnp.where(m4 == 0, up, jnp.where(m4 == 1, 0.0, jnp.where(m4 == 2, lf3, dn + lf3)))
    else:
        odd = (sub & 1) == 1
        gl = jnp.where(odd, 0.0, lf3) if reverse else jnp.where(odd, lf3, 0.0)
    y3 = jnp.where(q_role, q3, k3) * jnp.exp2(gl)
    return y3.reshape(c, HG_W).astype(BF16)


def _hgrn_direction(q_ref, v_ref, k_ref, lf_ref, o_ref, st_ref, reverse):
    c = CHUNK
    q = q_ref[...]
    k = k_ref[...]
    v = v_ref[...]
    lf = lf_ref[...]

    r_i = lax.broadcasted_iota(jnp.int32, (c, c), 0)
    c_i = lax.broadcasted_iota(jnp.int32, (c, c), 1)
    tri = ((c_i >= r_i) if reverse else (c_i <= r_i)).astype(BF16)
    hi, lo = _split2(lf)
    b = _dot(jnp.concatenate([tri, tri], axis=1), jnp.concatenate([hi, lo], axis=0))

    levels = (64, 32, 16, 8, 4, 2, 1)
    ys = [_level_operand(q, k, b, lf, blk, reverse) for blk in levels]

    lane = lax.broadcasted_iota(jnp.int32, (1, c), 1)
    small_valid = {}
    for blk in (4, 2, 1):
        sh = blk.bit_length() - 1
        same = (r_i >> (sh + 1)) == (c_i >> (sh + 1))
        t_later = ((r_i >> sh) & 1) == 1
        s_later = ((c_i >> sh) & 1) == 1
        small_valid[blk] = (same & jnp.logical_not(t_later) & s_later) if reverse else (
            same & t_later & jnp.logical_not(s_later))

    def tile_mask(blk, tile):
        if blk < 8:
            return small_valid[blk][8 * tile:8 * tile + 8, :]
        block = (8 * tile) // blk
        if (block % 2 == 1) == reverse:
            return None
        col0 = (block + 1) * blk if reverse else (block - 1) * blk
        return (lane >= col0) & (lane < col0 + blk)

    b_last = b[0:1, :] if reverse else b[c - 1:c, :]
    q_inter = (q * jnp.exp2(b)).astype(BF16)
    k_state = (k * jnp.exp2(b_last - b)).astype(BF16)
    e_last = jnp.exp2(b_last)
    qk = q * k
    eye = r_i == c_i

    def issue(h):
        hs = slice(h * HG_K, (h + 1) * HG_K)
        products = [_dot_nt(y[:, hs], y[:, hs]) for y in ys]
        st = st_ref[h]
        inter = _dot_nt(q_inter[:, hs], st.astype(BF16))
        st_ref[h] = st * e_last[:, hs] + _dot_tn(v[:, hs], k_state[:, hs])
        return products, inter

    def finish(h, issued):
        products, inter = issued
        hs = slice(h * HG_K, (h + 1) * HG_K)
        diag = jnp.where(eye, jnp.sum(qk[:, hs], axis=1, keepdims=True), 0.0)
        tiles = []
        for t in range(c // 8):
            rows = slice(8 * t, 8 * t + 8)
            a_t = diag[rows]
            for blk, p in zip(levels, products):
                mask = tile_mask(blk, t)
                if mask is not None:
                    a_t = jnp.where(mask, p[rows], a_t)
            tiles.append(a_t)
        a = jnp.concatenate(tiles, axis=0)
        o_ref[:, hs] = _dot(a.astype(BF16), v[:, hs]) + inter

    return issue, finish


def _hgrn_meta_state(v_ref, k_ref, lf_ref, st_ref):
    c = CHUNK
    k = k_ref[...]
    v = v_ref[...]
    r_i = lax.broadcasted_iota(jnp.int32, (c, c), 0)
    c_i = lax.broadcasted_iota(jnp.int32, (c, c), 1)
    tri = (c_i <= r_i).astype(BF16)
    hi, lo = _split2(lf_ref[...])
    b = _dot(jnp.concatenate([tri, tri], axis=1), jnp.concatenate([hi, lo], axis=0))
    k_state = (k * jnp.exp2(b[c - 1:c, :] - b)).astype(BF16)
    for h in range(HG_HEADS):
        hs = slice(h * HG_K, (h + 1) * HG_K)
        st_ref[h] = _dot_tn(v[:, hs], k_state[:, hs])


def _hgrn_kernel(qf_ref, vf_ref, kf_ref, lff_ref, qb_ref, vb_ref, kb_ref, lfb_ref, vm_ref, kfm_ref, lfm_ref,
                 of_ref, ob_ref, sf_ref, sb_ref):
    @pl.when(pl.program_id(0) == 0)
    def _():
        sb_ref[...] = jnp.zeros_like(sb_ref)
        _hgrn_meta_state(vm_ref, kfm_ref, lfm_ref, sf_ref)

    def sub(ref, j):
        return ref.at[pl.ds(j * CHUNK, CHUNK)]

    units = []
    for j in range(HGRN_STEP_CHUNKS):
        jb = HGRN_STEP_CHUNKS - 1 - j
        fwd = (sub(qf_ref, j), sub(vf_ref, j), sub(kf_ref, j), sub(lff_ref, j), sub(of_ref, j), sf_ref, False)
        bwd = (sub(qb_ref, jb), sub(vb_ref, jb), sub(kb_ref, jb), sub(lfb_ref, jb), sub(ob_ref, jb), sb_ref, True)
        for args in (fwd, bwd):
            issue, finish = _hgrn_direction(*args)
            units += [(issue, finish, h) for h in range(HG_HEADS)]

    pending = None
    for issue, finish, h in units:
        issued = issue(h)
        if pending is not None:
            pending[0](pending[1], pending[2])
        pending = (finish, h, issued)
    pending[0](pending[1], pending[2])


def _hgrn_call(hq, hv, kf, lff, kb, lfb, n_real):
    nb = n_real // CHUNK
    steps = nb // HGRN_STEP_CHUNKS
    fwd = lambda s: (s, 0)
    bwd = lambda s: (steps - 1 - s, 0)
    spec = lambda m: pl.BlockSpec((HGRN_STEP_CHUNKS * CHUNK, HG_W), m)
    meta = pl.BlockSpec((CHUNK, HG_W), lambda s: (nb, 0))
    out = jax.ShapeDtypeStruct((n_real, HG_W), F32)
    return pl.pallas_call(
        _hgrn_kernel,
        out_shape=[out, out],
        grid=(steps,),
        in_specs=[spec(fwd), spec(fwd), spec(fwd), spec(fwd), spec(bwd), spec(bwd), spec(bwd), spec(bwd),
                  meta, meta, meta],
        out_specs=[spec(fwd), spec(bwd)],
        scratch_shapes=[pltpu.VMEM((HG_HEADS, HG_K, HG_K), F32), pltpu.VMEM((HG_HEADS, HG_K, HG_K), F32)],
        compiler_params=pltpu.CompilerParams(
            dimension_semantics=("arbitrary",), vmem_limit_bytes=VMEM_LIMIT),
        name="hgrn",
    )(hq, hv, kf, lff, hq, hv, kb, lfb, hv, kf, lff)


def _flash_kernel(qT_ref, k_ref, vT_ref, kt_ref, vTt_ref, o_ref, qp_ref, m_ref, acc_ref, s_ref, mc_ref,
                  st_ref, mct_ref, *,
                  n_kv, n_tail_valid):
    g = pl.program_id(0)
    tq = qT_ref.shape[1]

    half = lax.broadcasted_iota(jnp.int32, (AT_KVW, tq), 0) >> 6
    for h in range(AT_GROUP):
        qh = qT_ref[h * AT_HD:(h + 1) * AT_HD, :].astype(F32)
        q2 = jnp.concatenate([qh, qh], axis=0)
        qp_ref[h] = jnp.where(half == g, q2, 0.0).astype(BF16)

    m_ref[...] = jnp.full(m_ref.shape, -jnp.inf, F32)
    acc_ref[...] = jnp.zeros(acc_ref.shape, F32)

    def store_scores(sT, s_dst, mc_dst, h):
        s_dst[h] = sT
        mc_dst[h] = jnp.max(sT, axis=0, keepdims=True)

    def scores(chunk, slot):
        off = pl.multiple_of(chunk * KV_TILE, KV_TILE)
        kc = k_ref[pl.ds(off, KV_TILE), :]
        for h in range(AT_GROUP):
            store_scores(_dot(kc, qp_ref[h]), s_ref.at[slot], mc_ref.at[slot], h)

    def consume(vc, s_src, mc_src):
        for h in range(AT_GROUP):
            m_prev = m_ref[h]
            m_new = jnp.maximum(m_prev, mc_src[h])
            alpha = jnp.exp2(m_prev - m_new)
            pT = jnp.exp2(s_src[h] - m_new).astype(BF16)
            acc_ref[h] = alpha * acc_ref[h] + _dot(vc, pT)
            m_ref[h] = m_new

    def consume_chunk(chunk, slot):
        off = pl.multiple_of(chunk * KV_TILE, KV_TILE)
        consume(vT_ref[:, pl.ds(off, KV_TILE)], s_ref.at[slot], mc_ref.at[slot])

    krow = lax.broadcasted_iota(jnp.int32, (128, 1), 0)
    for h in range(AT_GROUP):
        sT = jnp.where(krow < n_tail_valid, _dot(kt_ref[...], qp_ref[h]), -jnp.inf)
        store_scores(sT, st_ref, mct_ref, h)
    scores(0, 0)
    consume(vTt_ref[...], st_ref, mct_ref)

    def body(i, carry):
        scores(2 * i + 1, 1)
        consume_chunk(2 * i, 0)
        scores(jnp.minimum(2 * i + 2, n_kv - 1), 0)
        consume_chunk(2 * i + 1, 1)
        return carry

    lax.fori_loop(0, n_kv // 2, body, 0, unroll=FLASH_UNROLL)

    outs = []
    for h in range(AT_GROUP):
        acc = acc_ref[h]
        outs.append(acc[0:AT_HD, :] / acc[AT_HD:AT_HD + 1, :])
    o_ref[...] = jnp.transpose(jnp.concatenate(outs, axis=0)).astype(BF16)


def _flash_call(qT, k, vT, n_real):
    n_kv = n_real // KV_TILE
    tail_blk = n_real // 128
    gw = AT_GROUP * AT_HD
    return pl.pallas_call(
        functools.partial(_flash_kernel, n_kv=n_kv, n_tail_valid=N_META),
        out_shape=jax.ShapeDtypeStruct((n_real, AT_W), BF16),
        grid=(AT_KV_HEADS, n_real // Q_TILE),
        in_specs=[
            pl.BlockSpec((gw, Q_TILE), lambda g, i: (g, i)),
            pl.BlockSpec((n_real, AT_KVW), lambda g, i: (0, 0)),
            pl.BlockSpec((None, V_ROWS, n_real), lambda g, i: (g, 0, 0)),
            pl.BlockSpec((128, AT_KVW), lambda g, i: (tail_blk, 0)),
            pl.BlockSpec((None, V_ROWS, 128), lambda g, i: (g, 0, tail_blk)),
        ],
        out_specs=pl.BlockSpec((Q_TILE, gw), lambda g, i: (i, g)),
        scratch_shapes=[
            pltpu.VMEM((AT_GROUP, AT_KVW, Q_TILE), BF16),
            pltpu.VMEM((AT_GROUP, 1, Q_TILE), F32),
            pltpu.VMEM((AT_GROUP, V_ROWS, Q_TILE), F32),
            pltpu.VMEM((2, AT_GROUP, KV_TILE, Q_TILE), F32),
            pltpu.VMEM((2, AT_GROUP, 1, Q_TILE), F32),
            pltpu.VMEM((AT_GROUP, 128, Q_TILE), F32),
            pltpu.VMEM((AT_GROUP, 1, Q_TILE), F32),
        ],
        compiler_params=pltpu.CompilerParams(
            dimension_semantics=("arbitrary", "arbitrary"), vmem_limit_bytes=VMEM_LIMIT),
        name="flash",
    )(qT, k, vT, k, vT)


def _merge_ffn_kernel(h1_ref, of_ref, ob_ref, gs_ref, yb_ref, ga_ref, gb_ref, hgw_ref,
                      wua_ref, wub_ref, wout_ref, nw_ref, wg_ref, wu_ref, wd_ref, o_ref):
    o = of_ref[...] + ob_ref[...]
    normed = []
    for h in range(HG_HEADS):
        oh = o[:, h * HG_K:(h + 1) * HG_K]
        normed.append(oh * lax.rsqrt(jnp.mean(oh * oh, axis=-1, keepdims=True) + EPS))
    ya = (jnp.concatenate(normed, axis=1) * hgw_ref[...] * gs_ref[...]).astype(BF16)
    mixed = ga_ref[...] * _dot(ya, wua_ref[...]) + gb_ref[...] * _dot(yb_ref[...], wub_ref[...])
    h2 = h1_ref[...] + _dot(mixed.astype(BF16), wout_ref[...])
    o_ref[...] = _swiglu_half_step(h2, nw_ref[...], wg_ref, wu_ref, wd_ref)


def _merge_ffn_call(h1, o_f, o_b, gs, yb, ga, gb, hgw, wua, wub, wout, nw, wg, wu, wd, n_real):
    row_spec = lambda w: pl.BlockSpec((ROW_TILE, w), lambda i: (i, 0))
    return pl.pallas_call(
        _merge_ffn_kernel,
        out_shape=jax.ShapeDtypeStruct((n_real, D_MODEL), F32),
        grid=(n_real // ROW_TILE,),
        in_specs=[
            row_spec(D_MODEL), row_spec(HG_W), row_spec(HG_W), row_spec(HG_W), row_spec(AT_W),
            row_spec(D_MODEL), row_spec(D_MODEL),
            _const_spec((1, HG_W)),
            _const_spec((HG_W, D_MODEL)),
            _const_spec((AT_W, D_MODEL)),
            _const_spec((D_MODEL, D_MODEL)),
            _const_spec((1, D_MODEL)),
            _const_spec((D_MODEL, D_FF)),
            _const_spec((D_MODEL, D_FF)),
            _const_spec((D_FF, D_MODEL)),
        ],
        out_specs=row_spec(D_MODEL),
        compiler_params=pltpu.CompilerParams(
            dimension_semantics=("arbitrary",), vmem_limit_bytes=VMEM_LIMIT),
        name="merge_ffn2",
    )(h1, o_f, o_b, gs, yb, ga, gb, hgw, wua, wub, wout, nw, wg, wu, wd)


def _rope_tables(n_real):
    half = AT_HD // 2
    inv = ROPE_THETA ** (-jnp.arange(0, half, 2, dtype=F32) / half)
    sign = jnp.tile(jnp.array([-1.0, 1.0], F32), half)

    def lanes(ang, first_half):
        cos, sin = jnp.cos(ang), jnp.sin(ang)
        one, zero = jnp.ones_like(cos), jnp.zeros_like(sin)
        cos = jnp.concatenate([cos, one] if first_half else [one, cos], axis=-1)
        sin = jnp.concatenate([sin, zero] if first_half else [zero, sin], axis=-1)
        cos = jnp.repeat(cos, 2, axis=-1)
        sin = jnp.repeat(sin, 2, axis=-1) * sign
        reps = 128 // AT_HD
        return jnp.tile(cos, (1, reps)), jnp.tile(sin, (1, reps))

    r = jnp.arange(n_real // GRID_W, dtype=F32)
    c = jnp.arange(GRID_W, dtype=F32)
    return lanes(r[:, None] * inv, True) + lanes(c[:, None] * inv, False)


def _head_mean_matrix(width):
    heads = width // AT_HD
    return jnp.kron(jnp.eye(heads, dtype=F32), jnp.full((AT_HD, AT_HD), 1.0 / AT_HD, F32)).astype(BF16)


def kernel(x, meta_tokens, ffn1_norm, ffn1_w_gate, ffn1_w_up, ffn1_w_down, mix_norm, w_in, hg_lb_fwd, hg_lb_bwd, hg_out_norm, q_norm, k_norm, w_up_a, w_up_b, w_out, ffn2_norm, ffn2_w_gate, ffn2_w_up, ffn2_w_down):
    batch, n_real, _ = x.shape
    assert batch == 1 and n_real % ROW_TILE == 0 and n_real % GRID_W == 0
    rows = n_real + ROW_TILE
    n_valid = n_real + N_META

    tail = jnp.concatenate(
        [meta_tokens.astype(x.dtype), jnp.zeros((ROW_TILE - N_META, D_MODEL), x.dtype)], axis=0)
    row = lambda w: w.reshape(1, -1).astype(F32)
    (w1g, w1u, w1d, win, wua, wub, wout, w2g, w2u, w2d) = _cast_weights(
        [w[0].astype(F32) for w in (ffn1_w_gate, ffn1_w_up, ffn1_w_down, w_in, w_up_a, w_up_b, w_out,
                                    ffn2_w_gate, ffn2_w_up, ffn2_w_down)])

    h1 = _ffn_call(x[0], tail, row(ffn1_norm[0]), w1g, w1u, w1d)

    rope = _rope_tables(n_real)
    hq, hv, kf, lff, kb, lfb, gs, qT, k, vT, ga, gb = _mix_proj_call(
        h1, row(mix_norm[0]), win, hg_lb_fwd.astype(F32), hg_lb_bwd.astype(F32),
        row(jnp.tile(q_norm[0], AT_HEADS)), row(jnp.tile(k_norm[0], AT_KV_HEADS)), rope,
        _head_mean_matrix(AT_W), _head_mean_matrix(AT_KVW), n_valid)

    o_f, o_b = _hgrn_call(hq, hv, kf, lff, kb, lfb, n_real)
    yb = _flash_call(qT, k, vT, n_real)

    out = _merge_ffn_call(
        h1, o_f, o_b, gs, yb, ga, gb, row(hg_out_norm[0]), wua, wub, wout,
        row(ffn2_norm[0]), w2g, w2u, w2d, n_real)
    return out.reshape(batch, n_real, D_MODEL)
```

```python
import functools

import jax
import jax.numpy as jnp
from jax import lax
from jax.experimental import pallas as pl
from jax.experimental.pallas import tpu as pltpu

F32 = jnp.float32
BF16 = jnp.bfloat16

D_MODEL = 1024
D_FF = 2816
N_META = 16
GRID_W = 64
EPS = 1e-6
HG_HEADS = 4
HG_K = 128
HG_W = HG_HEADS * HG_K
AT_HEADS = 8
AT_KV_HEADS = 2
AT_HD = 64
AT_GROUP = AT_HEADS // AT_KV_HEADS
AT_W = AT_HEADS * AT_HD
AT_KVW = AT_KV_HEADS * AT_HD
ROPE_THETA = 10000.0
IN_SIZES = (HG_W, HG_W, HG_W, HG_W, HG_W, AT_W, AT_KVW, AT_KVW, D_MODEL, D_MODEL)
IN_OFFS = tuple(sum(IN_SIZES[:i]) for i in range(len(IN_SIZES)))
D_IN = sum(IN_SIZES)

ROW_TILE = 512
FF_CHUNK = 256
CHUNK = 128
HGRN_STEP_CHUNKS = 8
Q_TILE = 256
KV_TILE = 256
V_ROWS = AT_HD + 16
LOG2E = 1.4426950408889634
Q_SCALE = AT_HD ** -0.5 * LOG2E
FLASH_UNROLL = 16
VMEM_LIMIT = 56 * 1024 * 1024


def _dot(a, b):
    return jnp.dot(a, b, preferred_element_type=F32)


def _dot_nt(a, b):
    return lax.dot_general(a, b, (((1,), (1,)), ((), ())), preferred_element_type=F32)


def _dot_tn(a, b):
    return lax.dot_general(a, b, (((0,), (0,)), ((), ())), preferred_element_type=F32)


def _rms(x, w):
    ms = jnp.mean(x * x, axis=-1, keepdims=True)
    return x * lax.rsqrt(ms + EPS) * w


def _sigmoid(x):
    return 0.5 * jnp.tanh(0.5 * x) + 0.5


def _silu(x):
    h = 0.5 * x
    return h * jnp.tanh(h) + h


def _swiglu_half_step(h, norm_w, wg_ref, wu_ref, wd_ref):
    hn = _rms(h, norm_w).astype(BF16)
    acc = None
    pending = None
    for c in range(D_FF // FF_CHUNK + 1):
        if c < D_FF // FF_CHUNK:
            sl = slice(c * FF_CHUNK, (c + 1) * FF_CHUNK)
            g = _dot(hn, wg_ref[:, sl])
            u = _dot(hn, wu_ref[:, sl])
        if pending is not None:
            d = _dot(pending[0], wd_ref[pending[1], :])
            acc = d if acc is None else acc + d
        if c < D_FF // FF_CHUNK:
            pending = ((_silu(g) * u).astype(BF16), sl)
    return h + 0.5 * acc


CAST_STEPS = 16


def _cast_kernel(*refs):
    n = len(refs) // 2
    for src, dst in zip(refs[:n], refs[n:]):
        dst[...] = src[...].astype(BF16)


def _cast_weights(ws):
    specs = [pl.BlockSpec((w.shape[0] // CAST_STEPS, w.shape[1]), lambda i: (i, 0)) for w in ws]
    return pl.pallas_call(
        _cast_kernel,
        out_shape=[jax.ShapeDtypeStruct(w.shape, BF16) for w in ws],
        grid=(CAST_STEPS,),
        in_specs=specs,
        out_specs=specs,
        compiler_params=pltpu.CompilerParams(
            dimension_semantics=("arbitrary",), vmem_limit_bytes=VMEM_LIMIT),
        name="cast_weights",
    )(*ws)


def _ffn_kernel(x_ref, tail_ref, nw_ref, wg_ref, wu_ref, wd_ref, o_ref, *, n_x_tiles):
    h = jnp.where(pl.program_id(0) < n_x_tiles, x_ref[...], tail_ref[...])
    o_ref[...] = _swiglu_half_step(h, nw_ref[...], wg_ref, wu_ref, wd_ref)


def _const_spec(shape):
    nd = len(shape)
    return pl.BlockSpec(shape, lambda *_: (0,) * nd, pipeline_mode=pl.Buffered(1))


def _ffn_call(x, tail, nw, wg, wu, wd):
    n_x_tiles = x.shape[0] // ROW_TILE
    rows = x.shape[0] + tail.shape[0]
    return pl.pallas_call(
        functools.partial(_ffn_kernel, n_x_tiles=n_x_tiles),
        out_shape=jax.ShapeDtypeStruct((rows, D_MODEL), F32),
        grid=(n_x_tiles + 1,),
        in_specs=[
            pl.BlockSpec((ROW_TILE, D_MODEL), lambda i: (jnp.minimum(i, n_x_tiles - 1), 0)),
            _const_spec((ROW_TILE, D_MODEL)),
            _const_spec((1, D_MODEL)),
            _const_spec((D_MODEL, D_FF)),
            _const_spec((D_MODEL, D_FF)),
            _const_spec((D_FF, D_MODEL)),
        ],
        out_specs=pl.BlockSpec((ROW_TILE, D_MODEL), lambda i: (i, 0)),
        compiler_params=pltpu.CompilerParams(
            dimension_semantics=("arbitrary",), vmem_limit_bytes=VMEM_LIMIT),
        name="ffn1",
    )(x, tail, nw, wg, wu, wd)


def _lower_bound_gate(z, lbp, valid):
    m = jnp.max(lbp, axis=0, keepdims=True)
    e = jnp.exp(lbp - m)
    lb = e[0:1, :] / jnp.sum(e, axis=0, keepdims=True)
    kk = (1.0 - lb) * _sigmoid(-z)
    kk = jnp.where(valid, kk, 0.0)
    return kk, jnp.log2(1.0 - kk)


def _head_rms_rope(z, ms, w, cos, sin):
    zn = z * lax.rsqrt(ms + EPS) * w
    width = z.shape[1]
    lane = lax.broadcasted_iota(jnp.int32, z.shape, 1)
    partner = jnp.where((lane & 1) == 0, pltpu.roll(zn, width - 1, 1), pltpu.roll(zn, 1, 1))
    return zn * cos + partner * sin


def _rope_tile(by_row_ref, by_col_ref, is_tail, tail_value):
    groups = ROW_TILE // GRID_W
    rp = by_row_ref[...]
    by_row = jnp.concatenate([jnp.broadcast_to(rp[j:j + 1, :], (GRID_W, 128)) for j in range(groups)], axis=0)
    by_col = jnp.concatenate([by_col_ref[...]] * groups, axis=0)
    lane = lax.broadcasted_iota(jnp.int32, (1, 128), 1)
    t = jnp.where((lane & (AT_HD - 1)) < AT_HD // 2, by_row, by_col)
    return jnp.where(is_tail, tail_value, t)


def _mix_proj_kernel(h_ref, nw_ref, win_ref, lbf_ref, lbb_ref, qnw_ref, knw_ref, rcos_ref, rsin_ref,
                     ccos_ref, csin_ref, bdq_ref, bdk_ref,
                     hq_ref, hv_ref, kf_ref, lff_ref, kb_ref, lfb_ref, gs_ref, qT_ref, k_ref, vT_ref,
                     ga_ref, gb_ref, *, n_valid, n_x_tiles):
    i = pl.program_id(0)
    un = _rms(h_ref[...], nw_ref[...]).astype(BF16)

    def proj(piece):
        off, size = IN_OFFS[piece], IN_SIZES[piece]
        return _dot(un, win_ref[:, off:off + size])

    row = i * ROW_TILE + lax.broadcasted_iota(jnp.int32, (ROW_TILE, 1), 0)
    valid = row < n_valid

    zq = proj(5)
    zk = proj(6)
    cos = _rope_tile(rcos_ref, ccos_ref, i >= n_x_tiles, 1.0)
    sin = _rope_tile(rsin_ref, csin_ref, i >= n_x_tiles, 0.0)
    cos_q = jnp.concatenate([cos] * (AT_W // 128), axis=1)
    sin_q = jnp.concatenate([sin] * (AT_W // 128), axis=1)
    ms_q = _dot((zq * zq).astype(BF16), bdq_ref[...])
    ms_k = _dot((zk * zk).astype(BF16), bdk_ref[...])
    q = _head_rms_rope(zq, ms_q, qnw_ref[...], cos_q, sin_q) * Q_SCALE
    qT_ref[...] = jnp.transpose(q).astype(BF16)
    k_ref[...] = _head_rms_rope(zk, ms_k, knw_ref[...], cos, sin).astype(BF16)

    hq_ref[...] = _silu(proj(0))
    hv_ref[...] = proj(1).astype(BF16)
    kf, lff = _lower_bound_gate(proj(2), lbf_ref[...], valid)
    kf_ref[...] = kf
    lff_ref[...] = lff
    kb, lfb = _lower_bound_gate(proj(3), lbb_ref[...], valid)
    kb_ref[...] = kb
    lfb_ref[...] = lfb
    gs_ref[...] = _silu(proj(4))

    vT = jnp.transpose(proj(7)).astype(BF16)
    ones = jnp.ones((V_ROWS - AT_HD, ROW_TILE), BF16)
    for g in range(AT_KV_HEADS):
        vT_ref[g, 0:AT_HD, :] = vT[g * AT_HD:(g + 1) * AT_HD, :]
        vT_ref[g, AT_HD:V_ROWS, :] = ones

    ga_ref[...] = _sigmoid(proj(8))
    gb_ref[...] = _sigmoid(proj(9))


def _mix_proj_call(h1, nw, win, lbf, lbb, qnw, knw, rope, bdq, bdk, n_valid):
    rows = h1.shape[0]
    n_x_tiles = rows // ROW_TILE - 1
    groups = ROW_TILE // GRID_W
    rope_row_spec = pl.BlockSpec((groups, 128), lambda i: (jnp.minimum(i, n_x_tiles - 1), 0))
    row_spec = lambda w: pl.BlockSpec((ROW_TILE, w), lambda i: (i, 0))
    out_shape = [
        jax.ShapeDtypeStruct((rows, HG_W), F32),
        jax.ShapeDtypeStruct((rows, HG_W), BF16),
        jax.ShapeDtypeStruct((rows, HG_W), F32),
        jax.ShapeDtypeStruct((rows, HG_W), F32),
        jax.ShapeDtypeStruct((rows, HG_W), F32),
        jax.ShapeDtypeStruct((rows, HG_W), F32),
        jax.ShapeDtypeStruct((rows, HG_W), F32),
        jax.ShapeDtypeStruct((AT_W, rows), BF16),
        jax.ShapeDtypeStruct((rows, AT_KVW), BF16),
        jax.ShapeDtypeStruct((AT_KV_HEADS, V_ROWS, rows), BF16),
        jax.ShapeDtypeStruct((rows, D_MODEL), F32),
        jax.ShapeDtypeStruct((rows, D_MODEL), F32),
    ]
    out_specs = [
        row_spec(HG_W), row_spec(HG_W), row_spec(HG_W), row_spec(HG_W), row_spec(HG_W), row_spec(HG_W),
        row_spec(HG_W),
        pl.BlockSpec((AT_W, ROW_TILE), lambda i: (0, i)),
        row_spec(AT_KVW),
        pl.BlockSpec((AT_KV_HEADS, V_ROWS, ROW_TILE), lambda i: (0, 0, i)),
        row_spec(D_MODEL), row_spec(D_MODEL),
    ]
    in_specs = [
        row_spec(D_MODEL),
        _const_spec((1, D_MODEL)),
        _const_spec((D_MODEL, D_IN)),
        _const_spec(lbf.shape),
        _const_spec(lbb.shape),
        _const_spec((1, AT_W)),
        _const_spec((1, AT_KVW)),
        rope_row_spec,
        rope_row_spec,
        _const_spec((GRID_W, 128)),
        _const_spec((GRID_W, 128)),
        _const_spec((AT_W, AT_W)),
        _const_spec((AT_KVW, AT_KVW)),
    ]
    return pl.pallas_call(
        functools.partial(_mix_proj_kernel, n_valid=n_valid, n_x_tiles=n_x_tiles),
        out_shape=out_shape,
        grid=(rows // ROW_TILE,),
        in_specs=in_specs,
        out_specs=out_specs,
        compiler_params=pltpu.CompilerParams(
            dimension_semantics=("arbitrary",), vmem_limit_bytes=VMEM_LIMIT),
        name="mix_proj",
    )(h1, nw, win, lbf, lbb, qnw, knw, *rope, bdq, bdk)


def _split2(x):
    hi = x.astype(BF16)
    lo = (x - hi.astype(F32)).astype(BF16)
    return hi, lo


def _level_operand(q, k, b, lf, blk, reverse):
    c = CHUNK
    if blk >= 8:
        pieces = []
        for g in range(c // (2 * blk)):
            r0 = g * 2 * blk
            lo, hi = slice(r0, r0 + blk), slice(r0 + blk, r0 + 2 * blk)
            if reverse:
                ref = b[r0 + blk:r0 + blk + 1, :]
                pieces.append(q[lo] * jnp.exp2(b[lo] - ref))
                pieces.append(k[hi] * jnp.exp2(ref - b[hi]))
            else:
                ref = b[r0 + blk - 1:r0 + blk, :]
                pieces.append(k[lo] * jnp.exp2(ref - b[lo]))
                pieces.append(q[hi] * jnp.exp2(b[hi] - ref))
        return jnp.concatenate(pieces, axis=0).astype(BF16)

    q3, k3, b3, lf3 = (x.reshape(c // 8, 8, HG_W) for x in (q, k, b, lf))
    sub = lax.broadcasted_iota(jnp.int32, (1, 8, 1), 1)
    later = ((sub >> (blk.bit_length() - 1)) & 1) == 1
    q_role = jnp.logical_not(later) if reverse else later
    if blk == 4:
        ref = b3[:, 4:5, :] if reverse else b3[:, 3:4, :]
        gl = -jnp.abs(b3 - ref)
    elif blk == 2:
        up = pltpu.roll(lf3, 7, 1)
        dn = pltpu.roll(lf3, 1, 1)
        m4 = sub & 3
        if reverse:
            gl = jnp.where(m4 == 0, lf3 + up, jnp.where(m4 == 1, lf3, jnp.where(m4 == 2, 0.0, dn)))
        else:
            gl = jnp.where(m4 == 0, up, jnp.where(m4 == 1, 0.0, jnp.where(m4 == 2, lf3, dn + lf3)))
    else:
        odd = (sub & 1) == 1
        gl = jnp.where(odd, 0.0, lf3) if reverse else jnp.where(odd, lf3, 0.0)
    y3 = jnp.where(q_role, q3, k3) * jnp.exp2(gl)
    return y3.reshape(c, HG_W).astype(BF16)


def _hgrn_direction(q_ref, v_ref, k_ref, lf_ref, o_ref, st_ref, reverse):
    c = CHUNK
    q = q_ref[...]
    k = k_ref[...]
    v = v_ref[...]
    lf = lf_ref[...]

    r_i = lax.broadcasted_iota(jnp.int32, (c, c), 0)
    c_i = lax.broadcasted_iota(jnp.int32, (c, c), 1)
    tri = ((c_i >= r_i) if reverse else (c_i <= r_i)).astype(BF16)
    hi, lo = _split2(lf)
    b = _dot(jnp.concatenate([tri, tri], axis=1), jnp.concatenate([hi, lo], axis=0))

    levels = (64, 32, 16, 8, 4, 2, 1)
    ys = [_level_operand(q, k, b, lf, blk, reverse) for blk in levels]

    lane = lax.broadcasted_iota(jnp.int32, (1, c), 1)
    small_valid = {}
    for blk in (4, 2, 1):
        sh = blk.bit_length() - 1
        same = (r_i >> (sh + 1)) == (c_i >> (sh + 1))
        t_later = ((r_i >> sh) & 1) == 1
        s_later = ((c_i >> sh) & 1) == 1
        small_valid[blk] = (same & jnp.logical_not(t_later) & s_later) if reverse else (
            same & t_later & jnp.logical_not(s_later))

    def tile_mask(blk, tile):
        if blk < 8:
            return small_valid[blk][8 * tile:8 * tile + 8, :]
        block = (8 * tile) // blk
        if (block % 2 == 1) == reverse:
            return None
        col0 = (block + 1) * blk if reverse else (block - 1) * blk
        return (lane >= col0) & (lane < col0 + blk)

    b_last = b[0:1, :] if reverse else b[c - 1:c, :]
    q_inter = (q * jnp.exp2(b)).astype(BF16)
    k_state = (k * jnp.exp2(b_last - b)).astype(BF16)
    e_last = jnp.exp2(b_last)
    qk = q * k
    eye = r_i == c_i

    def issue(h):
        hs = slice(h * HG_K, (h + 1) * HG_K)
        products = [_dot_nt(y[:, hs], y[:, hs]) for y in ys]
        st = st_ref[h]
        inter = _dot_nt(q_inter[:, hs], st.astype(BF16))
        st_ref[h] = st * e_last[:, hs] + _dot_tn(v[:, hs], k_state[:, hs])
        return products, inter

    def finish(h, issued):
        products, inter = issued
        hs = slice(h * HG_K, (h + 1) * HG_K)
        diag = jnp.where(eye, jnp.sum(qk[:, hs], axis=1, keepdims=True), 0.0)
        tiles = []
        for t in range(c // 8):
            rows = slice(8 * t, 8 * t + 8)
            a_t = diag[rows]
            for blk, p in zip(levels, products):
                mask = tile_mask(blk, t)
                if mask is not None:
                    a_t = jnp.where(mask, p[rows], a_t)
            tiles.append(a_t)
        a = jnp.concatenate(tiles, axis=0)
        o_ref[:, hs] = _dot(a.astype(BF16), v[:, hs]) + inter

    return issue, finish


def _hgrn_meta_state(v_ref, k_ref, lf_ref, st_ref):
    c = CHUNK
    k = k_ref[...]
    v = v_ref[...]
    r_i = lax.broadcasted_iota(jnp.int32, (c, c), 0)
    c_i = lax.broadcasted_iota(jnp.int32, (c, c), 1)
    tri = (c_i <= r_i).astype(BF16)
    hi, lo = _split2(lf_ref[...])
    b = _dot(jnp.concatenate([tri, tri], axis=1), jnp.concatenate([hi, lo], axis=0))
    k_state = (k * jnp.exp2(b[c - 1:c, :] - b)).astype(BF16)
    for h in range(HG_HEADS):
        hs = slice(h * HG_K, (h + 1) * HG_K)
        st_ref[h] = _dot_tn(v[:, hs], k_state[:, hs])


def _hgrn_kernel(qf_ref, vf_ref, kf_ref, lff_ref, qb_ref, vb_ref, kb_ref, lfb_ref, vm_ref, kfm_ref, lfm_ref,
                 of_ref, ob_ref, sf_ref, sb_ref):
    @pl.when(pl.program_id(0) == 0)
    def _():
        sb_ref[...] = jnp.zeros_like(sb_ref)
        _hgrn_meta_state(vm_ref, kfm_ref, lfm_ref, sf_ref)

    def sub(ref, j):
        return ref.at[pl.ds(j * CHUNK, CHUNK)]

    units = []
    for j in range(HGRN_STEP_CHUNKS):
        jb = HGRN_STEP_CHUNKS - 1 - j
        fwd = (sub(qf_ref, j), sub(vf_ref, j), sub(kf_ref, j), sub(lff_ref, j), sub(of_ref, j), sf_ref, False)
        bwd = (sub(qb_ref, jb), sub(vb_ref, jb), sub(kb_ref, jb), sub(lfb_ref, jb), sub(ob_ref, jb), sb_ref, True)
        for args in (fwd, bwd):
            issue, finish = _hgrn_direction(*args)
            units += [(issue, finish, h) for h in range(HG_HEADS)]

    pending = None
    for issue, finish, h in units:
        issued = issue(h)
        if pending is not None:
            pending[0](pending[1], pending[2])
        pending = (finish, h, issued)
    pending[0](pending[1], pending[2])


def _hgrn_call(hq, hv, kf, lff, kb, lfb, n_real):
    nb = n_real // CHUNK
    steps = nb // HGRN_STEP_CHUNKS
    fwd = lambda s: (s, 0)
    bwd = lambda s: (steps - 1 - s, 0)
    spec = lambda m: pl.BlockSpec((HGRN_STEP_CHUNKS * CHUNK, HG_W), m)
    meta = pl.BlockSpec((CHUNK, HG_W), lambda s: (nb, 0))
    out = jax.ShapeDtypeStruct((n_real, HG_W), F32)
    return pl.pallas_call(
        _hgrn_kernel,
        out_shape=[out, out],
        grid=(steps,),
        in_specs=[spec(fwd), spec(fwd), spec(fwd), spec(fwd), spec(bwd), spec(bwd), spec(bwd), spec(bwd),
                  meta, meta, meta],
        out_specs=[spec(fwd), spec(bwd)],
        scratch_shapes=[pltpu.VMEM((HG_HEADS, HG_K, HG_K), F32), pltpu.VMEM((HG_HEADS, HG_K, HG_K), F32)],
        compiler_params=pltpu.CompilerParams(
            dimension_semantics=("arbitrary",), vmem_limit_bytes=VMEM_LIMIT),
        name="hgrn",
    )(hq, hv, kf, lff, hq, hv, kb, lfb, hv, kf, lff)


def _flash_kernel(qT_ref, k_ref, vT_ref, kt_ref, vTt_ref, o_ref, qp_ref, m_ref, acc_ref, s_ref, mc_ref,
                  st_ref, mct_ref, *,
                  n_kv, n_tail_valid):
    g = pl.program_id(0)
    tq = qT_ref.shape[1]

    half = lax.broadcasted_iota(jnp.int32, (AT_KVW, tq), 0) >> 6
    for h in range(AT_GROUP):
        qh = qT_ref[h * AT_HD:(h + 1) * AT_HD, :].astype(F32)
        q2 = jnp.concatenate([qh, qh], axis=0)
        qp_ref[h] = jnp.where(half == g, q2, 0.0).astype(BF16)

    m_ref[...] = jnp.full(m_ref.shape, -jnp.inf, F32)
    acc_ref[...] = jnp.zeros(acc_ref.shape, F32)

    def store_scores(sT, s_dst, mc_dst, h):
        s_dst[h] = sT
        mc_dst[h] = jnp.max(sT, axis=0, keepdims=True)

    def scores(chunk, slot):
        off = pl.multiple_of(chunk * KV_TILE, KV_TILE)
        kc = k_ref[pl.ds(off, KV_TILE), :]
        for h in range(AT_GROUP):
            store_scores(_dot(kc, qp_ref[h]), s_ref.at[slot], mc_ref.at[slot], h)

    def consume(vc, s_src, mc_src):
        for h in range(AT_GROUP):
            m_prev = m_ref[h]
            m_new = jnp.maximum(m_prev, mc_src[h])
            alpha = jnp.exp2(m_prev - m_new)
            pT = jnp.exp2(s_src[h] - m_new).astype(BF16)
            acc_ref[h] = alpha * acc_ref[h] + _dot(vc, pT)
            m_ref[h] = m_new

    def consume_chunk(chunk, slot):
        off = pl.multiple_of(chunk * KV_TILE, KV_TILE)
        consume(vT_ref[:, pl.ds(off, KV_TILE)], s_ref.at[slot], mc_ref.at[slot])

    krow = lax.broadcasted_iota(jnp.int32, (128, 1), 0)
    for h in range(AT_GROUP):
        sT = jnp.where(krow < n_tail_valid, _dot(kt_ref[...], qp_ref[h]), -jnp.inf)
        store_scores(sT, st_ref, mct_ref, h)
    scores(0, 0)
    consume(vTt_ref[...], st_ref, mct_ref)

    def half_step(next_chunk, cur_chunk, cur_slot):
        kc = k_ref[pl.ds(pl.multiple_of(next_chunk * KV_TILE, KV_TILE), KV_TILE), :]
        vc = vT_ref[:, pl.ds(pl.multiple_of(cur_chunk * KV_TILE, KV_TILE), KV_TILE)]
        for h in range(AT_GROUP):
            store_scores(_dot(kc, qp_ref[h]), s_ref.at[1 - cur_slot], mc_ref.at[1 - cur_slot], h)
            m_prev = m_ref[h]
            m_new = jnp.maximum(m_prev, mc_ref[cur_slot, h])
            alpha = jnp.exp2(m_prev - m_new)
            pT = jnp.exp2(s_ref[cur_slot, h] - m_new).astype(BF16)
            acc_ref[h] = alpha * acc_ref[h] + _dot(vc, pT)
            m_ref[h] = m_new

    def body(i, carry):
        half_step(2 * i + 1, 2 * i, 0)
        half_step(jnp.minimum(2 * i + 2, n_kv - 1), 2 * i + 1, 1)
        return carry

    lax.fori_loop(0, n_kv // 2, body, 0, unroll=FLASH_UNROLL)

    outs = []
    for h in range(AT_GROUP):
        acc = acc_ref[h]
        outs.append(acc[0:AT_HD, :] / acc[AT_HD:AT_HD + 1, :])
    o_ref[...] = jnp.transpose(jnp.concatenate(outs, axis=0)).astype(BF16)


def _flash_call(qT, k, vT, n_real):
    n_kv = n_real // KV_TILE
    tail_blk = n_real // 128
    gw = AT_GROUP * AT_HD
    return pl.pallas_call(
        functools.partial(_flash_kernel, n_kv=n_kv, n_tail_valid=N_META),
        out_shape=jax.ShapeDtypeStruct((n_real, AT_W), BF16),
        grid=(AT_KV_HEADS, n_real // Q_TILE),
        in_specs=[
            pl.BlockSpec((gw, Q_TILE), lambda g, i: (g, i)),
            pl.BlockSpec((n_real, AT_KVW), lambda g, i: (0, 0)),
            pl.BlockSpec((None, V_ROWS, n_real), lambda g, i: (g, 0, 0)),
            pl.BlockSpec((128, AT_KVW), lambda g, i: (tail_blk, 0)),
            pl.BlockSpec((None, V_ROWS, 128), lambda g, i: (g, 0, tail_blk)),
        ],
        out_specs=pl.BlockSpec((Q_TILE, gw), lambda g, i: (i, g)),
        scratch_shapes=[
            pltpu.VMEM((AT_GROUP, AT_KVW, Q_TILE), BF16),
            pltpu.VMEM((AT_GROUP, 1, Q_TILE), F32),
            pltpu.VMEM((AT_GROUP, V_ROWS, Q_TILE), F32),
            pltpu.VMEM((2, AT_GROUP, KV_TILE, Q_TILE), F32),
            pltpu.VMEM((2, AT_GROUP, 1, Q_TILE), F32),
            pltpu.VMEM((AT_GROUP, 128, Q_TILE), F32),
            pltpu.VMEM((AT_GROUP, 1, Q_TILE), F32),
        ],
        compiler_params=pltpu.CompilerParams(
            dimension_semantics=("arbitrary", "arbitrary"), vmem_limit_bytes=VMEM_LIMIT),
        name="flash",
    )(qT, k, vT, k, vT)


def _merge_ffn_kernel(h1_ref, of_ref, ob_ref, gs_ref, yb_ref, ga_ref, gb_ref, hgw_ref,
                      wua_ref, wub_ref, wout_ref, nw_ref, wg_ref, wu_ref, wd_ref, o_ref):
    o = of_ref[...] + ob_ref[...]
    normed = []
    for h in range(HG_HEADS):
        oh = o[:, h * HG_K:(h + 1) * HG_K]
        normed.append(oh * lax.rsqrt(jnp.mean(oh * oh, axis=-1, keepdims=True) + EPS))
    ya = (jnp.concatenate(normed, axis=1) * hgw_ref[...] * gs_ref[...]).astype(BF16)
    mixed = ga_ref[...] * _dot(ya, wua_ref[...]) + gb_ref[...] * _dot(yb_ref[...], wub_ref[...])
    h2 = h1_ref[...] + _dot(mixed.astype(BF16), wout_ref[...])
    o_ref[...] = _swiglu_half_step(h2, nw_ref[...], wg_ref, wu_ref, wd_ref)


def _merge_ffn_call(h1, o_f, o_b, gs, yb, ga, gb, hgw, wua, wub, wout, nw, wg, wu, wd, n_real):
    row_spec = lambda w: pl.BlockSpec((ROW_TILE, w), lambda i: (i, 0))
    return pl.pallas_call(
        _merge_ffn_kernel,
        out_shape=jax.ShapeDtypeStruct((n_real, D_MODEL), F32),
        grid=(n_real // ROW_TILE,),
        in_specs=[
            row_spec(D_MODEL), row_spec(HG_W), row_spec(HG_W), row_spec(HG_W), row_spec(AT_W),
            row_spec(D_MODEL), row_spec(D_MODEL),
            _const_spec((1, HG_W)),
            _const_spec((HG_W, D_MODEL)),
            _const_spec((AT_W, D_MODEL)),
            _const_spec((D_MODEL, D_MODEL)),
            _const_spec((1, D_MODEL)),
            _const_spec((D_MODEL, D_FF)),
            _const_spec((D_MODEL, D_FF)),
            _const_spec((D_FF, D_MODEL)),
        ],
        out_specs=row_spec(D_MODEL),
        compiler_params=pltpu.CompilerParams(
            dimension_semantics=("arbitrary",), vmem_limit_bytes=VMEM_LIMIT),
        name="merge_ffn2",
    )(h1, o_f, o_b, gs, yb, ga, gb, hgw, wua, wub, wout, nw, wg, wu, wd)


def _rope_tables(n_real):
    half = AT_HD // 2
    inv = ROPE_THETA ** (-jnp.arange(0, half, 2, dtype=F32) / half)
    sign = jnp.tile(jnp.array([-1.0, 1.0], F32), half)

    def lanes(ang, first_half):
        cos, sin = jnp.cos(ang), jnp.sin(ang)
        one, zero = jnp.ones_like(cos), jnp.zeros_like(sin)
        cos = jnp.concatenate([cos, one] if first_half else [one, cos], axis=-1)
        sin = jnp.concatenate([sin, zero] if first_half else [zero, sin], axis=-1)
        cos = jnp.repeat(cos, 2, axis=-1)
        sin = jnp.repeat(sin, 2, axis=-1) * sign
        reps = 128 // AT_HD
        return jnp.tile(cos, (1, reps)), jnp.tile(sin, (1, reps))

    r = jnp.arange(n_real // GRID_W, dtype=F32)
    c = jnp.arange(GRID_W, dtype=F32)
    return lanes(r[:, None] * inv, True) + lanes(c[:, None] * inv, False)


def _head_mean_matrix(width):
    heads = width // AT_HD
    return jnp.kron(jnp.eye(heads, dtype=F32), jnp.full((AT_HD, AT_HD), 1.0 / AT_HD, F32)).astype(BF16)


def kernel(x, meta_tokens, ffn1_norm, ffn1_w_gate, ffn1_w_up, ffn1_w_down, mix_norm, w_in, hg_lb_fwd, hg_lb_bwd, hg_out_norm, q_norm, k_norm, w_up_a, w_up_b, w_out, ffn2_norm, ffn2_w_gate, ffn2_w_up, ffn2_w_down):
    batch, n_real, _ = x.shape
    assert batch == 1 and n_real % ROW_TILE == 0 and n_real % GRID_W == 0
    rows = n_real + ROW_TILE
    n_valid = n_real + N_META

    tail = jnp.concatenate(
        [meta_tokens.astype(x.dtype), jnp.zeros((ROW_TILE - N_META, D_MODEL), x.dtype)], axis=0)
    row = lambda w: w.reshape(1, -1).astype(F32)
    (w1g, w1u, w1d, win, wua, wub, wout, w2g, w2u, w2d) = _cast_weights(
        [w[0].astype(F32) for w in (ffn1_w_gate, ffn1_w_up, ffn1_w_down, w_in, w_up_a, w_up_b, w_out,
                                    ffn2_w_gate, ffn2_w_up, ffn2_w_down)])

    h1 = _ffn_call(x[0], tail, row(ffn1_norm[0]), w1g, w1u, w1d)

    rope = _rope_tables(n_real)
    hq, hv, kf, lff, kb, lfb, gs, qT, k, vT, ga, gb = _mix_proj_call(
        h1, row(mix_norm[0]), win, hg_lb_fwd.astype(F32), hg_lb_bwd.astype(F32),
        row(jnp.tile(q_norm[0], AT_HEADS)), row(jnp.tile(k_norm[0], AT_KV_HEADS)), rope,
        _head_mean_matrix(AT_W), _head_mean_matrix(AT_KVW), n_valid)

    o_f, o_b = _hgrn_call(hq, hv, kf, lff, kb, lfb, n_real)
    yb = _flash_call(qT, k, vT, n_real)

    out = _merge_ffn_call(
        h1, o_f, o_b, gs, yb, ga, gb, row(hg_out_norm[0]), wua, wub, wout,
        row(ffn2_norm[0]), w2g, w2u, w2d, n_real)
    return out.reshape(batch, n_real, D_MODEL)
```

```python
import functools

import jax
import jax.numpy as jnp
from jax import lax
from jax.experimental import pallas as pl
from jax.experimental.pallas import tpu as pltpu

F32 = jnp.float32
BF16 = jnp.bfloat16

D_MODEL = 1024
D_FF = 2816
N_META = 16
GRID_W = 64
EPS = 1e-6
HG_HEADS = 4
HG_K = 128
HG_W = HG_HEADS * HG_K
AT_HEADS = 8
AT_KV_HEADS = 2
AT_HD = 64
AT_GROUP = AT_HEADS // AT_KV_HEADS
AT_W = AT_HEADS * AT_HD
AT_KVW = AT_KV_HEADS * AT_HD
ROPE_THETA = 10000.0
IN_SIZES = (HG_W, HG_W, HG_W, HG_W, HG_W, AT_W, AT_KVW, AT_KVW, D_MODEL, D_MODEL)
IN_OFFS = tuple(sum(IN_SIZES[:i]) for i in range(len(IN_SIZES)))
D_IN = sum(IN_SIZES)

ROW_TILE = 512
FF_CHUNK = 256
CHUNK = 128
HGRN_STEP_CHUNKS = 8
Q_TILE = 256
KV_TILE = 256
V_ROWS = AT_HD + 16
LOG2E = 1.4426950408889634
Q_SCALE = AT_HD ** -0.5 * LOG2E
FLASH_UNROLL = 16
FLASH_LEAD = AT_GROUP + 1
VMEM_LIMIT = 56 * 1024 * 1024


def _dot(a, b):
    return jnp.dot(a, b, preferred_element_type=F32)


def _dot_nt(a, b):
    return lax.dot_general(a, b, (((1,), (1,)), ((), ())), preferred_element_type=F32)


def _dot_tn(a, b):
    return lax.dot_general(a, b, (((0,), (0,)), ((), ())), preferred_element_type=F32)


def _rms(x, w):
    ms = jnp.mean(x * x, axis=-1, keepdims=True)
    return x * lax.rsqrt(ms + EPS) * w


def _sigmoid(x):
    return 0.5 * jnp.tanh(0.5 * x) + 0.5


def _silu(x):
    h = 0.5 * x
    return h * jnp.tanh(h) + h


def _swiglu_half_step(h, norm_w, wg_ref, wu_ref, wd_ref):
    hn = _rms(h, norm_w).astype(BF16)
    acc = None
    pending = None
    for c in range(D_FF // FF_CHUNK + 1):
        if c < D_FF // FF_CHUNK:
            sl = slice(c * FF_CHUNK, (c + 1) * FF_CHUNK)
            g = _dot(hn, wg_ref[:, sl])
            u = _dot(hn, wu_ref[:, sl])
        if pending is not None:
            d = _dot(pending[0], wd_ref[pending[1], :])
            acc = d if acc is None else acc + d
        if c < D_FF // FF_CHUNK:
            pending = ((_silu(g) * u).astype(BF16), sl)
    return h + 0.5 * acc


CAST_STEPS = 16


def _cast_kernel(*refs):
    n = len(refs) // 2
    for src, dst in zip(refs[:n], refs[n:]):
        dst[...] = src[...].astype(BF16)


def _cast_weights(ws):
    specs = [pl.BlockSpec((w.shape[0] // CAST_STEPS, w.shape[1]), lambda i: (i, 0)) for w in ws]
    return pl.pallas_call(
        _cast_kernel,
        out_shape=[jax.ShapeDtypeStruct(w.shape, BF16) for w in ws],
        grid=(CAST_STEPS,),
        in_specs=specs,
        out_specs=specs,
        compiler_params=pltpu.CompilerParams(
            dimension_semantics=("arbitrary",), vmem_limit_bytes=VMEM_LIMIT),
        name="cast_weights",
    )(*ws)


def _ffn_kernel(x_ref, tail_ref, nw_ref, wg_ref, wu_ref, wd_ref, o_ref, *, n_x_tiles):
    h = jnp.where(pl.program_id(0) < n_x_tiles, x_ref[...], tail_ref[...])
    o_ref[...] = _swiglu_half_step(h, nw_ref[...], wg_ref, wu_ref, wd_ref)


def _const_spec(shape):
    nd = len(shape)
    return pl.BlockSpec(shape, lambda *_: (0,) * nd, pipeline_mode=pl.Buffered(1))


def _ffn_call(x, tail, nw, wg, wu, wd):
    n_x_tiles = x.shape[0] // ROW_TILE
    rows = x.shape[0] + tail.shape[0]
    return pl.pallas_call(
        functools.partial(_ffn_kernel, n_x_tiles=n_x_tiles),
        out_shape=jax.ShapeDtypeStruct((rows, D_MODEL), F32),
        grid=(n_x_tiles + 1,),
        in_specs=[
            pl.BlockSpec((ROW_TILE, D_MODEL), lambda i: (jnp.minimum(i, n_x_tiles - 1), 0)),
            _const_spec((ROW_TILE, D_MODEL)),
            _const_spec((1, D_MODEL)),
            _const_spec((D_MODEL, D_FF)),
            _const_spec((D_MODEL, D_FF)),
            _const_spec((D_FF, D_MODEL)),
        ],
        out_specs=pl.BlockSpec((ROW_TILE, D_MODEL), lambda i: (i, 0)),
        compiler_params=pltpu.CompilerParams(
            dimension_semantics=("arbitrary",), vmem_limit_bytes=VMEM_LIMIT),
        name="ffn1",
    )(x, tail, nw, wg, wu, wd)


def _lower_bound_gate(z, lbp, valid):
    m = jnp.max(lbp, axis=0, keepdims=True)
    e = jnp.exp(lbp - m)
    lb = e[0:1, :] / jnp.sum(e, axis=0, keepdims=True)
    kk = (1.0 - lb) * _sigmoid(-z)
    kk = jnp.where(valid, kk, 0.0)
    return kk, jnp.log2(1.0 - kk)


def _head_rms_rope(z, ms, w, cos, sin):
    zn = z * lax.rsqrt(ms + EPS) * w
    width = z.shape[1]
    lane = lax.broadcasted_iota(jnp.int32, z.shape, 1)
    partner = jnp.where((lane & 1) == 0, pltpu.roll(zn, width - 1, 1), pltpu.roll(zn, 1, 1))
    return zn * cos + partner * sin


def _rope_tile(by_row_ref, by_col_ref, is_tail, tail_value):
    groups = ROW_TILE // GRID_W
    rp = by_row_ref[...]
    by_row = jnp.concatenate([jnp.broadcast_to(rp[j:j + 1, :], (GRID_W, 128)) for j in range(groups)], axis=0)
    by_col = jnp.concatenate([by_col_ref[...]] * groups, axis=0)
    lane = lax.broadcasted_iota(jnp.int32, (1, 128), 1)
    t = jnp.where((lane & (AT_HD - 1)) < AT_HD // 2, by_row, by_col)
    return jnp.where(is_tail, tail_value, t)


def _mix_proj_kernel(h_ref, nw_ref, win_ref, lbf_ref, lbb_ref, qnw_ref, knw_ref, rcos_ref, rsin_ref,
                     ccos_ref, csin_ref, bdq_ref, bdk_ref,
                     hq_ref, hv_ref, kf_ref, lff_ref, kb_ref, lfb_ref, gs_ref, qT_ref, k_ref, vT_ref,
                     ga_ref, gb_ref, *, n_valid, n_x_tiles):
    i = pl.program_id(0)
    un = _rms(h_ref[...], nw_ref[...]).astype(BF16)

    def proj(piece):
        off, size = IN_OFFS[piece], IN_SIZES[piece]
        return _dot(un, win_ref[:, off:off + size])

    row = i * ROW_TILE + lax.broadcasted_iota(jnp.int32, (ROW_TILE, 1), 0)
    valid = row < n_valid

    zq = proj(5)
    zk = proj(6)
    cos = _rope_tile(rcos_ref, ccos_ref, i >= n_x_tiles, 1.0)
    sin = _rope_tile(rsin_ref, csin_ref, i >= n_x_tiles, 0.0)
    cos_q = jnp.concatenate([cos] * (AT_W // 128), axis=1)
    sin_q = jnp.concatenate([sin] * (AT_W // 128), axis=1)
    ms_q = _dot((zq * zq).astype(BF16), bdq_ref[...])
    ms_k = _dot((zk * zk).astype(BF16), bdk_ref[...])
    q = _head_rms_rope(zq, ms_q, qnw_ref[...], cos_q, sin_q) * Q_SCALE
    qT_ref[...] = jnp.transpose(q).astype(BF16)
    k_ref[...] = _head_rms_rope(zk, ms_k, knw_ref[...], cos, sin).astype(BF16)

    hq_ref[...] = _silu(proj(0))
    hv_ref[...] = proj(1).astype(BF16)
    kf, lff = _lower_bound_gate(proj(2), lbf_ref[...], valid)
    kf_ref[...] = kf
    lff_ref[...] = lff
    kb, lfb = _lower_bound_gate(proj(3), lbb_ref[...], valid)
    kb_ref[...] = kb
    lfb_ref[...] = lfb
    gs_ref[...] = _silu(proj(4))

    vT = jnp.transpose(proj(7)).astype(BF16)
    ones = jnp.ones((V_ROWS - AT_HD, ROW_TILE), BF16)
    for g in range(AT_KV_HEADS):
        vT_ref[g, 0:AT_HD, :] = vT[g * AT_HD:(g + 1) * AT_HD, :]
        vT_ref[g, AT_HD:V_ROWS, :] = ones

    ga_ref[...] = _sigmoid(proj(8))
    gb_ref[...] = _sigmoid(proj(9))


def _mix_proj_call(h1, nw, win, lbf, lbb, qnw, knw, rope, bdq, bdk, n_valid):
    rows = h1.shape[0]
    n_x_tiles = rows // ROW_TILE - 1
    groups = ROW_TILE // GRID_W
    rope_row_spec = pl.BlockSpec((groups, 128), lambda i: (jnp.minimum(i, n_x_tiles - 1), 0))
    row_spec = lambda w: pl.BlockSpec((ROW_TILE, w), lambda i: (i, 0))
    out_shape = [
        jax.ShapeDtypeStruct((rows, HG_W), F32),
        jax.ShapeDtypeStruct((rows, HG_W), BF16),
        jax.ShapeDtypeStruct((rows, HG_W), F32),
        jax.ShapeDtypeStruct((rows, HG_W), F32),
        jax.ShapeDtypeStruct((rows, HG_W), F32),
        jax.ShapeDtypeStruct((rows, HG_W), F32),
        jax.ShapeDtypeStruct((rows, HG_W), F32),
        jax.ShapeDtypeStruct((AT_W, rows), BF16),
        jax.ShapeDtypeStruct((rows, AT_KVW), BF16),
        jax.ShapeDtypeStruct((AT_KV_HEADS, V_ROWS, rows), BF16),
        jax.ShapeDtypeStruct((rows, D_MODEL), F32),
        jax.ShapeDtypeStruct((rows, D_MODEL), F32),
    ]
    out_specs = [
        row_spec(HG_W), row_spec(HG_W), row_spec(HG_W), row_spec(HG_W), row_spec(HG_W), row_spec(HG_W),
        row_spec(HG_W),
        pl.BlockSpec((AT_W, ROW_TILE), lambda i: (0, i)),
        row_spec(AT_KVW),
        pl.BlockSpec((AT_KV_HEADS, V_ROWS, ROW_TILE), lambda i: (0, 0, i)),
        row_spec(D_MODEL), row_spec(D_MODEL),
    ]
    in_specs = [
        row_spec(D_MODEL),
        _const_spec((1, D_MODEL)),
        _const_spec((D_MODEL, D_IN)),
        _const_spec(lbf.shape),
        _const_spec(lbb.shape),
        _const_spec((1, AT_W)),
        _const_spec((1, AT_KVW)),
        rope_row_spec,
        rope_row_spec,
        _const_spec((GRID_W, 128)),
        _const_spec((GRID_W, 128)),
        _const_spec((AT_W, AT_W)),
        _const_spec((AT_KVW, AT_KVW)),
    ]
    return pl.pallas_call(
        functools.partial(_mix_proj_kernel, n_valid=n_valid, n_x_tiles=n_x_tiles),
        out_shape=out_shape,
        grid=(rows // ROW_TILE,),
        in_specs=in_specs,
        out_specs=out_specs,
        compiler_params=pltpu.CompilerParams(
            dimension_semantics=("arbitrary",), vmem_limit_bytes=VMEM_LIMIT),
        name="mix_proj",
    )(h1, nw, win, lbf, lbb, qnw, knw, *rope, bdq, bdk)


def _split2(x):
    hi = x.astype(BF16)
    lo = (x - hi.astype(F32)).astype(BF16)
    return hi, lo


def _level_operand(q, k, b, lf, blk, reverse):
    c = CHUNK
    if blk >= 8:
        pieces = []
        for g in range(c // (2 * blk)):
            r0 = g * 2 * blk
            lo, hi = slice(r0, r0 + blk), slice(r0 + blk, r0 + 2 * blk)
            if reverse:
                ref = b[r0 + blk:r0 + blk + 1, :]
                pieces.append(q[lo] * jnp.exp2(b[lo] - ref))
                pieces.append(k[hi] * jnp.exp2(ref - b[hi]))
            else:
                ref = b[r0 + blk - 1:r0 + blk, :]
                pieces.append(k[lo] * jnp.exp2(ref - b[lo]))
                pieces.append(q[hi] * jnp.exp2(b[hi] - ref))
        return jnp.concatenate(pieces, axis=0).astype(BF16)

    q3, k3, b3, lf3 = (x.reshape(c // 8, 8, HG_W) for x in (q, k, b, lf))
    sub = lax.broadcasted_iota(jnp.int32, (1, 8, 1), 1)
    later = ((sub >> (blk.bit_length() - 1)) & 1) == 1
    q_role = jnp.logical_not(later) if reverse else later
    if blk == 4:
        ref = b3[:, 4:5, :] if reverse else b3[:, 3:4, :]
        gl = -jnp.abs(b3 - ref)
    elif blk == 2:
        up = pltpu.roll(lf3, 7, 1)
        dn = pltpu.roll(lf3, 1, 1)
        m4 = sub & 3
        if reverse:
            gl = jnp.where(m4 == 0, lf3 + up, jnp.where(m4 == 1, lf3, jnp.where(m4 == 2, 0.0, dn)))
        else:
            gl = jnp.where(m4 == 0, up, jnp.where(m4 == 1, 0.0, jnp.where(m4 == 2, lf3, dn + lf3)))
    else:
        odd = (sub & 1) == 1
        gl = jnp.where(odd, 0.0, lf3) if reverse else jnp.where(odd, lf3, 0.0)
    y3 = jnp.where(q_role, q3, k3) * jnp.exp2(gl)
    return y3.reshape(c, HG_W).astype(BF16)


def _hgrn_direction(q_ref, v_ref, k_ref, lf_ref, o_ref, st_ref, reverse):
    c = CHUNK
    q = q_ref[...]
    k = k_ref[...]
    v = v_ref[...]
    lf = lf_ref[...]

    r_i = lax.broadcasted_iota(jnp.int32, (c, c), 0)
    c_i = lax.broadcasted_iota(jnp.int32, (c, c), 1)
    tri = ((c_i >= r_i) if reverse else (c_i <= r_i)).astype(BF16)
    hi, lo = _split2(lf)
    b = _dot(jnp.concatenate([tri, tri], axis=1), jnp.concatenate([hi, lo], axis=0))

    levels = (64, 32, 16, 8, 4, 2, 1)
    ys = [_level_operand(q, k, b, lf, blk, reverse) for blk in levels]

    lane = lax.broadcasted_iota(jnp.int32, (1, c), 1)
    small_valid = {}
    for blk in (4, 2, 1):
        sh = blk.bit_length() - 1
        same = (r_i >> (sh + 1)) == (c_i >> (sh + 1))
        t_later = ((r_i >> sh) & 1) == 1
        s_later = ((c_i >> sh) & 1) == 1
        small_valid[blk] = (same & jnp.logical_not(t_later) & s_later) if reverse else (
            same & t_later & jnp.logical_not(s_later))

    def tile_mask(blk, tile):
        if blk < 8:
            return small_valid[blk][8 * tile:8 * tile + 8, :]
        block = (8 * tile) // blk
        if (block % 2 == 1) == reverse:
            return None
        col0 = (block + 1) * blk if reverse else (block - 1) * blk
        return (lane >= col0) & (lane < col0 + blk)

    b_last = b[0:1, :] if reverse else b[c - 1:c, :]
    q_inter = (q * jnp.exp2(b)).astype(BF16)
    k_state = (k * jnp.exp2(b_last - b)).astype(BF16)
    e_last = jnp.exp2(b_last)
    qk = q * k
    eye = r_i == c_i

    def issue(h):
        hs = slice(h * HG_K, (h + 1) * HG_K)
        products = [_dot_nt(y[:, hs], y[:, hs]) for y in ys]
        st = st_ref[h]
        inter = _dot_nt(q_inter[:, hs], st.astype(BF16))
        st_ref[h] = st * e_last[:, hs] + _dot_tn(v[:, hs], k_state[:, hs])
        return products, inter

    def finish(h, issued):
        products, inter = issued
        hs = slice(h * HG_K, (h + 1) * HG_K)
        diag = jnp.where(eye, jnp.sum(qk[:, hs], axis=1, keepdims=True), 0.0)
        tiles = []
        for t in range(c // 8):
            rows = slice(8 * t, 8 * t + 8)
            a_t = diag[rows]
            for blk, p in zip(levels, products):
                mask = tile_mask(blk, t)
                if mask is not None:
                    a_t = jnp.where(mask, p[rows], a_t)
            tiles.append(a_t)
        a = jnp.concatenate(tiles, axis=0)
        o_ref[:, hs] = _dot(a.astype(BF16), v[:, hs]) + inter

    return issue, finish


def _hgrn_meta_state(v_ref, k_ref, lf_ref, st_ref):
    c = CHUNK
    k = k_ref[...]
    v = v_ref[...]
    r_i = lax.broadcasted_iota(jnp.int32, (c, c), 0)
    c_i = lax.broadcasted_iota(jnp.int32, (c, c), 1)
    tri = (c_i <= r_i).astype(BF16)
    hi, lo = _split2(lf_ref[...])
    b = _dot(jnp.concatenate([tri, tri], axis=1), jnp.concatenate([hi, lo], axis=0))
    k_state = (k * jnp.exp2(b[c - 1:c, :] - b)).astype(BF16)
    for h in range(HG_HEADS):
        hs = slice(h * HG_K, (h + 1) * HG_K)
        st_ref[h] = _dot_tn(v[:, hs], k_state[:, hs])


def _hgrn_kernel(qf_ref, vf_ref, kf_ref, lff_ref, qb_ref, vb_ref, kb_ref, lfb_ref, vm_ref, kfm_ref, lfm_ref,
                 of_ref, ob_ref, sf_ref, sb_ref):
    @pl.when(pl.program_id(0) == 0)
    def _():
        sb_ref[...] = jnp.zeros_like(sb_ref)
        _hgrn_meta_state(vm_ref, kfm_ref, lfm_ref, sf_ref)

    def sub(ref, j):
        return ref.at[pl.ds(j * CHUNK, CHUNK)]

    units = []
    for j in range(HGRN_STEP_CHUNKS):
        jb = HGRN_STEP_CHUNKS - 1 - j
        fwd = (sub(qf_ref, j), sub(vf_ref, j), sub(kf_ref, j), sub(lff_ref, j), sub(of_ref, j), sf_ref, False)
        bwd = (sub(qb_ref, jb), sub(vb_ref, jb), sub(kb_ref, jb), sub(lfb_ref, jb), sub(ob_ref, jb), sb_ref, True)
        for args in (fwd, bwd):
            issue, finish = _hgrn_direction(*args)
            units += [(issue, finish, h) for h in range(HG_HEADS)]

    pending = None
    for issue, finish, h in units:
        issued = issue(h)
        if pending is not None:
            pending[0](pending[1], pending[2])
        pending = (finish, h, issued)
    pending[0](pending[1], pending[2])


def _hgrn_call(hq, hv, kf, lff, kb, lfb, n_real):
    nb = n_real // CHUNK
    steps = nb // HGRN_STEP_CHUNKS
    fwd = lambda s: (s, 0)
    bwd = lambda s: (steps - 1 - s, 0)
    spec = lambda m: pl.BlockSpec((HGRN_STEP_CHUNKS * CHUNK, HG_W), m)
    meta = pl.BlockSpec((CHUNK, HG_W), lambda s: (nb, 0))
    out = jax.ShapeDtypeStruct((n_real, HG_W), F32)
    return pl.pallas_call(
        _hgrn_kernel,
        out_shape=[out, out],
        grid=(steps,),
        in_specs=[spec(fwd), spec(fwd), spec(fwd), spec(fwd), spec(bwd), spec(bwd), spec(bwd), spec(bwd),
                  meta, meta, meta],
        out_specs=[spec(fwd), spec(bwd)],
        scratch_shapes=[pltpu.VMEM((HG_HEADS, HG_K, HG_K), F32), pltpu.VMEM((HG_HEADS, HG_K, HG_K), F32)],
        compiler_params=pltpu.CompilerParams(
            dimension_semantics=("arbitrary",), vmem_limit_bytes=VMEM_LIMIT),
        name="hgrn",
    )(hq, hv, kf, lff, hq, hv, kb, lfb, hv, kf, lff)


def _flash_kernel(qT_ref, k_ref, vT_ref, kt_ref, vTt_ref, o_ref, qp_ref, m_ref, acc_ref, s_ref, mc_ref,
                  st_ref, mct_ref, *,
                  n_kv, n_tail_valid):
    g = pl.program_id(0)
    tq = qT_ref.shape[1]

    half = lax.broadcasted_iota(jnp.int32, (AT_KVW, tq), 0) >> 6
    for h in range(AT_GROUP):
        qh = qT_ref[h * AT_HD:(h + 1) * AT_HD, :].astype(F32)
        q2 = jnp.concatenate([qh, qh], axis=0)
        qp_ref[h] = jnp.where(half == g, q2, 0.0).astype(BF16)

    m_ref[...] = jnp.full(m_ref.shape, -jnp.inf, F32)
    acc_ref[...] = jnp.zeros(acc_ref.shape, F32)

    def store_scores(sT, s_dst, mc_dst, h):
        s_dst[h] = sT
        mc_dst[h] = jnp.max(sT, axis=0, keepdims=True)

    def unit_scores(chunk, h, slot):
        kc = k_ref[pl.ds(pl.multiple_of(chunk * KV_TILE, KV_TILE), KV_TILE), :]
        store_scores(_dot(kc, qp_ref[h]), s_ref.at[slot], mc_ref.at[slot], h)

    def unit_consume(vc, s_src, mc_src, h):
        m_prev = m_ref[h]
        m_new = jnp.maximum(m_prev, mc_src[h])
        alpha = jnp.exp2(m_prev - m_new)
        pT = jnp.exp2(s_src[h] - m_new).astype(BF16)
        acc_ref[h] = alpha * acc_ref[h] + _dot(vc, pT)
        m_ref[h] = m_new

    krow = lax.broadcasted_iota(jnp.int32, (128, 1), 0)
    for h in range(AT_GROUP):
        sT = jnp.where(krow < n_tail_valid, _dot(kt_ref[...], qp_ref[h]), -jnp.inf)
        store_scores(sT, st_ref, mct_ref, h)
    for u in range(FLASH_LEAD):
        unit_scores(u // AT_GROUP, u % AT_GROUP, (u // AT_GROUP) % 2)
    for h in range(AT_GROUP):
        unit_consume(vTt_ref[...], st_ref, mct_ref, h)

    def body(i, carry):
        for j in range(2 * AT_GROUP):
            uq = j + FLASH_LEAD
            cq = jnp.minimum(2 * i + uq // AT_GROUP, n_kv - 1)
            unit_scores(cq, uq % AT_GROUP, (uq // AT_GROUP) % 2)
            cp = 2 * i + j // AT_GROUP
            vc = vT_ref[:, pl.ds(pl.multiple_of(cp * KV_TILE, KV_TILE), KV_TILE)]
            unit_consume(vc, s_ref.at[(j // AT_GROUP) % 2], mc_ref.at[(j // AT_GROUP) % 2], j % AT_GROUP)
        return carry

    lax.fori_loop(0, n_kv // 2, body, 0, unroll=FLASH_UNROLL)

    outs = []
    for h in range(AT_GROUP):
        acc = acc_ref[h]
        outs.append(acc[0:AT_HD, :] / acc[AT_HD:AT_HD + 1, :])
    o_ref[...] = jnp.transpose(jnp.concatenate(outs, axis=0)).astype(BF16)


def _flash_call(qT, k, vT, n_real):
    n_kv = n_real // KV_TILE
    tail_blk = n_real // 128
    gw = AT_GROUP * AT_HD
    return pl.pallas_call(
        functools.partial(_flash_kernel, n_kv=n_kv, n_tail_valid=N_META),
        out_shape=jax.ShapeDtypeStruct((n_real, AT_W), BF16),
        grid=(AT_KV_HEADS, n_real // Q_TILE),
        in_specs=[
            pl.BlockSpec((gw, Q_TILE), lambda g, i: (g, i)),
            pl.BlockSpec((n_real, AT_KVW), lambda g, i: (0, 0)),
            pl.BlockSpec((None, V_ROWS, n_real), lambda g, i: (g, 0, 0)),
            pl.BlockSpec((128, AT_KVW), lambda g, i: (tail_blk, 0)),
            pl.BlockSpec((None, V_ROWS, 128), lambda g, i: (g, 0, tail_blk)),
        ],
        out_specs=pl.BlockSpec((Q_TILE, gw), lambda g, i: (i, g)),
        scratch_shapes=[
            pltpu.VMEM((AT_GROUP, AT_KVW, Q_TILE), BF16),
            pltpu.VMEM((AT_GROUP, 1, Q_TILE), F32),
            pltpu.VMEM((AT_GROUP, V_ROWS, Q_TILE), F32),
            pltpu.VMEM((2, AT_GROUP, KV_TILE, Q_TILE), F32),
            pltpu.VMEM((2, AT_GROUP, 1, Q_TILE), F32),
            pltpu.VMEM((AT_GROUP, 128, Q_TILE), F32),
            pltpu.VMEM((AT_GROUP, 1, Q_TILE), F32),
        ],
        compiler_params=pltpu.CompilerParams(
            dimension_semantics=("arbitrary", "arbitrary"), vmem_limit_bytes=VMEM_LIMIT),
        name="flash",
    )(qT, k, vT, k, vT)


def _merge_ffn_kernel(h1_ref, of_ref, ob_ref, gs_ref, yb_ref, ga_ref, gb_ref, hgw_ref,
                      wua_ref, wub_ref, wout_ref, nw_ref, wg_ref, wu_ref, wd_ref, o_ref):
    o = of_ref[...] + ob_ref[...]
    normed = []
    for h in range(HG_HEADS):
        oh = o[:, h * HG_K:(h + 1) * HG_K]
        normed.append(oh * lax.rsqrt(jnp.mean(oh * oh, axis=-1, keepdims=True) + EPS))
    ya = (jnp.concatenate(normed, axis=1) * hgw_ref[...] * gs_ref[...]).astype(BF16)
    mixed = ga_ref[...] * _dot(ya, wua_ref[...]) + gb_ref[...] * _dot(yb_ref[...], wub_ref[...])
    h2 = h1_ref[...] + _dot(mixed.astype(BF16), wout_ref[...])
    o_ref[...] = _swiglu_half_step(h2, nw_ref[...], wg_ref, wu_ref, wd_ref)


def _merge_ffn_call(h1, o_f, o_b, gs, yb, ga, gb, hgw, wua, wub, wout, nw, wg, wu, wd, n_real):
    row_spec = lambda w: pl.BlockSpec((ROW_TILE, w), lambda i: (i, 0))
    return pl.pallas_call(
        _merge_ffn_kernel,
        out_shape=jax.ShapeDtypeStruct((n_real, D_MODEL), F32),
        grid=(n_real // ROW_TILE,),
        in_specs=[
            row_spec(D_MODEL), row_spec(HG_W), row_spec(HG_W), row_spec(HG_W), row_spec(AT_W),
            row_spec(D_MODEL), row_spec(D_MODEL),
            _const_spec((1, HG_W)),
            _const_spec((HG_W, D_MODEL)),
            _const_spec((AT_W, D_MODEL)),
            _const_spec((D_MODEL, D_MODEL)),
            _const_spec((1, D_MODEL)),
            _const_spec((D_MODEL, D_FF)),
            _const_spec((D_MODEL, D_FF)),
            _const_spec((D_FF, D_MODEL)),
        ],
        out_specs=row_spec(D_MODEL),
        compiler_params=pltpu.CompilerParams(
            dimension_semantics=("arbitrary",), vmem_limit_bytes=VMEM_LIMIT),
        name="merge_ffn2",
    )(h1, o_f, o_b, gs, yb, ga, gb, hgw, wua, wub, wout, nw, wg, wu, wd)


def _rope_tables(n_real):
    half = AT_HD // 2
    inv = ROPE_THETA ** (-jnp.arange(0, half, 2, dtype=F32) / half)
    sign = jnp.tile(jnp.array([-1.0, 1.0], F32), half)

    def lanes(ang, first_half):
        cos, sin = jnp.cos(ang), jnp.sin(ang)
        one, zero = jnp.ones_like(cos), jnp.zeros_like(sin)
        cos = jnp.concatenate([cos, one] if first_half else [one, cos], axis=-1)
        sin = jnp.concatenate([sin, zero] if first_half else [zero, sin], axis=-1)
        cos = jnp.repeat(cos, 2, axis=-1)
        sin = jnp.repeat(sin, 2, axis=-1) * sign
        reps = 128 // AT_HD
        return jnp.tile(cos, (1, reps)), jnp.tile(sin, (1, reps))

    r = jnp.arange(n_real // GRID_W, dtype=F32)
    c = jnp.arange(GRID_W, dtype=F32)
    return lanes(r[:, None] * inv, True) + lanes(c[:, None] * inv, False)


def _head_mean_matrix(width):
    heads = width // AT_HD
    return jnp.kron(jnp.eye(heads, dtype=F32), jnp.full((AT_HD, AT_HD), 1.0 / AT_HD, F32)).astype(BF16)


def kernel(x, meta_tokens, ffn1_norm, ffn1_w_gate, ffn1_w_up, ffn1_w_down, mix_norm, w_in, hg_lb_fwd, hg_lb_bwd, hg_out_norm, q_norm, k_norm, w_up_a, w_up_b, w_out, ffn2_norm, ffn2_w_gate, ffn2_w_up, ffn2_w_down):
    batch, n_real, _ = x.shape
    assert batch == 1 and n_real % ROW_TILE == 0 and n_real % GRID_W == 0
    rows = n_real + ROW_TILE
    n_valid = n_real + N_META

    tail = jnp.concatenate(
        [meta_tokens.astype(x.dtype), jnp.zeros((ROW_TILE - N_META, D_MODEL), x.dtype)], axis=0)
    row = lambda w: w.reshape(1, -1).astype(F32)
    (w1g, w1u, w1d, win, wua, wub, wout, w2g, w2u, w2d) = _cast_weights(
        [w[0].astype(F32) for w in (ffn1_w_gate, ffn1_w_up, ffn1_w_down, w_in, w_up_a, w_up_b, w_out,
                                    ffn2_w_gate, ffn2_w_up, ffn2_w_down)])

    h1 = _ffn_call(x[0], tail, row(ffn1_norm[0]), w1g, w1u, w1d)

    rope = _rope_tables(n_real)
    hq, hv, kf, lff, kb, lfb, gs, qT, k, vT, ga, gb = _mix_proj_call(
        h1, row(mix_norm[0]), win, hg_lb_fwd.astype(F32), hg_lb_bwd.astype(F32),
        row(jnp.tile(q_norm[0], AT_HEADS)), row(jnp.tile(k_norm[0], AT_KV_HEADS)), rope,
        _head_mean_matrix(AT_W), _head_mean_matrix(AT_KVW), n_valid)

    o_f, o_b = _hgrn_call(hq, hv, kf, lff, kb, lfb, n_real)
    yb = _flash_call(qT, k, vT, n_real)

    out = _merge_ffn_call(
        h1, o_f, o_b, gs, yb, ga, gb, row(hg_out_norm[0]), wua, wub, wout,
        row(ffn2_norm[0]), w2g, w2u, w2d, n_real)
    return out.reshape(batch, n_real, D_MODEL)
```

```python
import functools

import jax
import jax.numpy as jnp
from jax import lax
from jax.experimental import pallas as pl
from jax.experimental.pallas import tpu as pltpu

F32 = jnp.float32
BF16 = jnp.bfloat16

D_MODEL = 1024
D_FF = 2816
N_META = 16
GRID_W = 64
EPS = 1e-6
HG_HEADS = 4
HG_K = 128
HG_W = HG_HEADS * HG_K
AT_HEADS = 8
AT_KV_HEADS = 2
AT_HD = 64
AT_GROUP = AT_HEADS // AT_KV_HEADS
AT_W = AT_HEADS * AT_HD
AT_KVW = AT_KV_HEADS * AT_HD
ROPE_THETA = 10000.0
IN_SIZES = (HG_W, HG_W, HG_W, HG_W, HG_W, AT_W, AT_KVW, AT_KVW, D_MODEL, D_MODEL)
IN_OFFS = tuple(sum(IN_SIZES[:i]) for i in range(len(IN_SIZES)))
D_IN = sum(IN_SIZES)

ROW_TILE = 512
FF_CHUNK = 256
CHUNK = 128
HGRN_STEP_CHUNKS = 8
Q_TILE = 256
KV_TILE = 256
V_ROWS = AT_HD + 16
LOG2E = 1.4426950408889634
Q_SCALE = AT_HD ** -0.5 * LOG2E
FLASH_UNROLL = 16
FLASH_LEAD = AT_GROUP + 1
VMEM_LIMIT = 56 * 1024 * 1024


def _dot(a, b):
    return jnp.dot(a, b, preferred_element_type=F32)


def _dot_nt(a, b):
    return lax.dot_general(a, b, (((1,), (1,)), ((), ())), preferred_element_type=F32)


def _dot_tn(a, b):
    return lax.dot_general(a, b, (((0,), (0,)), ((), ())), preferred_element_type=F32)


def _rms(x, w):
    ms = jnp.mean(x * x, axis=-1, keepdims=True)
    return x * lax.rsqrt(ms + EPS) * w


def _sigmoid(x):
    return 0.5 * jnp.tanh(0.5 * x) + 0.5


def _silu(x):
    h = 0.5 * x
    return h * jnp.tanh(h) + h


def _swiglu_half_step(h, norm_w, wg_ref, wu_ref, wd_ref):
    hn = _rms(h, norm_w).astype(BF16)
    acc = None
    pending = None
    for c in range(D_FF // FF_CHUNK + 1):
        if c < D_FF // FF_CHUNK:
            sl = slice(c * FF_CHUNK, (c + 1) * FF_CHUNK)
            g = _dot(hn, wg_ref[:, sl])
            u = _dot(hn, wu_ref[:, sl])
        if pending is not None:
            d = _dot(pending[0], wd_ref[pending[1], :])
            acc = d if acc is None else acc + d
        if c < D_FF // FF_CHUNK:
            pending = ((_silu(g) * u).astype(BF16), sl)
    return h + 0.5 * acc


CAST_STEPS = 16


def _cast_kernel(*refs):
    n = len(refs) // 2
    for src, dst in zip(refs[:n], refs[n:]):
        dst[...] = src[...].astype(BF16)


def _cast_weights(ws):
    specs = [pl.BlockSpec((w.shape[0] // CAST_STEPS, w.shape[1]), lambda i: (i, 0)) for w in ws]
    return pl.pallas_call(
        _cast_kernel,
        out_shape=[jax.ShapeDtypeStruct(w.shape, BF16) for w in ws],
        grid=(CAST_STEPS,),
        in_specs=specs,
        out_specs=specs,
        compiler_params=pltpu.CompilerParams(
            dimension_semantics=("arbitrary",), vmem_limit_bytes=VMEM_LIMIT),
        name="cast_weights",
    )(*ws)


def _ffn_kernel(x_ref, tail_ref, nw_ref, wg_ref, wu_ref, wd_ref, o_ref, *, n_x_tiles):
    h = jnp.where(pl.program_id(0) < n_x_tiles, x_ref[...], tail_ref[...])
    o_ref[...] = _swiglu_half_step(h, nw_ref[...], wg_ref, wu_ref, wd_ref)


def _const_spec(shape):
    nd = len(shape)
    return pl.BlockSpec(shape, lambda *_: (0,) * nd, pipeline_mode=pl.Buffered(1))


def _ffn_call(x, tail, nw, wg, wu, wd):
    n_x_tiles = x.shape[0] // ROW_TILE
    rows = x.shape[0] + tail.shape[0]
    return pl.pallas_call(
        functools.partial(_ffn_kernel, n_x_tiles=n_x_tiles),
        out_shape=jax.ShapeDtypeStruct((rows, D_MODEL), F32),
        grid=(n_x_tiles + 1,),
        in_specs=[
            pl.BlockSpec((ROW_TILE, D_MODEL), lambda i: (jnp.minimum(i, n_x_tiles - 1), 0)),
            _const_spec((ROW_TILE, D_MODEL)),
            _const_spec((1, D_MODEL)),
            _const_spec((D_MODEL, D_FF)),
            _const_spec((D_MODEL, D_FF)),
            _const_spec((D_FF, D_MODEL)),
        ],
        out_specs=pl.BlockSpec((ROW_TILE, D_MODEL), lambda i: (i, 0)),
        compiler_params=pltpu.CompilerParams(
            dimension_semantics=("arbitrary",), vmem_limit_bytes=VMEM_LIMIT),
        name="ffn1",
    )(x, tail, nw, wg, wu, wd)


def _lower_bound_gate(z, lbp, valid):
    m = jnp.max(lbp, axis=0, keepdims=True)
    e = jnp.exp(lbp - m)
    lb = e[0:1, :] / jnp.sum(e, axis=0, keepdims=True)
    kk = (1.0 - lb) * _sigmoid(-z)
    kk = jnp.where(valid, kk, 0.0)
    return kk, jnp.log2(1.0 - kk)


def _head_rms_rope(z, ms, w, cos, sin):
    zn = z * lax.rsqrt(ms + EPS) * w
    width = z.shape[1]
    lane = lax.broadcasted_iota(jnp.int32, z.shape, 1)
    partner = jnp.where((lane & 1) == 0, pltpu.roll(zn, width - 1, 1), pltpu.roll(zn, 1, 1))
    return zn * cos + partner * sin


def _rope_tile(by_row_ref, by_col_ref, is_tail, tail_value):
    groups = ROW_TILE // GRID_W
    rp = by_row_ref[...]
    by_row = jnp.concatenate([jnp.broadcast_to(rp[j:j + 1, :], (GRID_W, 128)) for j in range(groups)], axis=0)
    by_col = jnp.concatenate([by_col_ref[...]] * groups, axis=0)
    lane = lax.broadcasted_iota(jnp.int32, (1, 128), 1)
    t = jnp.where((lane & (AT_HD - 1)) < AT_HD // 2, by_row, by_col)
    return jnp.where(is_tail, tail_value, t)


def _mix_proj_kernel(h_ref, nw_ref, win_ref, lbf_ref, lbb_ref, qnw_ref, knw_ref, rcos_ref, rsin_ref,
                     ccos_ref, csin_ref, bdq_ref, bdk_ref,
                     hq_ref, hv_ref, kf_ref, lff_ref, kb_ref, lfb_ref, gs_ref, qT_ref, k_ref, vT_ref,
                     ga_ref, gb_ref, *, n_valid, n_x_tiles):
    i = pl.program_id(0)
    un = _rms(h_ref[...], nw_ref[...]).astype(BF16)

    def proj(piece):
        off, size = IN_OFFS[piece], IN_SIZES[piece]
        return _dot(un, win_ref[:, off:off + size])

    row = i * ROW_TILE + lax.broadcasted_iota(jnp.int32, (ROW_TILE, 1), 0)
    valid = row < n_valid

    zq = proj(5)
    zk = proj(6)
    cos = _rope_tile(rcos_ref, ccos_ref, i >= n_x_tiles, 1.0)
    sin = _rope_tile(rsin_ref, csin_ref, i >= n_x_tiles, 0.0)
    cos_q = jnp.concatenate([cos] * (AT_W // 128), axis=1)
    sin_q = jnp.concatenate([sin] * (AT_W // 128), axis=1)
    ms_q = _dot((zq * zq).astype(BF16), bdq_ref[...])
    ms_k = _dot((zk * zk).astype(BF16), bdk_ref[...])
    q = _head_rms_rope(zq, ms_q, qnw_ref[...], cos_q, sin_q) * Q_SCALE
    qT_ref[...] = jnp.transpose(q).astype(BF16)
    k_ref[...] = _head_rms_rope(zk, ms_k, knw_ref[...], cos, sin).astype(BF16)

    hq_ref[...] = _silu(proj(0))
    hv_ref[...] = proj(1).astype(BF16)
    kf, lff = _lower_bound_gate(proj(2), lbf_ref[...], valid)
    kf_ref[...] = kf
    lff_ref[...] = lff
    kb, lfb = _lower_bound_gate(proj(3), lbb_ref[...], valid)
    kb_ref[...] = kb
    lfb_ref[...] = lfb
    gs_ref[...] = _silu(proj(4))

    vT = jnp.transpose(proj(7)).astype(BF16)
    ones = jnp.ones((V_ROWS - AT_HD, ROW_TILE), BF16)
    for g in range(AT_KV_HEADS):
        vT_ref[g, 0:AT_HD, :] = vT[g * AT_HD:(g + 1) * AT_HD, :]
        vT_ref[g, AT_HD:V_ROWS, :] = ones

    ga_ref[...] = _sigmoid(proj(8))
    gb_ref[...] = _sigmoid(proj(9))


def _mix_proj_call(h1, nw, win, lbf, lbb, qnw, knw, rope, bdq, bdk, n_valid):
    rows = h1.shape[0]
    n_x_tiles = rows // ROW_TILE - 1
    groups = ROW_TILE // GRID_W
    rope_row_spec = pl.BlockSpec((groups, 128), lambda i: (jnp.minimum(i, n_x_tiles - 1), 0))
    row_spec = lambda w: pl.BlockSpec((ROW_TILE, w), lambda i: (i, 0))
    out_shape = [
        jax.ShapeDtypeStruct((rows, HG_W), F32),
        jax.ShapeDtypeStruct((rows, HG_W), BF16),
        jax.ShapeDtypeStruct((rows, HG_W), F32),
        jax.ShapeDtypeStruct((rows, HG_W), F32),
        jax.ShapeDtypeStruct((rows, HG_W), F32),
        jax.ShapeDtypeStruct((rows, HG_W), F32),
        jax.ShapeDtypeStruct((rows, HG_W), F32),
        jax.ShapeDtypeStruct((AT_W, rows), BF16),
        jax.ShapeDtypeStruct((rows, AT_KVW), BF16),
        jax.ShapeDtypeStruct((AT_KV_HEADS, V_ROWS, rows), BF16),
        jax.ShapeDtypeStruct((rows, D_MODEL), F32),
        jax.ShapeDtypeStruct((rows, D_MODEL), F32),
    ]
    out_specs = [
        row_spec(HG_W), row_spec(HG_W), row_spec(HG_W), row_spec(HG_W), row_spec(HG_W), row_spec(HG_W),
        row_spec(HG_W),
        pl.BlockSpec((AT_W, ROW_TILE), lambda i: (0, i)),
        row_spec(AT_KVW),
        pl.BlockSpec((AT_KV_HEADS, V_ROWS, ROW_TILE), lambda i: (0, 0, i)),
        row_spec(D_MODEL), row_spec(D_MODEL),
    ]
    in_specs = [
        row_spec(D_MODEL),
        _const_spec((1, D_MODEL)),
        _const_spec((D_MODEL, D_IN)),
        _const_spec(lbf.shape),
        _const_spec(lbb.shape),
        _const_spec((1, AT_W)),
        _const_spec((1, AT_KVW)),
        rope_row_spec,
        rope_row_spec,
        _const_spec((GRID_W, 128)),
        _const_spec((GRID_W, 128)),
        _const_spec((AT_W, AT_W)),
        _const_spec((AT_KVW, AT_KVW)),
    ]
    return pl.pallas_call(
        functools.partial(_mix_proj_kernel, n_valid=n_valid, n_x_tiles=n_x_tiles),
        out_shape=out_shape,
        grid=(rows // ROW_TILE,),
        in_specs=in_specs,
        out_specs=out_specs,
        compiler_params=pltpu.CompilerParams(
            dimension_semantics=("arbitrary",), vmem_limit_bytes=VMEM_LIMIT),
        name="mix_proj",
    )(h1, nw, win, lbf, lbb, qnw, knw, *rope, bdq, bdk)


def _split2(x):
    hi = x.astype(BF16)
    lo = (x - hi.astype(F32)).astype(BF16)
    return hi, lo


def _level_operand(q, k, b, lf, blk, reverse):
    c = CHUNK
    if blk >= 8:
        pieces = []
        for g in range(c // (2 * blk)):
            r0 = g * 2 * blk
            lo, hi = slice(r0, r0 + blk), slice(r0 + blk, r0 + 2 * blk)
            if reverse:
                ref = b[r0 + blk:r0 + blk + 1, :]
                pieces.append(q[lo] * jnp.exp2(b[lo] - ref))
                pieces.append(k[hi] * jnp.exp2(ref - b[hi]))
            else:
                ref = b[r0 + blk - 1:r0 + blk, :]
                pieces.append(k[lo] * jnp.exp2(ref - b[lo]))
                pieces.append(q[hi] * jnp.exp2(b[hi] - ref))
        return jnp.concatenate(pieces, axis=0).astype(BF16)

    q3, k3, b3, lf3 = (x.reshape(c // 8, 8, HG_W) for x in (q, k, b, lf))
    sub = lax.broadcasted_iota(jnp.int32, (1, 8, 1), 1)
    later = ((sub >> (blk.bit_length() - 1)) & 1) == 1
    q_role = jnp.logical_not(later) if reverse else later
    if blk == 4:
        ref = b3[:, 4:5, :] if reverse else b3[:, 3:4, :]
        gl = -jnp.abs(b3 - ref)
    elif blk == 2:
        up = pltpu.roll(lf3, 7, 1)
        dn = pltpu.roll(lf3, 1, 1)
        m4 = sub & 3
        if reverse:
            gl = jnp.where(m4 == 0, lf3 + up, jnp.where(m4 == 1, lf3, jnp.where(m4 == 2, 0.0, dn)))
        else:
            gl = jnp.where(m4 == 0, up, jnp.where(m4 == 1, 0.0, jnp.where(m4 == 2, lf3, dn + lf3)))
    else:
        odd = (sub & 1) == 1
        gl = jnp.where(odd, 0.0, lf3) if reverse else jnp.where(odd, lf3, 0.0)
    y3 = jnp.where(q_role, q3, k3) * jnp.exp2(gl)
    return y3.reshape(c, HG_W).astype(BF16)


def _hgrn_direction(q_ref, v_ref, k_ref, lf_ref, o_ref, st_ref, reverse):
    c = CHUNK
    q = q_ref[...]
    k = k_ref[...]
    v = v_ref[...]
    lf = lf_ref[...]

    r_i = lax.broadcasted_iota(jnp.int32, (c, c), 0)
    c_i = lax.broadcasted_iota(jnp.int32, (c, c), 1)
    tri = ((c_i >= r_i) if reverse else (c_i <= r_i)).astype(BF16)
    hi, lo = _split2(lf)
    b = _dot(jnp.concatenate([tri, tri], axis=1), jnp.concatenate([hi, lo], axis=0))

    levels = (64, 32, 16, 8, 4, 2, 1)
    ys = [_level_operand(q, k, b, lf, blk, reverse) for blk in levels]

    lane = lax.broadcasted_iota(jnp.int32, (1, c), 1)
    small_valid = {}
    for blk in (4, 2, 1):
        sh = blk.bit_length() - 1
        same = (r_i >> (sh + 1)) == (c_i >> (sh + 1))
        t_later = ((r_i >> sh) & 1) == 1
        s_later = ((c_i >> sh) & 1) == 1
        small_valid[blk] = (same & jnp.logical_not(t_later) & s_later) if reverse else (
            same & t_later & jnp.logical_not(s_later))

    def tile_mask(blk, tile):
        if blk < 8:
            return small_valid[blk][8 * tile:8 * tile + 8, :]
        block = (8 * tile) // blk
        if (block % 2 == 1) == reverse:
            return None
        col0 = (block + 1) * blk if reverse else (block - 1) * blk
        return (lane >= col0) & (lane < col0 + blk)

    b_last = b[0:1, :] if reverse else b[c - 1:c, :]
    q_inter = (q * jnp.exp2(b)).astype(BF16)
    k_state = (k * jnp.exp2(b_last - b)).astype(BF16)
    e_last = jnp.exp2(b_last)
    qk = q * k
    eye = r_i == c_i

    def issue(h):
        hs = slice(h * HG_K, (h + 1) * HG_K)
        products = [_dot_nt(y[:, hs], y[:, hs]) for y in ys]
        st = st_ref[h]
        inter = _dot_nt(q_inter[:, hs], st.astype(BF16))
        st_ref[h] = st * e_last[:, hs] + _dot_tn(v[:, hs], k_state[:, hs])
        return products, inter

    def finish(h, issued):
        products, inter = issued
        hs = slice(h * HG_K, (h + 1) * HG_K)
        diag = jnp.where(eye, jnp.sum(qk[:, hs], axis=1, keepdims=True), 0.0)
        tiles = []
        for t in range(c // 8):
            rows = slice(8 * t, 8 * t + 8)
            a_t = diag[rows]
            for blk, p in zip(levels, products):
                mask = tile_mask(blk, t)
                if mask is not None:
                    a_t = jnp.where(mask, p[rows], a_t)
            tiles.append(a_t)
        a = jnp.concatenate(tiles, axis=0)
        o_ref[:, hs] = _dot(a.astype(BF16), v[:, hs]) + inter

    return issue, finish


def _hgrn_meta_state(v_ref, k_ref, lf_ref, st_ref):
    c = CHUNK
    k = k_ref[...]
    v = v_ref[...]
    r_i = lax.broadcasted_iota(jnp.int32, (c, c), 0)
    c_i = lax.broadcasted_iota(jnp.int32, (c, c), 1)
    tri = (c_i <= r_i).astype(BF16)
    hi, lo = _split2(lf_ref[...])
    b = _dot(jnp.concatenate([tri, tri], axis=1), jnp.concatenate([hi, lo], axis=0))
    k_state = (k * jnp.exp2(b[c - 1:c, :] - b)).astype(BF16)
    for h in range(HG_HEADS):
        hs = slice(h * HG_K, (h + 1) * HG_K)
        st_ref[h] = _dot_tn(v[:, hs], k_state[:, hs])


def _hgrn_kernel(qf_ref, vf_ref, kf_ref, lff_ref, qb_ref, vb_ref, kb_ref, lfb_ref, vm_ref, kfm_ref, lfm_ref,
                 of_ref, ob_ref, sf_ref, sb_ref):
    @pl.when(pl.program_id(0) == 0)
    def _():
        sb_ref[...] = jnp.zeros_like(sb_ref)
        _hgrn_meta_state(vm_ref, kfm_ref, lfm_ref, sf_ref)

    def sub(ref, j):
        return ref.at[pl.ds(j * CHUNK, CHUNK)]

    def prepare(j):
        jb = HGRN_STEP_CHUNKS - 1 - j
        fwd = (sub(qf_ref, j), sub(vf_ref, j), sub(kf_ref, j), sub(lff_ref, j), sub(of_ref, j), sf_ref, False)
        bwd = (sub(qb_ref, jb), sub(vb_ref, jb), sub(kb_ref, jb), sub(lfb_ref, jb), sub(ob_ref, jb), sb_ref, True)
        return [_hgrn_direction(*fwd), _hgrn_direction(*bwd)]

    pending = None
    ready = prepare(0)
    for j in range(HGRN_STEP_CHUNKS):
        current = ready
        if j + 1 < HGRN_STEP_CHUNKS:
            ready = prepare(j + 1)
        for issue, finish in current:
            for h in range(HG_HEADS):
                issued = issue(h)
                if pending is not None:
                    pending[0](pending[1], pending[2])
                pending = (finish, h, issued)
    pending[0](pending[1], pending[2])


def _hgrn_call(hq, hv, kf, lff, kb, lfb, n_real):
    nb = n_real // CHUNK
    steps = nb // HGRN_STEP_CHUNKS
    fwd = lambda s: (s, 0)
    bwd = lambda s: (steps - 1 - s, 0)
    spec = lambda m: pl.BlockSpec((HGRN_STEP_CHUNKS * CHUNK, HG_W), m)
    meta = pl.BlockSpec((CHUNK, HG_W), lambda s: (nb, 0))
    out = jax.ShapeDtypeStruct((n_real, HG_W), F32)
    return pl.pallas_call(
        _hgrn_kernel,
        out_shape=[out, out],
        grid=(steps,),
        in_specs=[spec(fwd), spec(fwd), spec(fwd), spec(fwd), spec(bwd), spec(bwd), spec(bwd), spec(bwd),
                  meta, meta, meta],
        out_specs=[spec(fwd), spec(bwd)],
        scratch_shapes=[pltpu.VMEM((HG_HEADS, HG_K, HG_K), F32), pltpu.VMEM((HG_HEADS, HG_K, HG_K), F32)],
        compiler_params=pltpu.CompilerParams(
            dimension_semantics=("arbitrary",), vmem_limit_bytes=VMEM_LIMIT),
        name="hgrn",
    )(hq, hv, kf, lff, hq, hv, kb, lfb, hv, kf, lff)


def _flash_kernel(qT_ref, k_ref, vT_ref, kt_ref, vTt_ref, o_ref, qp_ref, m_ref, acc_ref, s_ref, mc_ref,
                  st_ref, mct_ref, *,
                  n_kv, n_tail_valid):
    g = pl.program_id(0)
    tq = qT_ref.shape[1]

    half = lax.broadcasted_iota(jnp.int32, (AT_KVW, tq), 0) >> 6
    for h in range(AT_GROUP):
        qh = qT_ref[h * AT_HD:(h + 1) * AT_HD, :].astype(F32)
        q2 = jnp.concatenate([qh, qh], axis=0)
        qp_ref[h] = jnp.where(half == g, q2, 0.0).astype(BF16)

    m_ref[...] = jnp.full(m_ref.shape, -jnp.inf, F32)
    acc_ref[...] = jnp.zeros(acc_ref.shape, F32)

    def store_scores(sT, s_dst, mc_dst, h):
        s_dst[h] = sT
        mc_dst[h] = jnp.max(sT, axis=0, keepdims=True)

    def unit_scores(chunk, h, slot):
        kc = k_ref[pl.ds(pl.multiple_of(chunk * KV_TILE, KV_TILE), KV_TILE), :]
        store_scores(_dot(kc, qp_ref[h]), s_ref.at[slot], mc_ref.at[slot], h)

    def unit_consume(vc, s_src, mc_src, h):
        m_prev = m_ref[h]
        m_new = jnp.maximum(m_prev, mc_src[h])
        alpha = jnp.exp2(m_prev - m_new)
        pT = jnp.exp2(s_src[h] - m_new).astype(BF16)
        acc_ref[h] = alpha * acc_ref[h] + _dot(vc, pT)
        m_ref[h] = m_new

    krow = lax.broadcasted_iota(jnp.int32, (128, 1), 0)
    for h in range(AT_GROUP):
        sT = jnp.where(krow < n_tail_valid, _dot(kt_ref[...], qp_ref[h]), -jnp.inf)
        store_scores(sT, st_ref, mct_ref, h)
    for u in range(FLASH_LEAD):
        unit_scores(u // AT_GROUP, u % AT_GROUP, (u // AT_GROUP) % 2)
    for h in range(AT_GROUP):
        unit_consume(vTt_ref[...], st_ref, mct_ref, h)

    def body(i, carry):
        for j in range(2 * AT_GROUP):
            uq = j + FLASH_LEAD
            cq = jnp.minimum(2 * i + uq // AT_GROUP, n_kv - 1)
            unit_scores(cq, uq % AT_GROUP, (uq // AT_GROUP) % 2)
            cp = 2 * i + j // AT_GROUP
            vc = vT_ref[:, pl.ds(pl.multiple_of(cp * KV_TILE, KV_TILE), KV_TILE)]
            unit_consume(vc, s_ref.at[(j // AT_GROUP) % 2], mc_ref.at[(j // AT_GROUP) % 2], j % AT_GROUP)
        return carry

    lax.fori_loop(0, n_kv // 2, body, 0, unroll=FLASH_UNROLL)

    outs = []
    for h in range(AT_GROUP):
        acc = acc_ref[h]
        outs.append(acc[0:AT_HD, :] / acc[AT_HD:AT_HD + 1, :])
    o_ref[...] = jnp.transpose(jnp.concatenate(outs, axis=0)).astype(BF16)


def _flash_call(qT, k, vT, n_real):
    n_kv = n_real // KV_TILE
    tail_blk = n_real // 128
    gw = AT_GROUP * AT_HD
    return pl.pallas_call(
        functools.partial(_flash_kernel, n_kv=n_kv, n_tail_valid=N_META),
        out_shape=jax.ShapeDtypeStruct((n_real, AT_W), BF16),
        grid=(AT_KV_HEADS, n_real // Q_TILE),
        in_specs=[
            pl.BlockSpec((gw, Q_TILE), lambda g, i: (g, i)),
            pl.BlockSpec((n_real, AT_KVW), lambda g, i: (0, 0)),
            pl.BlockSpec((None, V_ROWS, n_real), lambda g, i: (g, 0, 0)),
            pl.BlockSpec((128, AT_KVW), lambda g, i: (tail_blk, 0)),
            pl.BlockSpec((None, V_ROWS, 128), lambda g, i: (g, 0, tail_blk)),
        ],
        out_specs=pl.BlockSpec((Q_TILE, gw), lambda g, i: (i, g)),
        scratch_shapes=[
            pltpu.VMEM((AT_GROUP, AT_KVW, Q_TILE), BF16),
            pltpu.VMEM((AT_GROUP, 1, Q_TILE), F32),
            pltpu.VMEM((AT_GROUP, V_ROWS, Q_TILE), F32),
            pltpu.VMEM((2, AT_GROUP, KV_TILE, Q_TILE), F32),
            pltpu.VMEM((2, AT_GROUP, 1, Q_TILE), F32),
            pltpu.VMEM((AT_GROUP, 128, Q_TILE), F32),
            pltpu.VMEM((AT_GROUP, 1, Q_TILE), F32),
        ],
        compiler_params=pltpu.CompilerParams(
            dimension_semantics=("arbitrary", "arbitrary"), vmem_limit_bytes=VMEM_LIMIT),
        name="flash",
    )(qT, k, vT, k, vT)


def _merge_ffn_kernel(h1_ref, of_ref, ob_ref, gs_ref, yb_ref, ga_ref, gb_ref, hgw_ref,
                      wua_ref, wub_ref, wout_ref, nw_ref, wg_ref, wu_ref, wd_ref, o_ref):
    o = of_ref[...] + ob_ref[...]
    normed = []
    for h in range(HG_HEADS):
        oh = o[:, h * HG_K:(h + 1) * HG_K]
        normed.append(oh * lax.rsqrt(jnp.mean(oh * oh, axis=-1, keepdims=True) + EPS))
    ya = (jnp.concatenate(normed, axis=1) * hgw_ref[...] * gs_ref[...]).astype(BF16)
    mixed = ga_ref[...] * _dot(ya, wua_ref[...]) + gb_ref[...] * _dot(yb_ref[...], wub_ref[...])
    h2 = h1_ref[...] + _dot(mixed.astype(BF16), wout_ref[...])
    o_ref[...] = _swiglu_half_step(h2, nw_ref[...], wg_ref, wu_ref, wd_ref)


def _merge_ffn_call(h1, o_f, o_b, gs, yb, ga, gb, hgw, wua, wub, wout, nw, wg, wu, wd, n_real):
    row_spec = lambda w: pl.BlockSpec((ROW_TILE, w), lambda i: (i, 0))
    return pl.pallas_call(
        _merge_ffn_kernel,
        out_shape=jax.ShapeDtypeStruct((n_real, D_MODEL), F32),
        grid=(n_real // ROW_TILE,),
        in_specs=[
            row_spec(D_MODEL), row_spec(HG_W), row_spec(HG_W), row_spec(HG_W), row_spec(AT_W),
            row_spec(D_MODEL), row_spec(D_MODEL),
            _const_spec((1, HG_W)),
            _const_spec((HG_W, D_MODEL)),
            _const_spec((AT_W, D_MODEL)),
            _const_spec((D_MODEL, D_MODEL)),
            _const_spec((1, D_MODEL)),
            _const_spec((D_MODEL, D_FF)),
            _const_spec((D_MODEL, D_FF)),
            _const_spec((D_FF, D_MODEL)),
        ],
        out_specs=row_spec(D_MODEL),
        compiler_params=pltpu.CompilerParams(
            dimension_semantics=("arbitrary",), vmem_limit_bytes=VMEM_LIMIT),
        name="merge_ffn2",
    )(h1, o_f, o_b, gs, yb, ga, gb, hgw, wua, wub, wout, nw, wg, wu, wd)


def _rope_tables(n_real):
    half = AT_HD // 2
    inv = ROPE_THETA ** (-jnp.arange(0, half, 2, dtype=F32) / half)
    sign = jnp.tile(jnp.array([-1.0, 1.0], F32), half)

    def lanes(ang, first_half):
        cos, sin = jnp.cos(ang), jnp.sin(ang)
        one, zero = jnp.ones_like(cos), jnp.zeros_like(sin)
        cos = jnp.concatenate([cos, one] if first_half else [one, cos], axis=-1)
        sin = jnp.concatenate([sin, zero] if first_half else [zero, sin], axis=-1)
        cos = jnp.repeat(cos, 2, axis=-1)
        sin = jnp.repeat(sin, 2, axis=-1) * sign
        reps = 128 // AT_HD
        return jnp.tile(cos, (1, reps)), jnp.tile(sin, (1, reps))

    r = jnp.arange(n_real // GRID_W, dtype=F32)
    c = jnp.arange(GRID_W, dtype=F32)
    return lanes(r[:, None] * inv, True) + lanes(c[:, None] * inv, False)


def _head_mean_matrix(width):
    heads = width // AT_HD
    return jnp.kron(jnp.eye(heads, dtype=F32), jnp.full((AT_HD, AT_HD), 1.0 / AT_HD, F32)).astype(BF16)


def kernel(x, meta_tokens, ffn1_norm, ffn1_w_gate, ffn1_w_up, ffn1_w_down, mix_norm, w_in, hg_lb_fwd, hg_lb_bwd, hg_out_norm, q_norm, k_norm, w_up_a, w_up_b, w_out, ffn2_norm, ffn2_w_gate, ffn2_w_up, ffn2_w_down):
    batch, n_real, _ = x.shape
    assert batch == 1 and n_real % ROW_TILE == 0 and n_real % GRID_W == 0
    rows = n_real + ROW_TILE
    n_valid = n_real + N_META

    tail = jnp.concatenate(
        [meta_tokens.astype(x.dtype), jnp.zeros((ROW_TILE - N_META, D_MODEL), x.dtype)], axis=0)
    row = lambda w: w.reshape(1, -1).astype(F32)
    (w1g, w1u, w1d, win, wua, wub, wout, w2g, w2u, w2d) = _cast_weights(
        [w[0].astype(F32) for w in (ffn1_w_gate, ffn1_w_up, ffn1_w_down, w_in, w_up_a, w_up_b, w_out,
                                    ffn2_w_gate, ffn2_w_up, ffn2_w_down)])

    h1 = _ffn_call(x[0], tail, row(ffn1_norm[0]), w1g, w1u, w1d)

    rope = _rope_tables(n_real)
    hq, hv, kf, lff, kb, lfb, gs, qT, k, vT, ga, gb = _mix_proj_call(
        h1, row(mix_norm[0]), win, hg_lb_fwd.astype(F32), hg_lb_bwd.astype(F32),
        row(jnp.tile(q_norm[0], AT_HEADS)), row(jnp.tile(k_norm[0], AT_KV_HEADS)), rope,
        _head_mean_matrix(AT_W), _head_mean_matrix(AT_KVW), n_valid)

    o_f, o_b = _hgrn_call(hq, hv, kf, lff, kb, lfb, n_real)
    yb = _flash_call(qT, k, vT, n_real)

    out = _merge_ffn_call(
        h1, o_f, o_b, gs, yb, ga, gb, row(hg_out_norm[0]), wua, wub, wout,
        row(ffn2_norm[0]), w2g, w2u, w2d, n_real)
    return out.reshape(batch, n_real, D_MODEL)
```

```python
import functools

import jax
import jax.numpy as jnp
from jax import lax
from jax.experimental import pallas as pl
from jax.experimental.pallas import tpu as pltpu

F32 = jnp.float32
BF16 = jnp.bfloat16

D_MODEL = 1024
D_FF = 2816
N_META = 16
GRID_W = 64
EPS = 1e-6
HG_HEADS = 4
HG_K = 128
HG_W = HG_HEADS * HG_K
AT_HEADS = 8
AT_KV_HEADS = 2
AT_HD = 64
AT_GROUP = AT_HEADS // AT_KV_HEADS
AT_W = AT_HEADS * AT_HD
AT_KVW = AT_KV_HEADS * AT_HD
ROPE_THETA = 10000.0
IN_SIZES = (HG_W, HG_W, HG_W, HG_W, HG_W, AT_W, AT_KVW, AT_KVW, D_MODEL, D_MODEL)
IN_OFFS = tuple(sum(IN_SIZES[:i]) for i in range(len(IN_SIZES)))
D_IN = sum(IN_SIZES)

ROW_TILE = 512
FF_CHUNK = 256
CHUNK = 128
HGRN_STEP_CHUNKS = 4
Q_TILE = 256
KV_TILE = 256
V_ROWS = AT_HD + 16
LOG2E = 1.4426950408889634
Q_SCALE = AT_HD ** -0.5 * LOG2E
FLASH_UNROLL = 16
FLASH_LEAD = AT_GROUP + 1
VMEM_LIMIT = 56 * 1024 * 1024


def _dot(a, b):
    return jnp.dot(a, b, preferred_element_type=F32)


def _dot_nt(a, b):
    return lax.dot_general(a, b, (((1,), (1,)), ((), ())), preferred_element_type=F32)


def _dot_tn(a, b):
    return lax.dot_general(a, b, (((0,), (0,)), ((), ())), preferred_element_type=F32)


def _rms(x, w):
    ms = jnp.mean(x * x, axis=-1, keepdims=True)
    return x * lax.rsqrt(ms + EPS) * w


def _sigmoid(x):
    return 0.5 * jnp.tanh(0.5 * x) + 0.5


def _silu(x):
    h = 0.5 * x
    return h * jnp.tanh(h) + h


def _swiglu_half_step(h, norm_w, wg_ref, wu_ref, wd_ref):
    hn = _rms(h, norm_w).astype(BF16)
    acc = None
    pending = None
    for c in range(D_FF // FF_CHUNK + 1):
        if c < D_FF // FF_CHUNK:
            sl = slice(c * FF_CHUNK, (c + 1) * FF_CHUNK)
            g = _dot(hn, wg_ref[:, sl])
            u = _dot(hn, wu_ref[:, sl])
        if pending is not None:
            d = _dot(pending[0], wd_ref[pending[1], :])
            acc = d if acc is None else acc + d
        if c < D_FF // FF_CHUNK:
            pending = ((_silu(g) * u).astype(BF16), sl)
    return h + 0.5 * acc


CAST_STEPS = 16


def _cast_kernel(*refs):
    n = len(refs) // 2
    for src, dst in zip(refs[:n], refs[n:]):
        dst[...] = src[...].astype(BF16)


def _cast_weights(ws):
    specs = [pl.BlockSpec((w.shape[0] // CAST_STEPS, w.shape[1]), lambda i: (i, 0)) for w in ws]
    return pl.pallas_call(
        _cast_kernel,
        out_shape=[jax.ShapeDtypeStruct(w.shape, BF16) for w in ws],
        grid=(CAST_STEPS,),
        in_specs=specs,
        out_specs=specs,
        compiler_params=pltpu.CompilerParams(
            dimension_semantics=("arbitrary",), vmem_limit_bytes=VMEM_LIMIT),
        name="cast_weights",
    )(*ws)


def _ffn_kernel(x_ref, tail_ref, nw_ref, wg_ref, wu_ref, wd_ref, o_ref, *, n_x_tiles):
    h = jnp.where(pl.program_id(0) < n_x_tiles, x_ref[...], tail_ref[...])
    o_ref[...] = _swiglu_half_step(h, nw_ref[...], wg_ref, wu_ref, wd_ref)


def _const_spec(shape):
    nd = len(shape)
    return pl.BlockSpec(shape, lambda *_: (0,) * nd, pipeline_mode=pl.Buffered(1))


def _ffn_call(x, tail, nw, wg, wu, wd):
    n_x_tiles = x.shape[0] // ROW_TILE
    rows = x.shape[0] + tail.shape[0]
    return pl.pallas_call(
        functools.partial(_ffn_kernel, n_x_tiles=n_x_tiles),
        out_shape=jax.ShapeDtypeStruct((rows, D_MODEL), F32),
        grid=(n_x_tiles + 1,),
        in_specs=[
            pl.BlockSpec((ROW_TILE, D_MODEL), lambda i: (jnp.minimum(i, n_x_tiles - 1), 0)),
            _const_spec((ROW_TILE, D_MODEL)),
            _const_spec((1, D_MODEL)),
            _const_spec((D_MODEL, D_FF)),
            _const_spec((D_MODEL, D_FF)),
            _const_spec((D_FF, D_MODEL)),
        ],
        out_specs=pl.BlockSpec((ROW_TILE, D_MODEL), lambda i: (i, 0)),
        compiler_params=pltpu.CompilerParams(
            dimension_semantics=("arbitrary",), vmem_limit_bytes=VMEM_LIMIT),
        name="ffn1",
    )(x, tail, nw, wg, wu, wd)


def _lower_bound_gate(z, lbp, valid):
    m = jnp.max(lbp, axis=0, keepdims=True)
    e = jnp.exp(lbp - m)
    lb = e[0:1, :] / jnp.sum(e, axis=0, keepdims=True)
    kk = (1.0 - lb) * _sigmoid(-z)
    kk = jnp.where(valid, kk, 0.0)
    return kk, jnp.log2(1.0 - kk)


def _head_rms_rope(z, ms, w, cos, sin):
    zn = z * lax.rsqrt(ms + EPS) * w
    width = z.shape[1]
    lane = lax.broadcasted_iota(jnp.int32, z.shape, 1)
    partner = jnp.where((lane & 1) == 0, pltpu.roll(zn, width - 1, 1), pltpu.roll(zn, 1, 1))
    return zn * cos + partner * sin


def _rope_tile(by_row_ref, by_col_ref, is_tail, tail_value):
    groups = ROW_TILE // GRID_W
    rp = by_row_ref[...]
    by_row = jnp.concatenate([jnp.broadcast_to(rp[j:j + 1, :], (GRID_W, 128)) for j in range(groups)], axis=0)
    by_col = jnp.concatenate([by_col_ref[...]] * groups, axis=0)
    lane = lax.broadcasted_iota(jnp.int32, (1, 128), 1)
    t = jnp.where((lane & (AT_HD - 1)) < AT_HD // 2, by_row, by_col)
    return jnp.where(is_tail, tail_value, t)


def _mix_proj_kernel(h_ref, nw_ref, win_ref, lbf_ref, lbb_ref, qnw_ref, knw_ref, rcos_ref, rsin_ref,
                     ccos_ref, csin_ref, bdq_ref, bdk_ref,
                     hq_ref, hv_ref, kf_ref, lff_ref, kb_ref, lfb_ref, gs_ref, qT_ref, k_ref, vT_ref,
                     ga_ref, gb_ref, *, n_valid, n_x_tiles):
    i = pl.program_id(0)
    un = _rms(h_ref[...], nw_ref[...]).astype(BF16)

    def proj(piece):
        off, size = IN_OFFS[piece], IN_SIZES[piece]
        return _dot(un, win_ref[:, off:off + size])

    row = i * ROW_TILE + lax.broadcasted_iota(jnp.int32, (ROW_TILE, 1), 0)
    valid = row < n_valid

    zq = proj(5)
    zk = proj(6)
    cos = _rope_tile(rcos_ref, ccos_ref, i >= n_x_tiles, 1.0)
    sin = _rope_tile(rsin_ref, csin_ref, i >= n_x_tiles, 0.0)
    cos_q = jnp.concatenate([cos] * (AT_W // 128), axis=1)
    sin_q = jnp.concatenate([sin] * (AT_W // 128), axis=1)
    ms_q = _dot((zq * zq).astype(BF16), bdq_ref[...])
    ms_k = _dot((zk * zk).astype(BF16), bdk_ref[...])
    q = _head_rms_rope(zq, ms_q, qnw_ref[...], cos_q, sin_q) * Q_SCALE
    qT_ref[...] = jnp.transpose(q).astype(BF16)
    k_ref[...] = _head_rms_rope(zk, ms_k, knw_ref[...], cos, sin).astype(BF16)

    hq_ref[...] = _silu(proj(0))
    hv_ref[...] = proj(1).astype(BF16)
    kf, lff = _lower_bound_gate(proj(2), lbf_ref[...], valid)
    kf_ref[...] = kf
    lff_ref[...] = lff
    kb, lfb = _lower_bound_gate(proj(3), lbb_ref[...], valid)
    kb_ref[...] = kb
    lfb_ref[...] = lfb
    gs_ref[...] = _silu(proj(4))

    vT = jnp.transpose(proj(7)).astype(BF16)
    ones = jnp.ones((V_ROWS - AT_HD, ROW_TILE), BF16)
    for g in range(AT_KV_HEADS):
        vT_ref[g, 0:AT_HD, :] = vT[g * AT_HD:(g + 1) * AT_HD, :]
        vT_ref[g, AT_HD:V_ROWS, :] = ones

    ga_ref[...] = _sigmoid(proj(8))
    gb_ref[...] = _sigmoid(proj(9))


def _mix_proj_call(h1, nw, win, lbf, lbb, qnw, knw, rope, bdq, bdk, n_valid):
    rows = h1.shape[0]
    n_x_tiles = rows // ROW_TILE - 1
    groups = ROW_TILE // GRID_W
    rope_row_spec = pl.BlockSpec((groups, 128), lambda i: (jnp.minimum(i, n_x_tiles - 1), 0))
    row_spec = lambda w: pl.BlockSpec((ROW_TILE, w), lambda i: (i, 0))
    out_shape = [
        jax.ShapeDtypeStruct((rows, HG_W), F32),
        jax.ShapeDtypeStruct((rows, HG_W), BF16),
        jax.ShapeDtypeStruct((rows, HG_W), F32),
        jax.ShapeDtypeStruct((rows, HG_W), F32),
        jax.ShapeDtypeStruct((rows, HG_W), F32),
        jax.ShapeDtypeStruct((rows, HG_W), F32),
        jax.ShapeDtypeStruct((rows, HG_W), F32),
        jax.ShapeDtypeStruct((AT_W, rows), BF16),
        jax.ShapeDtypeStruct((rows, AT_KVW), BF16),
        jax.ShapeDtypeStruct((AT_KV_HEADS, V_ROWS, rows), BF16),
        jax.ShapeDtypeStruct((rows, D_MODEL), F32),
        jax.ShapeDtypeStruct((rows, D_MODEL), F32),
    ]
    out_specs = [
        row_spec(HG_W), row_spec(HG_W), row_spec(HG_W), row_spec(HG_W), row_spec(HG_W), row_spec(HG_W),
        row_spec(HG_W),
        pl.BlockSpec((AT_W, ROW_TILE), lambda i: (0, i)),
        row_spec(AT_KVW),
        pl.BlockSpec((AT_KV_HEADS, V_ROWS, ROW_TILE), lambda i: (0, 0, i)),
        row_spec(D_MODEL), row_spec(D_MODEL),
    ]
    in_specs = [
        row_spec(D_MODEL),
        _const_spec((1, D_MODEL)),
        _const_spec((D_MODEL, D_IN)),
        _const_spec(lbf.shape),
        _const_spec(lbb.shape),
        _const_spec((1, AT_W)),
        _const_spec((1, AT_KVW)),
        rope_row_spec,
        rope_row_spec,
        _const_spec((GRID_W, 128)),
        _const_spec((GRID_W, 128)),
        _const_spec((AT_W, AT_W)),
        _const_spec((AT_KVW, AT_KVW)),
    ]
    return pl.pallas_call(
        functools.partial(_mix_proj_kernel, n_valid=n_valid, n_x_tiles=n_x_tiles),
        out_shape=out_shape,
        grid=(rows // ROW_TILE,),
        in_specs=in_specs,
        out_specs=out_specs,
        compiler_params=pltpu.CompilerParams(
            dimension_semantics=("arbitrary",), vmem_limit_bytes=VMEM_LIMIT),
        name="mix_proj",
    )(h1, nw, win, lbf, lbb, qnw, knw, *rope, bdq, bdk)


def _split2(x):
    hi = x.astype(BF16)
    lo = (x - hi.astype(F32)).astype(BF16)
    return hi, lo


def _level_operand(q, k, b, lf, blk, reverse):
    c = CHUNK
    if blk >= 8:
        pieces = []
        for g in range(c // (2 * blk)):
            r0 = g * 2 * blk
            lo, hi = slice(r0, r0 + blk), slice(r0 + blk, r0 + 2 * blk)
            if reverse:
                ref = b[r0 + blk:r0 + blk + 1, :]
                pieces.append(q[lo] * jnp.exp2(b[lo] - ref))
                pieces.append(k[hi] * jnp.exp2(ref - b[hi]))
            else:
                ref = b[r0 + blk - 1:r0 + blk, :]
                pieces.append(k[lo] * jnp.exp2(ref - b[lo]))
                pieces.append(q[hi] * jnp.exp2(b[hi] - ref))
        return jnp.concatenate(pieces, axis=0).astype(BF16)

    q3, k3, b3, lf3 = (x.reshape(c // 8, 8, HG_W) for x in (q, k, b, lf))
    sub = lax.broadcasted_iota(jnp.int32, (1, 8, 1), 1)
    later = ((sub >> (blk.bit_length() - 1)) & 1) == 1
    q_role = jnp.logical_not(later) if reverse else later
    if blk == 4:
        ref = b3[:, 4:5, :] if reverse else b3[:, 3:4, :]
        gl = -jnp.abs(b3 - ref)
    elif blk == 2:
        up = pltpu.roll(lf3, 7, 1)
        dn = pltpu.roll(lf3, 1, 1)
        m4 = sub & 3
        if reverse:
            gl = jnp.where(m4 == 0, lf3 + up, jnp.where(m4 == 1, lf3, jnp.where(m4 == 2, 0.0, dn)))
        else:
            gl = jnp.where(m4 == 0, up, jnp.where(m4 == 1, 0.0, jnp.where(m4 == 2, lf3, dn + lf3)))
    else:
        odd = (sub & 1) == 1
        gl = jnp.where(odd, 0.0, lf3) if reverse else jnp.where(odd, lf3, 0.0)
    y3 = jnp.where(q_role, q3, k3) * jnp.exp2(gl)
    return y3.reshape(c, HG_W).astype(BF16)


def _hgrn_direction(q_ref, v_ref, k_ref, lf_ref, o_ref, st_ref, reverse):
    c = CHUNK
    q = q_ref[...]
    k = k_ref[...]
    v = v_ref[...]
    lf = lf_ref[...]

    r_i = lax.broadcasted_iota(jnp.int32, (c, c), 0)
    c_i = lax.broadcasted_iota(jnp.int32, (c, c), 1)
    tri = ((c_i >= r_i) if reverse else (c_i <= r_i)).astype(BF16)
    hi, lo = _split2(lf)
    b = _dot(jnp.concatenate([tri, tri], axis=1), jnp.concatenate([hi, lo], axis=0))

    levels = (64, 32, 16, 8, 4, 2, 1)
    ys = [_level_operand(q, k, b, lf, blk, reverse) for blk in levels]

    lane = lax.broadcasted_iota(jnp.int32, (1, c), 1)
    small_valid = {}
    for blk in (4, 2, 1):
        sh = blk.bit_length() - 1
        same = (r_i >> (sh + 1)) == (c_i >> (sh + 1))
        t_later = ((r_i >> sh) & 1) == 1
        s_later = ((c_i >> sh) & 1) == 1
        small_valid[blk] = (same & jnp.logical_not(t_later) & s_later) if reverse else (
            same & t_later & jnp.logical_not(s_later))

    def tile_mask(blk, tile):
        if blk < 8:
            return small_valid[blk][8 * tile:8 * tile + 8, :]
        block = (8 * tile) // blk
        if (block % 2 == 1) == reverse:
            return None
        col0 = (block + 1) * blk if reverse else (block - 1) * blk
        return (lane >= col0) & (lane < col0 + blk)

    b_last = b[0:1, :] if reverse else b[c - 1:c, :]
    q_inter = (q * jnp.exp2(b)).astype(BF16)
    k_state = (k * jnp.exp2(b_last - b)).astype(BF16)
    e_last = jnp.exp2(b_last)
    qk = q * k
    eye = r_i == c_i

    def issue(h):
        hs = slice(h * HG_K, (h + 1) * HG_K)
        products = [_dot_nt(y[:, hs], y[:, hs]) for y in ys]
        st = st_ref[h]
        inter = _dot_nt(q_inter[:, hs], st.astype(BF16))
        st_ref[h] = st * e_last[:, hs] + _dot_tn(v[:, hs], k_state[:, hs])
        return products, inter

    def finish(h, issued):
        products, inter = issued
        hs = slice(h * HG_K, (h + 1) * HG_K)
        diag = jnp.where(eye, jnp.sum(qk[:, hs], axis=1, keepdims=True), 0.0)
        tiles = []
        for t in range(c // 8):
            rows = slice(8 * t, 8 * t + 8)
            a_t = diag[rows]
            for blk, p in zip(levels, products):
                mask = tile_mask(blk, t)
                if mask is not None:
                    a_t = jnp.where(mask, p[rows], a_t)
            tiles.append(a_t)
        a = jnp.concatenate(tiles, axis=0)
        o_ref[:, hs] = _dot(a.astype(BF16), v[:, hs]) + inter

    return issue, finish


def _hgrn_meta_state(v_ref, k_ref, lf_ref, st_ref):
    c = CHUNK
    k = k_ref[...]
    v = v_ref[...]
    r_i = lax.broadcasted_iota(jnp.int32, (c, c), 0)
    c_i = lax.broadcasted_iota(jnp.int32, (c, c), 1)
    tri = (c_i <= r_i).astype(BF16)
    hi, lo = _split2(lf_ref[...])
    b = _dot(jnp.concatenate([tri, tri], axis=1), jnp.concatenate([hi, lo], axis=0))
    k_state = (k * jnp.exp2(b[c - 1:c, :] - b)).astype(BF16)
    for h in range(HG_HEADS):
        hs = slice(h * HG_K, (h + 1) * HG_K)
        st_ref[h] = _dot_tn(v[:, hs], k_state[:, hs])


def _hgrn_kernel(qf_ref, vf_ref, kf_ref, lff_ref, qb_ref, vb_ref, kb_ref, lfb_ref, vm_ref, kfm_ref, lfm_ref,
                 of_ref, ob_ref, sf_ref, sb_ref):
    @pl.when(pl.program_id(0) == 0)
    def _():
        sb_ref[...] = jnp.zeros_like(sb_ref)
        _hgrn_meta_state(vm_ref, kfm_ref, lfm_ref, sf_ref)

    def sub(ref, j):
        return ref.at[pl.ds(j * CHUNK, CHUNK)]

    units = []
    for j in range(HGRN_STEP_CHUNKS):
        jb = HGRN_STEP_CHUNKS - 1 - j
        fwd = (sub(qf_ref, j), sub(vf_ref, j), sub(kf_ref, j), sub(lff_ref, j), sub(of_ref, j), sf_ref, False)
        bwd = (sub(qb_ref, jb), sub(vb_ref, jb), sub(kb_ref, jb), sub(lfb_ref, jb), sub(ob_ref, jb), sb_ref, True)
        for args in (fwd, bwd):
            issue, finish = _hgrn_direction(*args)
            units += [(issue, finish, h) for h in range(HG_HEADS)]

    pending = None
    for issue, finish, h in units:
        issued = issue(h)
        if pending is not None:
            pending[0](pending[1], pending[2])
        pending = (finish, h, issued)
    pending[0](pending[1], pending[2])


def _hgrn_call(hq, hv, kf, lff, kb, lfb, n_real):
    nb = n_real // CHUNK
    steps = nb // HGRN_STEP_CHUNKS
    fwd = lambda s: (s, 0)
    bwd = lambda s: (steps - 1 - s, 0)
    spec = lambda m: pl.BlockSpec((HGRN_STEP_CHUNKS * CHUNK, HG_W), m)
    meta = pl.BlockSpec((CHUNK, HG_W), lambda s: (nb, 0))
    out = jax.ShapeDtypeStruct((n_real, HG_W), F32)
    return pl.pallas_call(
        _hgrn_kernel,
        out_shape=[out, out],
        grid=(steps,),
        in_specs=[spec(fwd), spec(fwd), spec(fwd), spec(fwd), spec(bwd), spec(bwd), spec(bwd), spec(bwd),
                  meta, meta, meta],
        out_specs=[spec(fwd), spec(bwd)],
        scratch_shapes=[pltpu.VMEM((HG_HEADS, HG_K, HG_K), F32), pltpu.VMEM((HG_HEADS, HG_K, HG_K), F32)],
        compiler_params=pltpu.CompilerParams(
            dimension_semantics=("arbitrary",), vmem_limit_bytes=VMEM_LIMIT),
        name="hgrn",
    )(hq, hv, kf, lff, hq, hv, kb, lfb, hv, kf, lff)


def _flash_kernel(qT_ref, k_ref, vT_ref, kt_ref, vTt_ref, o_ref, qp_ref, m_ref, acc_ref, s_ref, mc_ref,
                  st_ref, mct_ref, *,
                  n_kv, n_tail_valid):
    g = pl.program_id(0)
    tq = qT_ref.shape[1]

    half = lax.broadcasted_iota(jnp.int32, (AT_KVW, tq), 0) >> 6
    for h in range(AT_GROUP):
        qh = qT_ref[h * AT_HD:(h + 1) * AT_HD, :].astype(F32)
        q2 = jnp.concatenate([qh, qh], axis=0)
        qp_ref[h] = jnp.where(half == g, q2, 0.0).astype(BF16)

    m_ref[...] = jnp.full(m_ref.shape, -jnp.inf, F32)
    acc_ref[...] = jnp.zeros(acc_ref.shape, F32)

    def store_scores(sT, s_dst, mc_dst, h):
        s_dst[h] = sT
        mc_dst[h] = jnp.max(sT, axis=0, keepdims=True)

    def unit_scores(chunk, h, slot):
        kc = k_ref[pl.ds(pl.multiple_of(chunk * KV_TILE, KV_TILE), KV_TILE), :]
        store_scores(_dot(kc, qp_ref[h]), s_ref.at[slot], mc_ref.at[slot], h)

    def unit_consume(vc, s_src, mc_src, h):
        m_prev = m_ref[h]
        m_new = jnp.maximum(m_prev, mc_src[h])
        alpha = jnp.exp2(m_prev - m_new)
        pT = jnp.exp2(s_src[h] - m_new).astype(BF16)
        acc_ref[h] = alpha * acc_ref[h] + _dot(vc, pT)
        m_ref[h] = m_new

    krow = lax.broadcasted_iota(jnp.int32, (128, 1), 0)
    for h in range(AT_GROUP):
        sT = jnp.where(krow < n_tail_valid, _dot(kt_ref[...], qp_ref[h]), -jnp.inf)
        store_scores(sT, st_ref, mct_ref, h)
    for u in range(FLASH_LEAD):
        unit_scores(u // AT_GROUP, u % AT_GROUP, (u // AT_GROUP) % 2)
    for h in range(AT_GROUP):
        unit_consume(vTt_ref[...], st_ref, mct_ref, h)

    def body(i, carry):
        for j in range(2 * AT_GROUP):
            uq = j + FLASH_LEAD
            cq = jnp.minimum(2 * i + uq // AT_GROUP, n_kv - 1)
            unit_scores(cq, uq % AT_GROUP, (uq // AT_GROUP) % 2)
            cp = 2 * i + j // AT_GROUP
            vc = vT_ref[:, pl.ds(pl.multiple_of(cp * KV_TILE, KV_TILE), KV_TILE)]
            unit_consume(vc, s_ref.at[(j // AT_GROUP) % 2], mc_ref.at[(j // AT_GROUP) % 2], j % AT_GROUP)
        return carry

    lax.fori_loop(0, n_kv // 2, body, 0, unroll=FLASH_UNROLL)

    outs = []
    for h in range(AT_GROUP):
        acc = acc_ref[h]
        outs.append(acc[0:AT_HD, :] / acc[AT_HD:AT_HD + 1, :])
    o_ref[...] = jnp.transpose(jnp.concatenate(outs, axis=0)).astype(BF16)


def _flash_call(qT, k, vT, n_real):
    n_kv = n_real // KV_TILE
    tail_blk = n_real // 128
    gw = AT_GROUP * AT_HD
    return pl.pallas_call(
        functools.partial(_flash_kernel, n_kv=n_kv, n_tail_valid=N_META),
        out_shape=jax.ShapeDtypeStruct((n_real, AT_W), BF16),
        grid=(AT_KV_HEADS, n_real // Q_TILE),
        in_specs=[
            pl.BlockSpec((gw, Q_TILE), lambda g, i: (g, i)),
            pl.BlockSpec((n_real, AT_KVW), lambda g, i: (0, 0)),
            pl.BlockSpec((None, V_ROWS, n_real), lambda g, i: (g, 0, 0)),
            pl.BlockSpec((128, AT_KVW), lambda g, i: (tail_blk, 0)),
            pl.BlockSpec((None, V_ROWS, 128), lambda g, i: (g, 0, tail_blk)),
        ],
        out_specs=pl.BlockSpec((Q_TILE, gw), lambda g, i: (i, g)),
        scratch_shapes=[
            pltpu.VMEM((AT_GROUP, AT_KVW, Q_TILE), BF16),
            pltpu.VMEM((AT_GROUP, 1, Q_TILE), F32),
            pltpu.VMEM((AT_GROUP, V_ROWS, Q_TILE), F32),
            pltpu.VMEM((2, AT_GROUP, KV_TILE, Q_TILE), F32),
            pltpu.VMEM((2, AT_GROUP, 1, Q_TILE), F32),
            pltpu.VMEM((AT_GROUP, 128, Q_TILE), F32),
            pltpu.VMEM((AT_GROUP, 1, Q_TILE), F32),
        ],
        compiler_params=pltpu.CompilerParams(
            dimension_semantics=("arbitrary", "arbitrary"), vmem_limit_bytes=VMEM_LIMIT),
        name="flash",
    )(qT, k, vT, k, vT)


def _merge_ffn_kernel(h1_ref, of_ref, ob_ref, gs_ref, yb_ref, ga_ref, gb_ref, hgw_ref,
                      wua_ref, wub_ref, wout_ref, nw_ref, wg_ref, wu_ref, wd_ref, o_ref):
    o = of_ref[...] + ob_ref[...]
    normed = []
    for h in range(HG_HEADS):
        oh = o[:, h * HG_K:(h + 1) * HG_K]
        normed.append(oh * lax.rsqrt(jnp.mean(oh * oh, axis=-1, keepdims=True) + EPS))
    ya = (jnp.concatenate(normed, axis=1) * hgw_ref[...] * gs_ref[...]).astype(BF16)
    mixed = ga_ref[...] * _dot(ya, wua_ref[...]) + gb_ref[...] * _dot(yb_ref[...], wub_ref[...])
    h2 = h1_ref[...] + _dot(mixed.astype(BF16), wout_ref[...])
    o_ref[...] = _swiglu_half_step(h2, nw_ref[...], wg_ref, wu_ref, wd_ref)


def _merge_ffn_call(h1, o_f, o_b, gs, yb, ga, gb, hgw, wua, wub, wout, nw, wg, wu, wd, n_real):
    row_spec = lambda w: pl.BlockSpec((ROW_TILE, w), lambda i: (i, 0))
    return pl.pallas_call(
        _merge_ffn_kernel,
        out_shape=jax.ShapeDtypeStruct((n_real, D_MODEL), F32),
        grid=(n_real // ROW_TILE,),
        in_specs=[
            row_spec(D_MODEL), row_spec(HG_W), row_spec(HG_W), row_spec(HG_W), row_spec(AT_W),
            row_spec(D_MODEL), row_spec(D_MODEL),
            _const_spec((1, HG_W)),
            _const_spec((HG_W, D_MODEL)),
            _const_spec((AT_W, D_MODEL)),
            _const_spec((D_MODEL, D_MODEL)),
            _const_spec((1, D_MODEL)),
            _const_spec((D_MODEL, D_FF)),
            _const_spec((D_MODEL, D_FF)),
            _const_spec((D_FF, D_MODEL)),
        ],
        out_specs=row_spec(D_MODEL),
        compiler_params=pltpu.CompilerParams(
            dimension_semantics=("arbitrary",), vmem_limit_bytes=VMEM_LIMIT),
        name="merge_ffn2",
    )(h1, o_f, o_b, gs, yb, ga, gb, hgw, wua, wub, wout, nw, wg, wu, wd)


def _rope_tables(n_real):
    half = AT_HD // 2
    inv = ROPE_THETA ** (-jnp.arange(0, half, 2, dtype=F32) / half)
    sign = jnp.tile(jnp.array([-1.0, 1.0], F32), half)

    def lanes(ang, first_half):
        cos, sin = jnp.cos(ang), jnp.sin(ang)
        one, zero = jnp.ones_like(cos), jnp.zeros_like(sin)
        cos = jnp.concatenate([cos, one] if first_half else [one, cos], axis=-1)
        sin = jnp.concatenate([sin, zero] if first_half else [zero, sin], axis=-1)
        cos = jnp.repeat(cos, 2, axis=-1)
        sin = jnp.repeat(sin, 2, axis=-1) * sign
        reps = 128 // AT_HD
        return jnp.tile(cos, (1, reps)), jnp.tile(sin, (1, reps))

    r = jnp.arange(n_real // GRID_W, dtype=F32)
    c = jnp.arange(GRID_W, dtype=F32)
    return lanes(r[:, None] * inv, True) + lanes(c[:, None] * inv, False)


def _head_mean_matrix(width):
    heads = width // AT_HD
    return jnp.kron(jnp.eye(heads, dtype=F32), jnp.full((AT_HD, AT_HD), 1.0 / AT_HD, F32)).astype(BF16)


def kernel(x, meta_tokens, ffn1_norm, ffn1_w_gate, ffn1_w_up, ffn1_w_down, mix_norm, w_in, hg_lb_fwd, hg_lb_bwd, hg_out_norm, q_norm, k_norm, w_up_a, w_up_b, w_out, ffn2_norm, ffn2_w_gate, ffn2_w_up, ffn2_w_down):
    batch, n_real, _ = x.shape
    assert batch == 1 and n_real % ROW_TILE == 0 and n_real % GRID_W == 0
    rows = n_real + ROW_TILE
    n_valid = n_real + N_META

    tail = jnp.concatenate(
        [meta_tokens.astype(x.dtype), jnp.zeros((ROW_TILE - N_META, D_MODEL), x.dtype)], axis=0)
    row = lambda w: w.reshape(1, -1).astype(F32)
    (w1g, w1u, w1d, win, wua, wub, wout, w2g, w2u, w2d) = _cast_weights(
        [w[0].astype(F32) for w in (ffn1_w_gate, ffn1_w_up, ffn1_w_down, w_in, w_up_a, w_up_b, w_out,
                                    ffn2_w_gate, ffn2_w_up, ffn2_w_down)])

    h1 = _ffn_call(x[0], tail, row(ffn1_norm[0]), w1g, w1u, w1d)

    rope = _rope_tables(n_real)
    hq, hv, kf, lff, kb, lfb, gs, qT, k, vT, ga, gb = _mix_proj_call(
        h1, row(mix_norm[0]), win, hg_lb_fwd.astype(F32), hg_lb_bwd.astype(F32),
        row(jnp.tile(q_norm[0], AT_HEADS)), row(jnp.tile(k_norm[0], AT_KV_HEADS)), rope,
        _head_mean_matrix(AT_W), _head_mean_matrix(AT_KVW), n_valid)

    o_f, o_b = _hgrn_call(hq, hv, kf, lff, kb, lfb, n_real)
    yb = _flash_call(qT, k, vT, n_real)

    out = _merge_ffn_call(
        h1, o_f, o_b, gs, yb, ga, gb, row(hg_out_norm[0]), wua, wub, wout,
        row(ffn2_norm[0]), w2g, w2u, w2d, n_real)
    return out.reshape(batch, n_real, D_MODEL)
```

```python
import functools

import jax
import jax.numpy as jnp
from jax import lax
from jax.experimental import pallas as pl
from jax.experimental.pallas import tpu as pltpu

F32 = jnp.float32
BF16 = jnp.bfloat16

D_MODEL = 1024
D_FF = 2816
N_META = 16
GRID_W = 64
EPS = 1e-6
HG_HEADS = 4
HG_K = 128
HG_W = HG_HEADS * HG_K
AT_HEADS = 8
AT_KV_HEADS = 2
AT_HD = 64
AT_GROUP = AT_HEADS // AT_KV_HEADS
AT_W = AT_HEADS * AT_HD
AT_KVW = AT_KV_HEADS * AT_HD
ROPE_THETA = 10000.0
IN_SIZES = (HG_W, HG_W, HG_W, HG_W, HG_W, AT_W, AT_KVW, AT_KVW, D_MODEL, D_MODEL)
IN_OFFS = tuple(sum(IN_SIZES[:i]) for i in range(len(IN_SIZES)))
D_IN = sum(IN_SIZES)

ROW_TILE = 512
FF_CHUNK = 256
CHUNK = 128
HGRN_STEP_CHUNKS = 8
Q_TILE = 256
KV_TILE = 256
V_ROWS = AT_HD + 16
LOG2E = 1.4426950408889634
Q_SCALE = AT_HD ** -0.5 * LOG2E
FLASH_UNROLL = 16
FLASH_LEAD = AT_GROUP + 1
VMEM_LIMIT = 56 * 1024 * 1024


def _dot(a, b):
    return jnp.dot(a, b, preferred_element_type=F32)


def _dot_nt(a, b):
    return lax.dot_general(a, b, (((1,), (1,)), ((), ())), preferred_element_type=F32)


def _dot_tn(a, b):
    return lax.dot_general(a, b, (((0,), (0,)), ((), ())), preferred_element_type=F32)


def _rms(x, w):
    ms = jnp.mean(x * x, axis=-1, keepdims=True)
    return x * lax.rsqrt(ms + EPS) * w


def _sigmoid(x):
    return 0.5 * jnp.tanh(0.5 * x) + 0.5


def _silu(x):
    h = 0.5 * x
    return h * jnp.tanh(h) + h


def _swiglu_half_step(h, norm_w, wg_ref, wu_ref, wd_ref):
    hn = _rms(h, norm_w).astype(BF16)
    acc = None
    pending = None
    for c in range(D_FF // FF_CHUNK + 1):
        if c < D_FF // FF_CHUNK:
            sl = slice(c * FF_CHUNK, (c + 1) * FF_CHUNK)
            g = _dot(hn, wg_ref[:, sl])
            u = _dot(hn, wu_ref[:, sl])
        if pending is not None:
            d = _dot(pending[0], wd_ref[pending[1], :])
            acc = d if acc is None else acc + d
        if c < D_FF // FF_CHUNK:
            pending = ((_silu(g) * u).astype(BF16), sl)
    return h + 0.5 * acc


CAST_STEPS = 16


def _cast_kernel(*refs):
    n = len(refs) // 2
    for src, dst in zip(refs[:n], refs[n:]):
        dst[...] = src[...].astype(BF16)


def _cast_weights(ws):
    specs = [pl.BlockSpec((w.shape[0] // CAST_STEPS, w.shape[1]), lambda i: (i, 0)) for w in ws]
    return pl.pallas_call(
        _cast_kernel,
        out_shape=[jax.ShapeDtypeStruct(w.shape, BF16) for w in ws],
        grid=(CAST_STEPS,),
        in_specs=specs,
        out_specs=specs,
        compiler_params=pltpu.CompilerParams(
            dimension_semantics=("arbitrary",), vmem_limit_bytes=VMEM_LIMIT),
        name="cast_weights",
    )(*ws)


def _ffn_kernel(x_ref, tail_ref, nw_ref, wg_ref, wu_ref, wd_ref, *rest, n_x_tiles):
    n_side = (len(rest) - 1) // 2
    o_ref = rest[n_side]
    for src, dst in zip(rest[:n_side], rest[n_side + 1:]):
        dst[...] = src[...].astype(BF16)
    h = jnp.where(pl.program_id(0) < n_x_tiles, x_ref[...], tail_ref[...])
    o_ref[...] = _swiglu_half_step(h, nw_ref[...], wg_ref, wu_ref, wd_ref)


def _const_spec(shape):
    nd = len(shape)
    return pl.BlockSpec(shape, lambda *_: (0,) * nd, pipeline_mode=pl.Buffered(1))


def _ffn_call(x, tail, nw, wg, wu, wd, side_weights):
    n_x_tiles = x.shape[0] // ROW_TILE
    rows = x.shape[0] + tail.shape[0]
    assert n_x_tiles + 1 >= CAST_STEPS
    side_specs = [pl.BlockSpec((w.shape[0] // CAST_STEPS, w.shape[1]), lambda i: (jnp.minimum(i, CAST_STEPS - 1), 0))
                  for w in side_weights]
    outs = pl.pallas_call(
        functools.partial(_ffn_kernel, n_x_tiles=n_x_tiles),
        out_shape=[jax.ShapeDtypeStruct((rows, D_MODEL), F32)]
        + [jax.ShapeDtypeStruct(w.shape, BF16) for w in side_weights],
        grid=(n_x_tiles + 1,),
        in_specs=[
            pl.BlockSpec((ROW_TILE, D_MODEL), lambda i: (jnp.minimum(i, n_x_tiles - 1), 0)),
            _const_spec((ROW_TILE, D_MODEL)),
            _const_spec((1, D_MODEL)),
            _const_spec((D_MODEL, D_FF)),
            _const_spec((D_MODEL, D_FF)),
            _const_spec((D_FF, D_MODEL)),
        ] + side_specs,
        out_specs=[pl.BlockSpec((ROW_TILE, D_MODEL), lambda i: (i, 0))] + side_specs,
        compiler_params=pltpu.CompilerParams(
            dimension_semantics=("arbitrary",), vmem_limit_bytes=VMEM_LIMIT),
        name="ffn1",
    )(x, tail, nw, wg, wu, wd, *side_weights)
    return outs[0], outs[1:]


def _lower_bound_gate(z, lbp, valid):
    m = jnp.max(lbp, axis=0, keepdims=True)
    e = jnp.exp(lbp - m)
    lb = e[0:1, :] / jnp.sum(e, axis=0, keepdims=True)
    kk = (1.0 - lb) * _sigmoid(-z)
    kk = jnp.where(valid, kk, 0.0)
    return kk, jnp.log2(1.0 - kk)


def _head_rms_rope(z, ms, w, cos, sin):
    zn = z * lax.rsqrt(ms + EPS) * w
    width = z.shape[1]
    lane = lax.broadcasted_iota(jnp.int32, z.shape, 1)
    partner = jnp.where((lane & 1) == 0, pltpu.roll(zn, width - 1, 1), pltpu.roll(zn, 1, 1))
    return zn * cos + partner * sin


def _rope_tile(by_row_ref, by_col_ref, is_tail, tail_value):
    groups = ROW_TILE // GRID_W
    rp = by_row_ref[...]
    by_row = jnp.concatenate([jnp.broadcast_to(rp[j:j + 1, :], (GRID_W, 128)) for j in range(groups)], axis=0)
    by_col = jnp.concatenate([by_col_ref[...]] * groups, axis=0)
    lane = lax.broadcasted_iota(jnp.int32, (1, 128), 1)
    t = jnp.where((lane & (AT_HD - 1)) < AT_HD // 2, by_row, by_col)
    return jnp.where(is_tail, tail_value, t)


def _mix_proj_kernel(h_ref, nw_ref, win_ref, lbf_ref, lbb_ref, qnw_ref, knw_ref, rcos_ref, rsin_ref,
                     ccos_ref, csin_ref, bdq_ref, bdk_ref,
                     hq_ref, hv_ref, kf_ref, lff_ref, kb_ref, lfb_ref, gs_ref, qT_ref, k_ref, vT_ref,
                     ga_ref, gb_ref, *, n_valid, n_x_tiles):
    i = pl.program_id(0)
    un = _rms(h_ref[...], nw_ref[...]).astype(BF16)

    def proj(piece):
        off, size = IN_OFFS[piece], IN_SIZES[piece]
        return _dot(un, win_ref[:, off:off + size])

    row = i * ROW_TILE + lax.broadcasted_iota(jnp.int32, (ROW_TILE, 1), 0)
    valid = row < n_valid

    zq = proj(5)
    zk = proj(6)
    cos = _rope_tile(rcos_ref, ccos_ref, i >= n_x_tiles, 1.0)
    sin = _rope_tile(rsin_ref, csin_ref, i >= n_x_tiles, 0.0)
    cos_q = jnp.concatenate([cos] * (AT_W // 128), axis=1)
    sin_q = jnp.concatenate([sin] * (AT_W // 128), axis=1)
    ms_q = _dot((zq * zq).astype(BF16), bdq_ref[...])
    ms_k = _dot((zk * zk).astype(BF16), bdk_ref[...])
    q = _head_rms_rope(zq, ms_q, qnw_ref[...], cos_q, sin_q) * Q_SCALE
    qT_ref[...] = jnp.transpose(q).astype(BF16)
    k_ref[...] = _head_rms_rope(zk, ms_k, knw_ref[...], cos, sin).astype(BF16)

    hq_ref[...] = _silu(proj(0))
    hv_ref[...] = proj(1).astype(BF16)
    kf, lff = _lower_bound_gate(proj(2), lbf_ref[...], valid)
    kf_ref[...] = kf
    lff_ref[...] = lff
    kb, lfb = _lower_bound_gate(proj(3), lbb_ref[...], valid)
    kb_ref[...] = kb
    lfb_ref[...] = lfb
    gs_ref[...] = _silu(proj(4))

    vT = jnp.transpose(proj(7)).astype(BF16)
    ones = jnp.ones((V_ROWS - AT_HD, ROW_TILE), BF16)
    for g in range(AT_KV_HEADS):
        vT_ref[g, 0:AT_HD, :] = vT[g * AT_HD:(g + 1) * AT_HD, :]
        vT_ref[g, AT_HD:V_ROWS, :] = ones

    ga_ref[...] = _sigmoid(proj(8))
    gb_ref[...] = _sigmoid(proj(9))


def _mix_proj_call(h1, nw, win, lbf, lbb, qnw, knw, rope, bdq, bdk, n_valid):
    rows = h1.shape[0]
    n_x_tiles = rows // ROW_TILE - 1
    groups = ROW_TILE // GRID_W
    rope_row_spec = pl.BlockSpec((groups, 128), lambda i: (jnp.minimum(i, n_x_tiles - 1), 0))
    row_spec = lambda w: pl.BlockSpec((ROW_TILE, w), lambda i: (i, 0))
    out_shape = [
        jax.ShapeDtypeStruct((rows, HG_W), F32),
        jax.ShapeDtypeStruct((rows, HG_W), BF16),
        jax.ShapeDtypeStruct((rows, HG_W), F32),
        jax.ShapeDtypeStruct((rows, HG_W), F32),
        jax.ShapeDtypeStruct((rows, HG_W), F32),
        jax.ShapeDtypeStruct((rows, HG_W), F32),
        jax.ShapeDtypeStruct((rows, HG_W), F32),
        jax.ShapeDtypeStruct((AT_W, rows), BF16),
        jax.ShapeDtypeStruct((rows, AT_KVW), BF16),
        jax.ShapeDtypeStruct((AT_KV_HEADS, V_ROWS, rows), BF16),
        jax.ShapeDtypeStruct((rows, D_MODEL), F32),
        jax.ShapeDtypeStruct((rows, D_MODEL), F32),
    ]
    out_specs = [
        row_spec(HG_W), row_spec(HG_W), row_spec(HG_W), row_spec(HG_W), row_spec(HG_W), row_spec(HG_W),
        row_spec(HG_W),
        pl.BlockSpec((AT_W, ROW_TILE), lambda i: (0, i)),
        row_spec(AT_KVW),
        pl.BlockSpec((AT_KV_HEADS, V_ROWS, ROW_TILE), lambda i: (0, 0, i)),
        row_spec(D_MODEL), row_spec(D_MODEL),
    ]
    in_specs = [
        row_spec(D_MODEL),
        _const_spec((1, D_MODEL)),
        _const_spec((D_MODEL, D_IN)),
        _const_spec(lbf.shape),
        _const_spec(lbb.shape),
        _const_spec((1, AT_W)),
        _const_spec((1, AT_KVW)),
        rope_row_spec,
        rope_row_spec,
        _const_spec((GRID_W, 128)),
        _const_spec((GRID_W, 128)),
        _const_spec((AT_W, AT_W)),
        _const_spec((AT_KVW, AT_KVW)),
    ]
    return pl.pallas_call(
        functools.partial(_mix_proj_kernel, n_valid=n_valid, n_x_tiles=n_x_tiles),
        out_shape=out_shape,
        grid=(rows // ROW_TILE,),
        in_specs=in_specs,
        out_specs=out_specs,
        compiler_params=pltpu.CompilerParams(
            dimension_semantics=("arbitrary",), vmem_limit_bytes=VMEM_LIMIT),
        name="mix_proj",
    )(h1, nw, win, lbf, lbb, qnw, knw, *rope, bdq, bdk)


def _split2(x):
    hi = x.astype(BF16)
    lo = (x - hi.astype(F32)).astype(BF16)
    return hi, lo


def _level_operand(q, k, b, lf, blk, reverse):
    c = CHUNK
    if blk >= 8:
        pieces = []
        for g in range(c // (2 * blk)):
            r0 = g * 2 * blk
            lo, hi = slice(r0, r0 + blk), slice(r0 + blk, r0 + 2 * blk)
            if reverse:
                ref = b[r0 + blk:r0 + blk + 1, :]
                pieces.append(q[lo] * jnp.exp2(b[lo] - ref))
                pieces.append(k[hi] * jnp.exp2(ref - b[hi]))
            else:
                ref = b[r0 + blk - 1:r0 + blk, :]
                pieces.append(k[lo] * jnp.exp2(ref - b[lo]))
                pieces.append(q[hi] * jnp.exp2(b[hi] - ref))
        return jnp.concatenate(pieces, axis=0).astype(BF16)

    q3, k3, b3, lf3 = (x.reshape(c // 8, 8, HG_W) for x in (q, k, b, lf))
    sub = lax.broadcasted_iota(jnp.int32, (1, 8, 1), 1)
    later = ((sub >> (blk.bit_length() - 1)) & 1) == 1
    q_role = jnp.logical_not(later) if reverse else later
    if blk == 4:
        ref = b3[:, 4:5, :] if reverse else b3[:, 3:4, :]
        gl = -jnp.abs(b3 - ref)
    elif blk == 2:
        up = pltpu.roll(lf3, 7, 1)
        dn = pltpu.roll(lf3, 1, 1)
        m4 = sub & 3
        if reverse:
            gl = jnp.where(m4 == 0, lf3 + up, jnp.where(m4 == 1, lf3, jnp.where(m4 == 2, 0.0, dn)))
        else:
            gl = jnp.where(m4 == 0, up, jnp.where(m4 == 1, 0.0, jnp.where(m4 == 2, lf3, dn + lf3)))
    else:
        odd = (sub & 1) == 1
        gl = jnp.where(odd, 0.0, lf3) if reverse else jnp.where(odd, lf3, 0.0)
    y3 = jnp.where(q_role, q3, k3) * jnp.exp2(gl)
    return y3.reshape(c, HG_W).astype(BF16)


def _hgrn_direction(q_ref, v_ref, k_ref, lf_ref, o_ref, st_ref, reverse):
    c = CHUNK
    q = q_ref[...]
    k = k_ref[...]
    v = v_ref[...]
    lf = lf_ref[...]

    r_i = lax.broadcasted_iota(jnp.int32, (c, c), 0)
    c_i = lax.broadcasted_iota(jnp.int32, (c, c), 1)
    tri = ((c_i >= r_i) if reverse else (c_i <= r_i)).astype(BF16)
    hi, lo = _split2(lf)
    b = _dot(jnp.concatenate([tri, tri], axis=1), jnp.concatenate([hi, lo], axis=0))

    levels = (64, 32, 16, 8, 4, 2, 1)
    ys = [_level_operand(q, k, b, lf, blk, reverse) for blk in levels]

    lane = lax.broadcasted_iota(jnp.int32, (1, c), 1)
    small_valid = {}
    for blk in (4, 2, 1):
        sh = blk.bit_length() - 1
        same = (r_i >> (sh + 1)) == (c_i >> (sh + 1))
        t_later = ((r_i >> sh) & 1) == 1
        s_later = ((c_i >> sh) & 1) == 1
        small_valid[blk] = (same & jnp.logical_not(t_later) & s_later) if reverse else (
            same & t_later & jnp.logical_not(s_later))

    def tile_mask(blk, tile):
        if blk < 8:
            return small_valid[blk][8 * tile:8 * tile + 8, :]
        block = (8 * tile) // blk
        if (block % 2 == 1) == reverse:
            return None
        col0 = (block + 1) * blk if reverse else (block - 1) * blk
        return (lane >= col0) & (lane < col0 + blk)

    b_last = b[0:1, :] if reverse else b[c - 1:c, :]
    q_inter = (q * jnp.exp2(b)).astype(BF16)
    k_state = (k * jnp.exp2(b_last - b)).astype(BF16)
    e_last = jnp.exp2(b_last)
    qk = q * k
    eye = r_i == c_i

    def issue(h):
        hs = slice(h * HG_K, (h + 1) * HG_K)
        products = [_dot_nt(y[:, hs], y[:, hs]) for y in ys]
        st = st_ref[h]
        inter = _dot_nt(q_inter[:, hs], st.astype(BF16))
        st_ref[h] = st * e_last[:, hs] + _dot_tn(v[:, hs], k_state[:, hs])
        return products, inter

    def finish(h, issued):
        products, inter = issued
        hs = slice(h * HG_K, (h + 1) * HG_K)
        diag = jnp.where(eye, jnp.sum(qk[:, hs], axis=1, keepdims=True), 0.0)
        tiles = []
        for t in range(c // 8):
            rows = slice(8 * t, 8 * t + 8)
            a_t = diag[rows]
            for blk, p in zip(levels, products):
                mask = tile_mask(blk, t)
                if mask is not None:
                    a_t = jnp.where(mask, p[rows], a_t)
            tiles.append(a_t)
        a = jnp.concatenate(tiles, axis=0)
        o_ref[:, hs] = _dot(a.astype(BF16), v[:, hs]) + inter

    return issue, finish


def _hgrn_meta_state(v_ref, k_ref, lf_ref, st_ref):
    c = CHUNK
    k = k_ref[...]
    v = v_ref[...]
    r_i = lax.broadcasted_iota(jnp.int32, (c, c), 0)
    c_i = lax.broadcasted_iota(jnp.int32, (c, c), 1)
    tri = (c_i <= r_i).astype(BF16)
    hi, lo = _split2(lf_ref[...])
    b = _dot(jnp.concatenate([tri, tri], axis=1), jnp.concatenate([hi, lo], axis=0))
    k_state = (k * jnp.exp2(b[c - 1:c, :] - b)).astype(BF16)
    for h in range(HG_HEADS):
        hs = slice(h * HG_K, (h + 1) * HG_K)
        st_ref[h] = _dot_tn(v[:, hs], k_state[:, hs])


def _hgrn_kernel(qf_ref, vf_ref, kf_ref, lff_ref, qb_ref, vb_ref, kb_ref, lfb_ref, vm_ref, kfm_ref, lfm_ref,
                 of_ref, ob_ref, sf_ref, sb_ref):
    @pl.when(pl.program_id(0) == 0)
    def _():
        sb_ref[...] = jnp.zeros_like(sb_ref)
        _hgrn_meta_state(vm_ref, kfm_ref, lfm_ref, sf_ref)

    def sub(ref, j):
        return ref.at[pl.ds(j * CHUNK, CHUNK)]

    units = []
    for j in range(HGRN_STEP_CHUNKS):
        jb = HGRN_STEP_CHUNKS - 1 - j
        fwd = (sub(qf_ref, j), sub(vf_ref, j), sub(kf_ref, j), sub(lff_ref, j), sub(of_ref, j), sf_ref, False)
        bwd = (sub(qb_ref, jb), sub(vb_ref, jb), sub(kb_ref, jb), sub(lfb_ref, jb), sub(ob_ref, jb), sb_ref, True)
        for args in (fwd, bwd):
            issue, finish = _hgrn_direction(*args)
            units += [(issue, finish, h) for h in range(HG_HEADS)]

    pending = None
    for issue, finish, h in units:
        issued = issue(h)
        if pending is not None:
            pending[0](pending[1], pending[2])
        pending = (finish, h, issued)
    pending[0](pending[1], pending[2])


def _hgrn_call(hq, hv, kf, lff, kb, lfb, n_real):
    nb = n_real // CHUNK
    steps = nb // HGRN_STEP_CHUNKS
    fwd = lambda s: (s, 0)
    bwd = lambda s: (steps - 1 - s, 0)
    spec = lambda m: pl.BlockSpec((HGRN_STEP_CHUNKS * CHUNK, HG_W), m)
    meta = pl.BlockSpec((CHUNK, HG_W), lambda s: (nb, 0))
    out = jax.ShapeDtypeStruct((n_real, HG_W), F32)
    return pl.pallas_call(
        _hgrn_kernel,
        out_shape=[out, out],
        grid=(steps,),
        in_specs=[spec(fwd), spec(fwd), spec(fwd), spec(fwd), spec(bwd), spec(bwd), spec(bwd), spec(bwd),
                  meta, meta, meta],
        out_specs=[spec(fwd), spec(bwd)],
        scratch_shapes=[pltpu.VMEM((HG_HEADS, HG_K, HG_K), F32), pltpu.VMEM((HG_HEADS, HG_K, HG_K), F32)],
        compiler_params=pltpu.CompilerParams(
            dimension_semantics=("arbitrary",), vmem_limit_bytes=VMEM_LIMIT),
        name="hgrn",
    )(hq, hv, kf, lff, hq, hv, kb, lfb, hv, kf, lff)


def _flash_kernel(qT_ref, k_ref, vT_ref, kt_ref, vTt_ref, o_ref, qp_ref, m_ref, acc_ref, s_ref, mc_ref,
                  st_ref, mct_ref, *,
                  n_kv, n_tail_valid):
    g = pl.program_id(0)
    tq = qT_ref.shape[1]

    half = lax.broadcasted_iota(jnp.int32, (AT_KVW, tq), 0) >> 6
    for h in range(AT_GROUP):
        qh = qT_ref[h * AT_HD:(h + 1) * AT_HD, :].astype(F32)
        q2 = jnp.concatenate([qh, qh], axis=0)
        qp_ref[h] = jnp.where(half == g, q2, 0.0).astype(BF16)

    m_ref[...] = jnp.full(m_ref.shape, -jnp.inf, F32)
    acc_ref[...] = jnp.zeros(acc_ref.shape, F32)

    def store_scores(sT, s_dst, mc_dst, h):
        s_dst[h] = sT
        mc_dst[h] = jnp.max(sT, axis=0, keepdims=True)

    def unit_scores(chunk, h, slot):
        kc = k_ref[pl.ds(pl.multiple_of(chunk * KV_TILE, KV_TILE), KV_TILE), :]
        store_scores(_dot(kc, qp_ref[h]), s_ref.at[slot], mc_ref.at[slot], h)

    def unit_consume(vc, s_src, mc_src, h):
        m_prev = m_ref[h]
        m_new = jnp.maximum(m_prev, mc_src[h])
        alpha = jnp.exp2(m_prev - m_new)
        pT = jnp.exp2(s_src[h] - m_new).astype(BF16)
        acc_ref[h] = alpha * acc_ref[h] + _dot(vc, pT)
        m_ref[h] = m_new

    krow = lax.broadcasted_iota(jnp.int32, (128, 1), 0)
    for h in range(AT_GROUP):
        sT = jnp.where(krow < n_tail_valid, _dot(kt_ref[...], qp_ref[h]), -jnp.inf)
        store_scores(sT, st_ref, mct_ref, h)
    for u in range(FLASH_LEAD):
        unit_scores(u // AT_GROUP, u % AT_GROUP, (u // AT_GROUP) % 2)
    for h in range(AT_GROUP):
        unit_consume(vTt_ref[...], st_ref, mct_ref, h)

    def body(i, carry):
        for j in range(2 * AT_GROUP):
            uq = j + FLASH_LEAD
            cq = jnp.minimum(2 * i + uq // AT_GROUP, n_kv - 1)
            unit_scores(cq, uq % AT_GROUP, (uq // AT_GROUP) % 2)
            cp = 2 * i + j // AT_GROUP
            vc = vT_ref[:, pl.ds(pl.multiple_of(cp * KV_TILE, KV_TILE), KV_TILE)]
            unit_consume(vc, s_ref.at[(j // AT_GROUP) % 2], mc_ref.at[(j // AT_GROUP) % 2], j % AT_GROUP)
        return carry

    lax.fori_loop(0, n_kv // 2, body, 0, unroll=FLASH_UNROLL)

    outs = []
    for h in range(AT_GROUP):
        acc = acc_ref[h]
        outs.append(acc[0:AT_HD, :] / acc[AT_HD:AT_HD + 1, :])
    o_ref[...] = jnp.transpose(jnp.concatenate(outs, axis=0)).astype(BF16)


def _flash_call(qT, k, vT, n_real):
    n_kv = n_real // KV_TILE
    tail_blk = n_real // 128
    gw = AT_GROUP * AT_HD
    return pl.pallas_call(
        functools.partial(_flash_kernel, n_kv=n_kv, n_tail_valid=N_META),
        out_shape=jax.ShapeDtypeStruct((n_real, AT_W), BF16),
        grid=(AT_KV_HEADS, n_real // Q_TILE),
        in_specs=[
            pl.BlockSpec((gw, Q_TILE), lambda g, i: (g, i)),
            pl.BlockSpec((n_real, AT_KVW), lambda g, i: (0, 0)),
            pl.BlockSpec((None, V_ROWS, n_real), lambda g, i: (g, 0, 0)),
            pl.BlockSpec((128, AT_KVW), lambda g, i: (tail_blk, 0)),
            pl.BlockSpec((None, V_ROWS, 128), lambda g, i: (g, 0, tail_blk)),
        ],
        out_specs=pl.BlockSpec((Q_TILE, gw), lambda g, i: (i, g)),
        scratch_shapes=[
            pltpu.VMEM((AT_GROUP, AT_KVW, Q_TILE), BF16),
            pltpu.VMEM((AT_GROUP, 1, Q_TILE), F32),
            pltpu.VMEM((AT_GROUP, V_ROWS, Q_TILE), F32),
            pltpu.VMEM((2, AT_GROUP, KV_TILE, Q_TILE), F32),
            pltpu.VMEM((2, AT_GROUP, 1, Q_TILE), F32),
            pltpu.VMEM((AT_GROUP, 128, Q_TILE), F32),
            pltpu.VMEM((AT_GROUP, 1, Q_TILE), F32),
        ],
        compiler_params=pltpu.CompilerParams(
            dimension_semantics=("arbitrary", "arbitrary"), vmem_limit_bytes=VMEM_LIMIT),
        name="flash",
    )(qT, k, vT, k, vT)


def _merge_ffn_kernel(h1_ref, of_ref, ob_ref, gs_ref, yb_ref, ga_ref, gb_ref, hgw_ref,
                      wua_ref, wub_ref, wout_ref, nw_ref, wg_ref, wu_ref, wd_ref, o_ref):
    o = of_ref[...] + ob_ref[...]
    normed = []
    for h in range(HG_HEADS):
        oh = o[:, h * HG_K:(h + 1) * HG_K]
        normed.append(oh * lax.rsqrt(jnp.mean(oh * oh, axis=-1, keepdims=True) + EPS))
    ya = (jnp.concatenate(normed, axis=1) * hgw_ref[...] * gs_ref[...]).astype(BF16)
    mixed = ga_ref[...] * _dot(ya, wua_ref[...]) + gb_ref[...] * _dot(yb_ref[...], wub_ref[...])
    h2 = h1_ref[...] + _dot(mixed.astype(BF16), wout_ref[...])
    o_ref[...] = _swiglu_half_step(h2, nw_ref[...], wg_ref, wu_ref, wd_ref)


def _merge_ffn_call(h1, o_f, o_b, gs, yb, ga, gb, hgw, wua, wub, wout, nw, wg, wu, wd, n_real):
    row_spec = lambda w: pl.BlockSpec((ROW_TILE, w), lambda i: (i, 0))
    return pl.pallas_call(
        _merge_ffn_kernel,
        out_shape=jax.ShapeDtypeStruct((n_real, D_MODEL), F32),
        grid=(n_real // ROW_TILE,),
        in_specs=[
            row_spec(D_MODEL), row_spec(HG_W), row_spec(HG_W), row_spec(HG_W), row_spec(AT_W),
            row_spec(D_MODEL), row_spec(D_MODEL),
            _const_spec((1, HG_W)),
            _const_spec((HG_W, D_MODEL)),
            _const_spec((AT_W, D_MODEL)),
            _const_spec((D_MODEL, D_MODEL)),
            _const_spec((1, D_MODEL)),
            _const_spec((D_MODEL, D_FF)),
            _const_spec((D_MODEL, D_FF)),
            _const_spec((D_FF, D_MODEL)),
        ],
        out_specs=row_spec(D_MODEL),
        compiler_params=pltpu.CompilerParams(
            dimension_semantics=("arbitrary",), vmem_limit_bytes=VMEM_LIMIT),
        name="merge_ffn2",
    )(h1, o_f, o_b, gs, yb, ga, gb, hgw, wua, wub, wout, nw, wg, wu, wd)


def _rope_tables(n_real):
    half = AT_HD // 2
    inv = ROPE_THETA ** (-jnp.arange(0, half, 2, dtype=F32) / half)
    sign = jnp.tile(jnp.array([-1.0, 1.0], F32), half)

    def lanes(ang, first_half):
        cos, sin = jnp.cos(ang), jnp.sin(ang)
        one, zero = jnp.ones_like(cos), jnp.zeros_like(sin)
        cos = jnp.concatenate([cos, one] if first_half else [one, cos], axis=-1)
        sin = jnp.concatenate([sin, zero] if first_half else [zero, sin], axis=-1)
        cos = jnp.repeat(cos, 2, axis=-1)
        sin = jnp.repeat(sin, 2, axis=-1) * sign
        reps = 128 // AT_HD
        return jnp.tile(cos, (1, reps)), jnp.tile(sin, (1, reps))

    r = jnp.arange(n_real // GRID_W, dtype=F32)
    c = jnp.arange(GRID_W, dtype=F32)
    return lanes(r[:, None] * inv, True) + lanes(c[:, None] * inv, False)


def _head_mean_matrix(width):
    heads = width // AT_HD
    return jnp.kron(jnp.eye(heads, dtype=F32), jnp.full((AT_HD, AT_HD), 1.0 / AT_HD, F32)).astype(BF16)


def kernel(x, meta_tokens, ffn1_norm, ffn1_w_gate, ffn1_w_up, ffn1_w_down, mix_norm, w_in, hg_lb_fwd, hg_lb_bwd, hg_out_norm, q_norm, k_norm, w_up_a, w_up_b, w_out, ffn2_norm, ffn2_w_gate, ffn2_w_up, ffn2_w_down):
    batch, n_real, _ = x.shape
    assert batch == 1 and n_real % ROW_TILE == 0 and n_real % GRID_W == 0
    rows = n_real + ROW_TILE
    n_valid = n_real + N_META

    tail = jnp.concatenate(
        [meta_tokens.astype(x.dtype), jnp.zeros((ROW_TILE - N_META, D_MODEL), x.dtype)], axis=0)
    row = lambda w: w.reshape(1, -1).astype(F32)
    f32 = lambda ws: [w[0].astype(F32) for w in ws]
    w1g, w1u, w1d = _cast_weights(f32((ffn1_w_gate, ffn1_w_up, ffn1_w_down)))
    h1, (win, wua, wub, wout, w2g, w2u, w2d) = _ffn_call(
        x[0], tail, row(ffn1_norm[0]), w1g, w1u, w1d,
        f32((w_in, w_up_a, w_up_b, w_out, ffn2_w_gate, ffn2_w_up, ffn2_w_down)))

    rope = _rope_tables(n_real)
    hq, hv, kf, lff, kb, lfb, gs, qT, k, vT, ga, gb = _mix_proj_call(
        h1, row(mix_norm[0]), win, hg_lb_fwd.astype(F32), hg_lb_bwd.astype(F32),
        row(jnp.tile(q_norm[0], AT_HEADS)), row(jnp.tile(k_norm[0], AT_KV_HEADS)), rope,
        _head_mean_matrix(AT_W), _head_mean_matrix(AT_KVW), n_valid)

    o_f, o_b = _hgrn_call(hq, hv, kf, lff, kb, lfb, n_real)
    yb = _flash_call(qT, k, vT, n_real)

    out = _merge_ffn_call(
        h1, o_f, o_b, gs, yb, ga, gb, row(hg_out_norm[0]), wua, wub, wout,
        row(ffn2_norm[0]), w2g, w2u, w2d, n_real)
    return out.reshape(batch, n_real, D_MODEL)
```

```python
import functools

import jax
import jax.numpy as jnp
from jax import lax
from jax.experimental import pallas as pl
from jax.experimental.pallas import tpu as pltpu

F32 = jnp.float32
BF16 = jnp.bfloat16

D_MODEL = 1024
D_FF = 2816
N_META = 16
GRID_W = 64
EPS = 1e-6
HG_HEADS = 4
HG_K = 128
HG_W = HG_HEADS * HG_K
AT_HEADS = 8
AT_KV_HEADS = 2
AT_HD = 64
AT_GROUP = AT_HEADS // AT_KV_HEADS
AT_W = AT_HEADS * AT_HD
AT_KVW = AT_KV_HEADS * AT_HD
ROPE_THETA = 10000.0
IN_SIZES = (HG_W, HG_W, HG_W, HG_W, HG_W, AT_W, AT_KVW, AT_KVW, D_MODEL, D_MODEL)
IN_OFFS = tuple(sum(IN_SIZES[:i]) for i in range(len(IN_SIZES)))
D_IN = sum(IN_SIZES)

ROW_TILE = 512
FF_CHUNK = 256
CHUNK = 128
HGRN_STEP_CHUNKS = 8
Q_TILE = 256
KV_TILE = 256
V_ROWS = AT_HD + 16
LOG2E = 1.4426950408889634
Q_SCALE = AT_HD ** -0.5 * LOG2E
FLASH_UNROLL = 16
FLASH_LEAD = AT_GROUP + 1
VMEM_LIMIT = 56 * 1024 * 1024


def _dot(a, b):
    return jnp.dot(a, b, preferred_element_type=F32)


def _dot_nt(a, b):
    return lax.dot_general(a, b, (((1,), (1,)), ((), ())), preferred_element_type=F32)


def _dot_tn(a, b):
    return lax.dot_general(a, b, (((0,), (0,)), ((), ())), preferred_element_type=F32)


def _rms(x, w):
    ms = jnp.mean(x * x, axis=-1, keepdims=True)
    return x * lax.rsqrt(ms + EPS) * w


def _sigmoid(x):
    return 0.5 * jnp.tanh(0.5 * x) + 0.5


def _silu(x):
    h = 0.5 * x
    return h * jnp.tanh(h) + h


def _swiglu_half_step(h, norm_w, wg_ref, wu_ref, wd_ref):
    hn = _rms(h, norm_w).astype(BF16)
    acc = None
    pending = None
    for c in range(D_FF // FF_CHUNK + 1):
        if c < D_FF // FF_CHUNK:
            sl = slice(c * FF_CHUNK, (c + 1) * FF_CHUNK)
            g = _dot(hn, wg_ref[:, sl])
            u = _dot(hn, wu_ref[:, sl])
        if pending is not None:
            d = _dot(pending[0], wd_ref[pending[1], :])
            acc = d if acc is None else acc + d
        if c < D_FF // FF_CHUNK:
            pending = ((_silu(g) * u).astype(BF16), sl)
    return h + 0.5 * acc


CAST_STEPS = 16


def _cast_kernel(*refs):
    n = len(refs) // 2
    for src, dst in zip(refs[:n], refs[n:]):
        dst[...] = src[...].astype(BF16)


def _cast_weights(ws):
    specs = [pl.BlockSpec((w.shape[0] // CAST_STEPS, w.shape[1]), lambda i: (i, 0)) for w in ws]
    return pl.pallas_call(
        _cast_kernel,
        out_shape=[jax.ShapeDtypeStruct(w.shape, BF16) for w in ws],
        grid=(CAST_STEPS,),
        in_specs=specs,
        out_specs=specs,
        compiler_params=pltpu.CompilerParams(
            dimension_semantics=("arbitrary",), vmem_limit_bytes=VMEM_LIMIT),
        name="cast_weights",
    )(*ws)


def _ffn_kernel(x_ref, tail_ref, nw_ref, wg_ref, wu_ref, wd_ref, o_ref, *, n_x_tiles):
    h = jnp.where(pl.program_id(0) < n_x_tiles, x_ref[...], tail_ref[...])
    o_ref[...] = _swiglu_half_step(h, nw_ref[...], wg_ref, wu_ref, wd_ref)


def _const_spec(shape):
    nd = len(shape)
    return pl.BlockSpec(shape, lambda *_: (0,) * nd, pipeline_mode=pl.Buffered(1))


def _ffn_call(x, tail, nw, wg, wu, wd):
    n_x_tiles = x.shape[0] // ROW_TILE
    rows = x.shape[0] + tail.shape[0]
    return pl.pallas_call(
        functools.partial(_ffn_kernel, n_x_tiles=n_x_tiles),
        out_shape=jax.ShapeDtypeStruct((rows, D_MODEL), F32),
        grid=(n_x_tiles + 1,),
        in_specs=[
            pl.BlockSpec((ROW_TILE, D_MODEL), lambda i: (jnp.minimum(i, n_x_tiles - 1), 0)),
            _const_spec((ROW_TILE, D_MODEL)),
            _const_spec((1, D_MODEL)),
            _const_spec((D_MODEL, D_FF)),
            _const_spec((D_MODEL, D_FF)),
            _const_spec((D_FF, D_MODEL)),
        ],
        out_specs=pl.BlockSpec((ROW_TILE, D_MODEL), lambda i: (i, 0)),
        compiler_params=pltpu.CompilerParams(
            dimension_semantics=("arbitrary",), vmem_limit_bytes=VMEM_LIMIT),
        name="ffn1",
    )(x, tail, nw, wg, wu, wd)


def _lower_bound_gate(z, lbp, valid):
    m = jnp.max(lbp, axis=0, keepdims=True)
    e = jnp.exp(lbp - m)
    lb = e[0:1, :] / jnp.sum(e, axis=0, keepdims=True)
    kk = (1.0 - lb) * _sigmoid(-z)
    kk = jnp.where(valid, kk, 0.0)
    return kk, jnp.log2(1.0 - kk)


def _head_rms_rope(z, ms, w, cos, sin):
    zn = z * lax.rsqrt(ms + EPS) * w
    width = z.shape[1]
    lane = lax.broadcasted_iota(jnp.int32, z.shape, 1)
    partner = jnp.where((lane & 1) == 0, pltpu.roll(zn, width - 1, 1), pltpu.roll(zn, 1, 1))
    return zn * cos + partner * sin


def _rope_tile(by_row_ref, by_col_ref, is_tail, tail_value):
    groups = ROW_TILE // GRID_W
    rp = by_row_ref[...]
    by_row = jnp.concatenate([jnp.broadcast_to(rp[j:j + 1, :], (GRID_W, 128)) for j in range(groups)], axis=0)
    by_col = jnp.concatenate([by_col_ref[...]] * groups, axis=0)
    lane = lax.broadcasted_iota(jnp.int32, (1, 128), 1)
    t = jnp.where((lane & (AT_HD - 1)) < AT_HD // 2, by_row, by_col)
    return jnp.where(is_tail, tail_value, t)


def _mix_proj_kernel(h_ref, nw_ref, win_ref, lbf_ref, lbb_ref, qnw_ref, knw_ref, rcos_ref, rsin_ref,
                     ccos_ref, csin_ref, bdq_ref, bdk_ref,
                     hq_ref, hv_ref, kf_ref, lff_ref, kb_ref, lfb_ref, gs_ref, qT_ref, k_ref, vT_ref,
                     ga_ref, gb_ref, *, n_valid, n_x_tiles):
    i = pl.program_id(0)
    un = _rms(h_ref[...], nw_ref[...]).astype(BF16)

    def proj(piece):
        off, size = IN_OFFS[piece], IN_SIZES[piece]
        return _dot(un, win_ref[:, off:off + size])

    row = i * ROW_TILE + lax.broadcasted_iota(jnp.int32, (ROW_TILE, 1), 0)
    valid = row < n_valid

    zq = proj(5)
    zk = proj(6)
    cos = _rope_tile(rcos_ref, ccos_ref, i >= n_x_tiles, 1.0)
    sin = _rope_tile(rsin_ref, csin_ref, i >= n_x_tiles, 0.0)
    cos_q = jnp.concatenate([cos] * (AT_W // 128), axis=1)
    sin_q = jnp.concatenate([sin] * (AT_W // 128), axis=1)
    ms_q = _dot((zq * zq).astype(BF16), bdq_ref[...])
    ms_k = _dot((zk * zk).astype(BF16), bdk_ref[...])
    q = _head_rms_rope(zq, ms_q, qnw_ref[...], cos_q, sin_q) * Q_SCALE
    qT_ref[...] = jnp.transpose(q).astype(BF16)
    k_ref[...] = _head_rms_rope(zk, ms_k, knw_ref[...], cos, sin).astype(BF16)

    hq_ref[...] = _silu(proj(0))
    hv_ref[...] = proj(1).astype(BF16)
    kf, lff = _lower_bound_gate(proj(2), lbf_ref[...], valid)
    kf_ref[...] = kf
    lff_ref[...] = lff
    kb, lfb = _lower_bound_gate(proj(3), lbb_ref[...], valid)
    kb_ref[...] = kb
    lfb_ref[...] = lfb
    gs_ref[...] = _silu(proj(4))

    vT = jnp.transpose(proj(7)).astype(BF16)
    ones = jnp.ones((V_ROWS - AT_HD, ROW_TILE), BF16)
    for g in range(AT_KV_HEADS):
        vT_ref[g, 0:AT_HD, :] = vT[g * AT_HD:(g + 1) * AT_HD, :]
        vT_ref[g, AT_HD:V_ROWS, :] = ones

    ga_ref[...] = _sigmoid(proj(8))
    gb_ref[...] = _sigmoid(proj(9))


def _mix_proj_call(h1, nw, win, lbf, lbb, qnw, knw, rope, bdq, bdk, n_valid):
    rows = h1.shape[0]
    n_x_tiles = rows // ROW_TILE - 1
    groups = ROW_TILE // GRID_W
    rope_row_spec = pl.BlockSpec((groups, 128), lambda i: (jnp.minimum(i, n_x_tiles - 1), 0))
    row_spec = lambda w: pl.BlockSpec((ROW_TILE, w), lambda i: (i, 0))
    out_shape = [
        jax.ShapeDtypeStruct((rows, HG_W), F32),
        jax.ShapeDtypeStruct((rows, HG_W), BF16),
        jax.ShapeDtypeStruct((rows, HG_W), F32),
        jax.ShapeDtypeStruct((rows, HG_W), F32),
        jax.ShapeDtypeStruct((rows, HG_W), F32),
        jax.ShapeDtypeStruct((rows, HG_W), F32),
        jax.ShapeDtypeStruct((rows, HG_W), F32),
        jax.ShapeDtypeStruct((AT_W, rows), BF16),
        jax.ShapeDtypeStruct((rows, AT_KVW), BF16),
        jax.ShapeDtypeStruct((AT_KV_HEADS, V_ROWS, rows), BF16),
        jax.ShapeDtypeStruct((rows, D_MODEL), F32),
        jax.ShapeDtypeStruct((rows, D_MODEL), F32),
    ]
    out_specs = [
        row_spec(HG_W), row_spec(HG_W), row_spec(HG_W), row_spec(HG_W), row_spec(HG_W), row_spec(HG_W),
        row_spec(HG_W),
        pl.BlockSpec((AT_W, ROW_TILE), lambda i: (0, i)),
        row_spec(AT_KVW),
        pl.BlockSpec((AT_KV_HEADS, V_ROWS, ROW_TILE), lambda i: (0, 0, i)),
        row_spec(D_MODEL), row_spec(D_MODEL),
    ]
    in_specs = [
        row_spec(D_MODEL),
        _const_spec((1, D_MODEL)),
        _const_spec((D_MODEL, D_IN)),
        _const_spec(lbf.shape),
        _const_spec(lbb.shape),
        _const_spec((1, AT_W)),
        _const_spec((1, AT_KVW)),
        rope_row_spec,
        rope_row_spec,
        _const_spec((GRID_W, 128)),
        _const_spec((GRID_W, 128)),
        _const_spec((AT_W, AT_W)),
        _const_spec((AT_KVW, AT_KVW)),
    ]
    return pl.pallas_call(
        functools.partial(_mix_proj_kernel, n_valid=n_valid, n_x_tiles=n_x_tiles),
        out_shape=out_shape,
        grid=(rows // ROW_TILE,),
        in_specs=in_specs,
        out_specs=out_specs,
        compiler_params=pltpu.CompilerParams(
            dimension_semantics=("arbitrary",), vmem_limit_bytes=VMEM_LIMIT),
        name="mix_proj",
    )(h1, nw, win, lbf, lbb, qnw, knw, *rope, bdq, bdk)


def _split2(x):
    hi = x.astype(BF16)
    lo = (x - hi.astype(F32)).astype(BF16)
    return hi, lo


def _level_operand(q, k, b, lf, blk, reverse):
    c = CHUNK
    if blk >= 8:
        pieces = []
        for g in range(c // (2 * blk)):
            r0 = g * 2 * blk
            lo, hi = slice(r0, r0 + blk), slice(r0 + blk, r0 + 2 * blk)
            if reverse:
                ref = b[r0 + blk:r0 + blk + 1, :]
                pieces.append(q[lo] * jnp.exp2(b[lo] - ref))
                pieces.append(k[hi] * jnp.exp2(ref - b[hi]))
            else:
                ref = b[r0 + blk - 1:r0 + blk, :]
                pieces.append(k[lo] * jnp.exp2(ref - b[lo]))
                pieces.append(q[hi] * jnp.exp2(b[hi] - ref))
        return jnp.concatenate(pieces, axis=0).astype(BF16)

    q3, k3, b3, lf3 = (x.reshape(c // 8, 8, HG_W) for x in (q, k, b, lf))
    sub = lax.broadcasted_iota(jnp.int32, (1, 8, 1), 1)
    later = ((sub >> (blk.bit_length() - 1)) & 1) == 1
    q_role = jnp.logical_not(later) if reverse else later
    if blk == 4:
        ref = b3[:, 4:5, :] if reverse else b3[:, 3:4, :]
        gl = -jnp.abs(b3 - ref)
    elif blk == 2:
        up = pltpu.roll(lf3, 7, 1)
        dn = pltpu.roll(lf3, 1, 1)
        m4 = sub & 3
        if reverse:
            gl = jnp.where(m4 == 0, lf3 + up, jnp.where(m4 == 1, lf3, jnp.where(m4 == 2, 0.0, dn)))
        else:
            gl = jnp.where(m4 == 0, up, jnp.where(m4 == 1, 0.0, jnp.where(m4 == 2, lf3, dn + lf3)))
    else:
        odd = (sub & 1) == 1
        gl = jnp.where(odd, 0.0, lf3) if reverse else jnp.where(odd, lf3, 0.0)
    y3 = jnp.where(q_role, q3, k3) * jnp.exp2(gl)
    return y3.reshape(c, HG_W).astype(BF16)


def _hgrn_direction(q_ref, v_ref, k_ref, lf_ref, o_ref, st_ref, reverse):
    c = CHUNK
    q = q_ref[...]
    k = k_ref[...]
    v = v_ref[...]
    lf = lf_ref[...]

    r_i = lax.broadcasted_iota(jnp.int32, (c, c), 0)
    c_i = lax.broadcasted_iota(jnp.int32, (c, c), 1)
    tri = ((c_i >= r_i) if reverse else (c_i <= r_i)).astype(BF16)
    hi, lo = _split2(lf)
    b = _dot(jnp.concatenate([tri, tri], axis=1), jnp.concatenate([hi, lo], axis=0))

    levels = (64, 32, 16, 8, 4, 2, 1)
    ys = [_level_operand(q, k, b, lf, blk, reverse) for blk in levels]

    lane = lax.broadcasted_iota(jnp.int32, (1, c), 1)
    small_valid = {}
    for blk in (4, 2, 1):
        sh = blk.bit_length() - 1
        same = (r_i >> (sh + 1)) == (c_i >> (sh + 1))
        t_later = ((r_i >> sh) & 1) == 1
        s_later = ((c_i >> sh) & 1) == 1
        small_valid[blk] = (same & jnp.logical_not(t_later) & s_later) if reverse else (
            same & t_later & jnp.logical_not(s_later))

    def tile_mask(blk, tile):
        if blk < 8:
            return small_valid[blk][8 * tile:8 * tile + 8, :]
        block = (8 * tile) // blk
        if (block % 2 == 1) == reverse:
            return None
        col0 = (block + 1) * blk if reverse else (block - 1) * blk
        return (lane >= col0) & (lane < col0 + blk)

    b_last = b[0:1, :] if reverse else b[c - 1:c, :]
    q_inter = (q * jnp.exp2(b)).astype(BF16)
    k_state = (k * jnp.exp2(b_last - b)).astype(BF16)
    e_last = jnp.exp2(b_last)
    qk = q * k
    eye = r_i == c_i

    def issue(h):
        hs = slice(h * HG_K, (h + 1) * HG_K)
        products = [_dot_nt(y[:, hs], y[:, hs]) for y in ys]
        st = st_ref[h]
        inter = _dot_nt(q_inter[:, hs], st.astype(BF16))
        st_ref[h] = st * e_last[:, hs] + _dot_tn(v[:, hs], k_state[:, hs])
        return products, inter

    def finish(h, issued):
        products, inter = issued
        hs = slice(h * HG_K, (h + 1) * HG_K)
        diag = jnp.where(eye, jnp.sum(qk[:, hs], axis=1, keepdims=True), 0.0)
        tiles = []
        for t in range(c // 8):
            rows = slice(8 * t, 8 * t + 8)
            a_t = diag[rows]
            for blk, p in zip(levels, products):
                mask = tile_mask(blk, t)
                if mask is not None:
                    a_t = jnp.where(mask, p[rows], a_t)
            tiles.append(a_t)
        a = jnp.concatenate(tiles, axis=0)
        o_ref[:, hs] = _dot(a.astype(BF16), v[:, hs]) + inter

    return issue, finish


def _hgrn_meta_state(v_ref, k_ref, lf_ref, st_ref):
    c = CHUNK
    k = k_ref[...]
    v = v_ref[...]
    r_i = lax.broadcasted_iota(jnp.int32, (c, c), 0)
    c_i = lax.broadcasted_iota(jnp.int32, (c, c), 1)
    tri = (c_i <= r_i).astype(BF16)
    hi, lo = _split2(lf_ref[...])
    b = _dot(jnp.concatenate([tri, tri], axis=1), jnp.concatenate([hi, lo], axis=0))
    k_state = (k * jnp.exp2(b[c - 1:c, :] - b)).astype(BF16)
    for h in range(HG_HEADS):
        hs = slice(h * HG_K, (h + 1) * HG_K)
        st_ref[h] = _dot_tn(v[:, hs], k_state[:, hs])


def _hgrn_kernel(qf_ref, vf_ref, kf_ref, lff_ref, qb_ref, vb_ref, kb_ref, lfb_ref, vm_ref, kfm_ref, lfm_ref,
                 of_ref, ob_ref, sf_ref, sb_ref):
    @pl.when(pl.program_id(0) == 0)
    def _():
        sb_ref[...] = jnp.zeros_like(sb_ref)
        _hgrn_meta_state(vm_ref, kfm_ref, lfm_ref, sf_ref)

    def sub(ref, j):
        return ref.at[pl.ds(j * CHUNK, CHUNK)]

    units = []
    for j in range(HGRN_STEP_CHUNKS):
        jb = HGRN_STEP_CHUNKS - 1 - j
        fwd = (sub(qf_ref, j), sub(vf_ref, j), sub(kf_ref, j), sub(lff_ref, j), sub(of_ref, j), sf_ref, False)
        bwd = (sub(qb_ref, jb), sub(vb_ref, jb), sub(kb_ref, jb), sub(lfb_ref, jb), sub(ob_ref, jb), sb_ref, True)
        for args in (fwd, bwd):
            issue, finish = _hgrn_direction(*args)
            units += [(issue, finish, h) for h in range(HG_HEADS)]

    pending = None
    for issue, finish, h in units:
        issued = issue(h)
        if pending is not None:
            pending[0](pending[1], pending[2])
        pending = (finish, h, issued)
    pending[0](pending[1], pending[2])


def _hgrn_call(hq, hv, kf, lff, kb, lfb, n_real):
    nb = n_real // CHUNK
    steps = nb // HGRN_STEP_CHUNKS
    fwd = lambda s: (s, 0)
    bwd = lambda s: (steps - 1 - s, 0)
    spec = lambda m: pl.BlockSpec((HGRN_STEP_CHUNKS * CHUNK, HG_W), m)
    meta = pl.BlockSpec((CHUNK, HG_W), lambda s: (nb, 0))
    out = jax.ShapeDtypeStruct((n_real, HG_W), F32)
    return pl.pallas_call(
        _hgrn_kernel,
        out_shape=[out, out],
        grid=(steps,),
        in_specs=[spec(fwd), spec(fwd), spec(fwd), spec(fwd), spec(bwd), spec(bwd), spec(bwd), spec(bwd),
                  meta, meta, meta],
        out_specs=[spec(fwd), spec(bwd)],
        scratch_shapes=[pltpu.VMEM((HG_HEADS, HG_K, HG_K), F32), pltpu.VMEM((HG_HEADS, HG_K, HG_K), F32)],
        compiler_params=pltpu.CompilerParams(
            dimension_semantics=("arbitrary",), vmem_limit_bytes=VMEM_LIMIT),
        name="hgrn",
    )(hq, hv, kf, lff, hq, hv, kb, lfb, hv, kf, lff)


def _flash_kernel(qT_ref, k_ref, vT_ref, kt_ref, vTt_ref, o_ref, qp_ref, m_ref, acc_ref, s_ref, mc_ref,
                  st_ref, mct_ref, *,
                  n_kv, n_tail_valid):
    g = pl.program_id(0)
    tq = qT_ref.shape[1]

    half = lax.broadcasted_iota(jnp.int32, (AT_KVW, tq), 0) >> 6
    for h in range(AT_GROUP):
        qh = qT_ref[h * AT_HD:(h + 1) * AT_HD, :].astype(F32)
        q2 = jnp.concatenate([qh, qh], axis=0)
        qp_ref[h] = jnp.where(half == g, q2, 0.0).astype(BF16)

    m_ref[...] = jnp.full(m_ref.shape, -jnp.inf, F32)
    acc_ref[...] = jnp.zeros(acc_ref.shape, F32)

    def store_scores(sT, s_dst, mc_dst, h):
        s_dst[h] = sT
        mc_dst[h] = jnp.max(sT, axis=0, keepdims=True)

    def unit_scores(chunk, h, slot):
        kc = k_ref[pl.ds(pl.multiple_of(chunk * KV_TILE, KV_TILE), KV_TILE), :]
        store_scores(_dot(kc, qp_ref[h]), s_ref.at[slot], mc_ref.at[slot], h)

    def unit_consume(vc, s_src, mc_src, h):
        m_prev = m_ref[h]
        m_new = jnp.maximum(m_prev, mc_src[h])
        alpha = jnp.exp2(m_prev - m_new)
        pT = jnp.exp2(s_src[h] - m_new).astype(BF16)
        acc_ref[h] = alpha * acc_ref[h] + _dot(vc, pT)
        m_ref[h] = m_new

    krow = lax.broadcasted_iota(jnp.int32, (128, 1), 0)
    for h in range(AT_GROUP):
        sT = jnp.where(krow < n_tail_valid, _dot(kt_ref[...], qp_ref[h]), -jnp.inf)
        store_scores(sT, st_ref, mct_ref, h)
    for u in range(FLASH_LEAD):
        unit_scores(u // AT_GROUP, u % AT_GROUP, (u // AT_GROUP) % 2)
    for h in range(AT_GROUP):
        unit_consume(vTt_ref[...], st_ref, mct_ref, h)

    def body(i, carry):
        for j in range(2 * AT_GROUP):
            uq = j + FLASH_LEAD
            cq = jnp.minimum(2 * i + uq // AT_GROUP, n_kv - 1)
            unit_scores(cq, uq % AT_GROUP, (uq // AT_GROUP) % 2)
            cp = 2 * i + j // AT_GROUP
            vc = vT_ref[:, pl.ds(pl.multiple_of(cp * KV_TILE, KV_TILE), KV_TILE)]
            unit_consume(vc, s_ref.at[(j // AT_GROUP) % 2], mc_ref.at[(j // AT_GROUP) % 2], j % AT_GROUP)
        return carry

    lax.fori_loop(0, n_kv // 2, body, 0, unroll=FLASH_UNROLL)

    outs = []
    for h in range(AT_GROUP):
        acc = acc_ref[h]
        outs.append(acc[0:AT_HD, :] / acc[AT_HD:AT_HD + 1, :])
    o_ref[...] = jnp.transpose(jnp.concatenate(outs, axis=0)).astype(BF16)


def _flash_call(qT, k, vT, n_real):
    n_kv = n_real // KV_TILE
    tail_blk = n_real // 128
    gw = AT_GROUP * AT_HD
    return pl.pallas_call(
        functools.partial(_flash_kernel, n_kv=n_kv, n_tail_valid=N_META),
        out_shape=jax.ShapeDtypeStruct((n_real, AT_W), BF16),
        grid=(AT_KV_HEADS, n_real // Q_TILE),
        in_specs=[
            pl.BlockSpec((gw, Q_TILE), lambda g, i: (g, i)),
            pl.BlockSpec((n_real, AT_KVW), lambda g, i: (0, 0)),
            pl.BlockSpec((None, V_ROWS, n_real), lambda g, i: (g, 0, 0)),
            pl.BlockSpec((128, AT_KVW), lambda g, i: (tail_blk, 0)),
            pl.BlockSpec((None, V_ROWS, 128), lambda g, i: (g, 0, tail_blk)),
        ],
        out_specs=pl.BlockSpec((Q_TILE, gw), lambda g, i: (i, g)),
        scratch_shapes=[
            pltpu.VMEM((AT_GROUP, AT_KVW, Q_TILE), BF16),
            pltpu.VMEM((AT_GROUP, 1, Q_TILE), F32),
            pltpu.VMEM((AT_GROUP, V_ROWS, Q_TILE), F32),
            pltpu.VMEM((2, AT_GROUP, KV_TILE, Q_TILE), F32),
            pltpu.VMEM((2, AT_GROUP, 1, Q_TILE), F32),
            pltpu.VMEM((AT_GROUP, 128, Q_TILE), F32),
            pltpu.VMEM((AT_GROUP, 1, Q_TILE), F32),
        ],
        compiler_params=pltpu.CompilerParams(
            dimension_semantics=("arbitrary", "arbitrary"), vmem_limit_bytes=VMEM_LIMIT),
        name="flash",
    )(qT, k, vT, k, vT)


def _merge_ffn_kernel(h1_ref, of_ref, ob_ref, gs_ref, yb_ref, ga_ref, gb_ref, hgw_ref,
                      wua_ref, wub_ref, wout_ref, nw_ref, wg_ref, wu_ref, wd_ref, o_ref):
    o = of_ref[...] + ob_ref[...]
    normed = []
    for h in range(HG_HEADS):
        oh = o[:, h * HG_K:(h + 1) * HG_K]
        normed.append(oh * lax.rsqrt(jnp.mean(oh * oh, axis=-1, keepdims=True) + EPS))
    ya = (jnp.concatenate(normed, axis=1) * hgw_ref[...] * gs_ref[...]).astype(BF16)
    mixed = ga_ref[...] * _dot(ya, wua_ref[...]) + gb_ref[...] * _dot(yb_ref[...], wub_ref[...])
    h2 = h1_ref[...] + _dot(mixed.astype(BF16), wout_ref[...])
    o_ref[...] = _swiglu_half_step(h2, nw_ref[...], wg_ref, wu_ref, wd_ref)


def _merge_ffn_call(h1, o_f, o_b, gs, yb, ga, gb, hgw, wua, wub, wout, nw, wg, wu, wd, n_real):
    row_spec = lambda w: pl.BlockSpec((ROW_TILE, w), lambda i: (i, 0))
    return pl.pallas_call(
        _merge_ffn_kernel,
        out_shape=jax.ShapeDtypeStruct((n_real, D_MODEL), F32),
        grid=(n_real // ROW_TILE,),
        in_specs=[
            row_spec(D_MODEL), row_spec(HG_W), row_spec(HG_W), row_spec(HG_W), row_spec(AT_W),
            row_spec(D_MODEL), row_spec(D_MODEL),
            _const_spec((1, HG_W)),
            _const_spec((HG_W, D_MODEL)),
            _const_spec((AT_W, D_MODEL)),
            _const_spec((D_MODEL, D_MODEL)),
            _const_spec((1, D_MODEL)),
            _const_spec((D_MODEL, D_FF)),
            _const_spec((D_MODEL, D_FF)),
            _const_spec((D_FF, D_MODEL)),
        ],
        out_specs=row_spec(D_MODEL),
        compiler_params=pltpu.CompilerParams(
            dimension_semantics=("arbitrary",), vmem_limit_bytes=VMEM_LIMIT),
        name="merge_ffn2",
    )(h1, o_f, o_b, gs, yb, ga, gb, hgw, wua, wub, wout, nw, wg, wu, wd)


def _rope_tables(n_real):
    half = AT_HD // 2
    inv = ROPE_THETA ** (-jnp.arange(0, half, 2, dtype=F32) / half)
    sign = jnp.tile(jnp.array([-1.0, 1.0], F32), half)

    def lanes(ang, first_half):
        cos, sin = jnp.cos(ang), jnp.sin(ang)
        one, zero = jnp.ones_like(cos), jnp.zeros_like(sin)
        cos = jnp.concatenate([cos, one] if first_half else [one, cos], axis=-1)
        sin = jnp.concatenate([sin, zero] if first_half else [zero, sin], axis=-1)
        cos = jnp.repeat(cos, 2, axis=-1)
        sin = jnp.repeat(sin, 2, axis=-1) * sign
        reps = 128 // AT_HD
        return jnp.tile(cos, (1, reps)), jnp.tile(sin, (1, reps))

    r = jnp.arange(n_real // GRID_W, dtype=F32)
    c = jnp.arange(GRID_W, dtype=F32)
    return lanes(r[:, None] * inv, True) + lanes(c[:, None] * inv, False)


def _head_mean_matrix(width):
    heads = width // AT_HD
    return jnp.kron(jnp.eye(heads, dtype=F32), jnp.full((AT_HD, AT_HD), 1.0 / AT_HD, F32)).astype(BF16)


def kernel(x, meta_tokens, ffn1_norm, ffn1_w_gate, ffn1_w_up, ffn1_w_down, mix_norm, w_in, hg_lb_fwd, hg_lb_bwd, hg_out_norm, q_norm, k_norm, w_up_a, w_up_b, w_out, ffn2_norm, ffn2_w_gate, ffn2_w_up, ffn2_w_down):
    batch, n_real, _ = x.shape
    assert batch == 1 and n_real % ROW_TILE == 0 and n_real % GRID_W == 0
    assert n_real % (HGRN_STEP_CHUNKS * CHUNK) == 0 and n_real % Q_TILE == 0 and n_real % (2 * KV_TILE) == 0
    assert N_META <= CHUNK <= ROW_TILE and ROW_TILE % GRID_W == 0
    rows = n_real + ROW_TILE
    n_valid = n_real + N_META

    tail = jnp.concatenate(
        [meta_tokens.astype(x.dtype), jnp.zeros((ROW_TILE - N_META, D_MODEL), x.dtype)], axis=0)
    row = lambda w: w.reshape(1, -1).astype(F32)
    (w1g, w1u, w1d, win, wua, wub, wout, w2g, w2u, w2d) = _cast_weights(
        [w[0].astype(F32) for w in (ffn1_w_gate, ffn1_w_up, ffn1_w_down, w_in, w_up_a, w_up_b, w_out,
                                    ffn2_w_gate, ffn2_w_up, ffn2_w_down)])

    h1 = _ffn_call(x[0], tail, row(ffn1_norm[0]), w1g, w1u, w1d)

    rope = _rope_tables(n_real)
    hq, hv, kf, lff, kb, lfb, gs, qT, k, vT, ga, gb = _mix_proj_call(
        h1, row(mix_norm[0]), win, hg_lb_fwd.astype(F32), hg_lb_bwd.astype(F32),
        row(jnp.tile(q_norm[0], AT_HEADS)), row(jnp.tile(k_norm[0], AT_KV_HEADS)), rope,
        _head_mean_matrix(AT_W), _head_mean_matrix(AT_KVW), n_valid)

    o_f, o_b = _hgrn_call(hq, hv, kf, lff, kb, lfb, n_real)
    yb = _flash_call(qT, k, vT, n_real)

    out = _merge_ffn_call(
        h1, o_f, o_b, gs, yb, ga, gb, row(hg_out_norm[0]), wua, wub, wout,
        row(ffn2_norm[0]), w2g, w2u, w2d, n_real)
    return out.reshape(batch, n_real, D_MODEL)
```

```python
import functools

import jax
import jax.numpy as jnp
from jax import lax
from jax.experimental import pallas as pl
from jax.experimental.pallas import tpu as pltpu

F32 = jnp.float32
BF16 = jnp.bfloat16

D_MODEL = 1024
D_FF = 2816
N_META = 16
GRID_W = 64
EPS = 1e-6
HG_HEADS = 4
HG_K = 128
HG_W = HG_HEADS * HG_K
AT_HEADS = 8
AT_KV_HEADS = 2
AT_HD = 64
AT_GROUP = AT_HEADS // AT_KV_HEADS
AT_W = AT_HEADS * AT_HD
AT_KVW = AT_KV_HEADS * AT_HD
ROPE_THETA = 10000.0
IN_SIZES = (HG_W, HG_W, HG_W, HG_W, HG_W, AT_W, AT_KVW, AT_KVW, D_MODEL, D_MODEL)
IN_OFFS = tuple(sum(IN_SIZES[:i]) for i in range(len(IN_SIZES)))
D_IN = sum(IN_SIZES)

ROW_TILE = 512
FF_CHUNK = 256
CHUNK = 128
HGRN_STEP_CHUNKS = 8
Q_TILE = 256
KV_TILE = 256
V_ROWS = AT_HD + 16
LOG2E = 1.4426950408889634
Q_SCALE = AT_HD ** -0.5 * LOG2E
FLASH_UNROLL = 16
FLASH_LEAD = AT_GROUP + 1
VMEM_LIMIT = 56 * 1024 * 1024


def _dot(a, b):
    return jnp.dot(a, b, preferred_element_type=F32)


def _dot_nt(a, b):
    return lax.dot_general(a, b, (((1,), (1,)), ((), ())), preferred_element_type=F32)


def _dot_tn(a, b):
    return lax.dot_general(a, b, (((0,), (0,)), ((), ())), preferred_element_type=F32)


def _rms(x, w):
    ms = jnp.mean(x * x, axis=-1, keepdims=True)
    return x * lax.rsqrt(ms + EPS) * w


def _sigmoid(x):
    return 0.5 * jnp.tanh(0.5 * x) + 0.5


def _silu(x):
    h = 0.5 * x
    return h * jnp.tanh(h) + h


def _swiglu_half_step(h, norm_w, wg_ref, wu_ref, wd_ref):
    hn = _rms(h, norm_w).astype(BF16)
    acts = []
    for c in range(D_FF // FF_CHUNK):
        sl = slice(c * FF_CHUNK, (c + 1) * FF_CHUNK)
        g = _dot(hn, wg_ref[:, sl])
        u = _dot(hn, wu_ref[:, sl])
        acts.append((_silu(g) * u).astype(BF16))
    return h + 0.5 * _dot(jnp.concatenate(acts, axis=1), wd_ref[...])


CAST_STEPS = 16


def _cast_kernel(*refs):
    n = len(refs) // 2
    for src, dst in zip(refs[:n], refs[n:]):
        dst[...] = src[...].astype(BF16)


def _cast_weights(ws):
    specs = [pl.BlockSpec((w.shape[0] // CAST_STEPS, w.shape[1]), lambda i: (i, 0)) for w in ws]
    return pl.pallas_call(
        _cast_kernel,
        out_shape=[jax.ShapeDtypeStruct(w.shape, BF16) for w in ws],
        grid=(CAST_STEPS,),
        in_specs=specs,
        out_specs=specs,
        compiler_params=pltpu.CompilerParams(
            dimension_semantics=("arbitrary",), vmem_limit_bytes=VMEM_LIMIT),
        name="cast_weights",
    )(*ws)


def _ffn_kernel(x_ref, tail_ref, nw_ref, wg_ref, wu_ref, wd_ref, o_ref, *, n_x_tiles):
    h = jnp.where(pl.program_id(0) < n_x_tiles, x_ref[...], tail_ref[...])
    o_ref[...] = _swiglu_half_step(h, nw_ref[...], wg_ref, wu_ref, wd_ref)


def _const_spec(shape):
    nd = len(shape)
    return pl.BlockSpec(shape, lambda *_: (0,) * nd, pipeline_mode=pl.Buffered(1))


def _ffn_call(x, tail, nw, wg, wu, wd):
    n_x_tiles = x.shape[0] // ROW_TILE
    rows = x.shape[0] + tail.shape[0]
    return pl.pallas_call(
        functools.partial(_ffn_kernel, n_x_tiles=n_x_tiles),
        out_shape=jax.ShapeDtypeStruct((rows, D_MODEL), F32),
        grid=(n_x_tiles + 1,),
        in_specs=[
            pl.BlockSpec((ROW_TILE, D_MODEL), lambda i: (jnp.minimum(i, n_x_tiles - 1), 0)),
            _const_spec((ROW_TILE, D_MODEL)),
            _const_spec((1, D_MODEL)),
            _const_spec((D_MODEL, D_FF)),
            _const_spec((D_MODEL, D_FF)),
            _const_spec((D_FF, D_MODEL)),
        ],
        out_specs=pl.BlockSpec((ROW_TILE, D_MODEL), lambda i: (i, 0)),
        compiler_params=pltpu.CompilerParams(
            dimension_semantics=("arbitrary",), vmem_limit_bytes=VMEM_LIMIT),
        name="ffn1",
    )(x, tail, nw, wg, wu, wd)


def _lower_bound_gate(z, lbp, valid):
    m = jnp.max(lbp, axis=0, keepdims=True)
    e = jnp.exp(lbp - m)
    lb = e[0:1, :] / jnp.sum(e, axis=0, keepdims=True)
    kk = (1.0 - lb) * _sigmoid(-z)
    kk = jnp.where(valid, kk, 0.0)
    return kk, jnp.log2(1.0 - kk)


def _head_rms_rope(z, ms, w, cos, sin):
    zn = z * lax.rsqrt(ms + EPS) * w
    width = z.shape[1]
    lane = lax.broadcasted_iota(jnp.int32, z.shape, 1)
    partner = jnp.where((lane & 1) == 0, pltpu.roll(zn, width - 1, 1), pltpu.roll(zn, 1, 1))
    return zn * cos + partner * sin


def _rope_tile(by_row_ref, by_col_ref, is_tail, tail_value):
    groups = ROW_TILE // GRID_W
    rp = by_row_ref[...]
    by_row = jnp.concatenate([jnp.broadcast_to(rp[j:j + 1, :], (GRID_W, 128)) for j in range(groups)], axis=0)
    by_col = jnp.concatenate([by_col_ref[...]] * groups, axis=0)
    lane = lax.broadcasted_iota(jnp.int32, (1, 128), 1)
    t = jnp.where((lane & (AT_HD - 1)) < AT_HD // 2, by_row, by_col)
    return jnp.where(is_tail, tail_value, t)


def _mix_proj_kernel(h_ref, nw_ref, win_ref, lbf_ref, lbb_ref, qnw_ref, knw_ref, rcos_ref, rsin_ref,
                     ccos_ref, csin_ref, bdq_ref, bdk_ref,
                     hq_ref, hv_ref, kf_ref, lff_ref, kb_ref, lfb_ref, gs_ref, qT_ref, k_ref, vT_ref,
                     ga_ref, gb_ref, *, n_valid, n_x_tiles):
    i = pl.program_id(0)
    un = _rms(h_ref[...], nw_ref[...]).astype(BF16)

    def proj(piece):
        off, size = IN_OFFS[piece], IN_SIZES[piece]
        return _dot(un, win_ref[:, off:off + size])

    row = i * ROW_TILE + lax.broadcasted_iota(jnp.int32, (ROW_TILE, 1), 0)
    valid = row < n_valid

    zq = proj(5)
    zk = proj(6)
    cos = _rope_tile(rcos_ref, ccos_ref, i >= n_x_tiles, 1.0)
    sin = _rope_tile(rsin_ref, csin_ref, i >= n_x_tiles, 0.0)
    cos_q = jnp.concatenate([cos] * (AT_W // 128), axis=1)
    sin_q = jnp.concatenate([sin] * (AT_W // 128), axis=1)
    ms_q = _dot((zq * zq).astype(BF16), bdq_ref[...])
    ms_k = _dot((zk * zk).astype(BF16), bdk_ref[...])
    q = _head_rms_rope(zq, ms_q, qnw_ref[...], cos_q, sin_q) * Q_SCALE
    qT_ref[...] = jnp.transpose(q).astype(BF16)
    k_ref[...] = _head_rms_rope(zk, ms_k, knw_ref[...], cos, sin).astype(BF16)

    hq_ref[...] = _silu(proj(0))
    hv_ref[...] = proj(1).astype(BF16)
    kf, lff = _lower_bound_gate(proj(2), lbf_ref[...], valid)
    kf_ref[...] = kf
    lff_ref[...] = lff
    kb, lfb = _lower_bound_gate(proj(3), lbb_ref[...], valid)
    kb_ref[...] = kb
    lfb_ref[...] = lfb
    gs_ref[...] = _silu(proj(4))

    vT = jnp.transpose(proj(7)).astype(BF16)
    ones = jnp.ones((V_ROWS - AT_HD, ROW_TILE), BF16)
    for g in range(AT_KV_HEADS):
        vT_ref[g, 0:AT_HD, :] = vT[g * AT_HD:(g + 1) * AT_HD, :]
        vT_ref[g, AT_HD:V_ROWS, :] = ones

    ga_ref[...] = _sigmoid(proj(8))
    gb_ref[...] = _sigmoid(proj(9))


def _mix_proj_call(h1, nw, win, lbf, lbb, qnw, knw, rope, bdq, bdk, n_valid):
    rows = h1.shape[0]
    n_x_tiles = rows // ROW_TILE - 1
    groups = ROW_TILE // GRID_W
    rope_row_spec = pl.BlockSpec((groups, 128), lambda i: (jnp.minimum(i, n_x_tiles - 1), 0))
    row_spec = lambda w: pl.BlockSpec((ROW_TILE, w), lambda i: (i, 0))
    out_shape = [
        jax.ShapeDtypeStruct((rows, HG_W), F32),
        jax.ShapeDtypeStruct((rows, HG_W), BF16),
        jax.ShapeDtypeStruct((rows, HG_W), F32),
        jax.ShapeDtypeStruct((rows, HG_W), F32),
        jax.ShapeDtypeStruct((rows, HG_W), F32),
        jax.ShapeDtypeStruct((rows, HG_W), F32),
        jax.ShapeDtypeStruct((rows, HG_W), F32),
        jax.ShapeDtypeStruct((AT_W, rows), BF16),
        jax.ShapeDtypeStruct((rows, AT_KVW), BF16),
        jax.ShapeDtypeStruct((AT_KV_HEADS, V_ROWS, rows), BF16),
        jax.ShapeDtypeStruct((rows, D_MODEL), F32),
        jax.ShapeDtypeStruct((rows, D_MODEL), F32),
    ]
    out_specs = [
        row_spec(HG_W), row_spec(HG_W), row_spec(HG_W), row_spec(HG_W), row_spec(HG_W), row_spec(HG_W),
        row_spec(HG_W),
        pl.BlockSpec((AT_W, ROW_TILE), lambda i: (0, i)),
        row_spec(AT_KVW),
        pl.BlockSpec((AT_KV_HEADS, V_ROWS, ROW_TILE), lambda i: (0, 0, i)),
        row_spec(D_MODEL), row_spec(D_MODEL),
    ]
    in_specs = [
        row_spec(D_MODEL),
        _const_spec((1, D_MODEL)),
        _const_spec((D_MODEL, D_IN)),
        _const_spec(lbf.shape),
        _const_spec(lbb.shape),
        _const_spec((1, AT_W)),
        _const_spec((1, AT_KVW)),
        rope_row_spec,
        rope_row_spec,
        _const_spec((GRID_W, 128)),
        _const_spec((GRID_W, 128)),
        _const_spec((AT_W, AT_W)),
        _const_spec((AT_KVW, AT_KVW)),
    ]
    return pl.pallas_call(
        functools.partial(_mix_proj_kernel, n_valid=n_valid, n_x_tiles=n_x_tiles),
        out_shape=out_shape,
        grid=(rows // ROW_TILE,),
        in_specs=in_specs,
        out_specs=out_specs,
        compiler_params=pltpu.CompilerParams(
            dimension_semantics=("arbitrary",), vmem_limit_bytes=VMEM_LIMIT),
        name="mix_proj",
    )(h1, nw, win, lbf, lbb, qnw, knw, *rope, bdq, bdk)


def _split2(x):
    hi = x.astype(BF16)
    lo = (x - hi.astype(F32)).astype(BF16)
    return hi, lo


def _level_operand(q, k, b, lf, blk, reverse):
    c = CHUNK
    if blk >= 8:
        pieces = []
        for g in range(c // (2 * blk)):
            r0 = g * 2 * blk
            lo, hi = slice(r0, r0 + blk), slice(r0 + blk, r0 + 2 * blk)
            if reverse:
                ref = b[r0 + blk:r0 + blk + 1, :]
                pieces.append(q[lo] * jnp.exp2(b[lo] - ref))
                pieces.append(k[hi] * jnp.exp2(ref - b[hi]))
            else:
                ref = b[r0 + blk - 1:r0 + blk, :]
                pieces.append(k[lo] * jnp.exp2(ref - b[lo]))
                pieces.append(q[hi] * jnp.exp2(b[hi] - ref))
        return jnp.concatenate(pieces, axis=0).astype(BF16)

    q3, k3, b3, lf3 = (x.reshape(c // 8, 8, HG_W) for x in (q, k, b, lf))
    sub = lax.broadcasted_iota(jnp.int32, (1, 8, 1), 1)
    later = ((sub >> (blk.bit_length() - 1)) & 1) == 1
    q_role = jnp.logical_not(later) if reverse else later
    if blk == 4:
        ref = b3[:, 4:5, :] if reverse else b3[:, 3:4, :]
        gl = -jnp.abs(b3 - ref)
    elif blk == 2:
        up = pltpu.roll(lf3, 7, 1)
        dn = pltpu.roll(lf3, 1, 1)
        m4 = sub & 3
        if reverse:
            gl = jnp.where(m4 == 0, lf3 + up, jnp.where(m4 == 1, lf3, jnp.where(m4 == 2, 0.0, dn)))
        else:
            gl = jnp.where(m4 == 0, up, jnp.where(m4 == 1, 0.0, jnp.where(m4 == 2, lf3, dn + lf3)))
    else:
        odd = (sub & 1) == 1
        gl = jnp.where(odd, 0.0, lf3) if reverse else jnp.where(odd, lf3, 0.0)
    y3 = jnp.where(q_role, q3, k3) * jnp.exp2(gl)
    return y3.reshape(c, HG_W).astype(BF16)


def _hgrn_direction(q_ref, v_ref, k_ref, lf_ref, o_ref, st_ref, reverse):
    c = CHUNK
    q = q_ref[...]
    k = k_ref[...]
    v = v_ref[...]
    lf = lf_ref[...]

    r_i = lax.broadcasted_iota(jnp.int32, (c, c), 0)
    c_i = lax.broadcasted_iota(jnp.int32, (c, c), 1)
    tri = ((c_i >= r_i) if reverse else (c_i <= r_i)).astype(BF16)
    hi, lo = _split2(lf)
    b = _dot(jnp.concatenate([tri, tri], axis=1), jnp.concatenate([hi, lo], axis=0))

    levels = (64, 32, 16, 8, 4, 2, 1)
    ys = [_level_operand(q, k, b, lf, blk, reverse) for blk in levels]

    lane = lax.broadcasted_iota(jnp.int32, (1, c), 1)
    small_valid = {}
    for blk in (4, 2, 1):
        sh = blk.bit_length() - 1
        same = (r_i >> (sh + 1)) == (c_i >> (sh + 1))
        t_later = ((r_i >> sh) & 1) == 1
        s_later = ((c_i >> sh) & 1) == 1
        small_valid[blk] = (same & jnp.logical_not(t_later) & s_later) if reverse else (
            same & t_later & jnp.logical_not(s_later))

    def tile_mask(blk, tile):
        if blk < 8:
            return small_valid[blk][8 * tile:8 * tile + 8, :]
        block = (8 * tile) // blk
        if (block % 2 == 1) == reverse:
            return None
        col0 = (block + 1) * blk if reverse else (block - 1) * blk
        return (lane >= col0) & (lane < col0 + blk)

    b_last = b[0:1, :] if reverse else b[c - 1:c, :]
    q_inter = (q * jnp.exp2(b)).astype(BF16)
    k_state = (k * jnp.exp2(b_last - b)).astype(BF16)
    e_last = jnp.exp2(b_last)
    qk = q * k
    eye = r_i == c_i

    def issue(h):
        hs = slice(h * HG_K, (h + 1) * HG_K)
        products = [_dot_nt(y[:, hs], y[:, hs]) for y in ys]
        st = st_ref[h]
        inter = _dot_nt(q_inter[:, hs], st.astype(BF16))
        st_ref[h] = st * e_last[:, hs] + _dot_tn(v[:, hs], k_state[:, hs])
        return products, inter

    def finish(h, issued):
        products, inter = issued
        hs = slice(h * HG_K, (h + 1) * HG_K)
        diag = jnp.where(eye, jnp.sum(qk[:, hs], axis=1, keepdims=True), 0.0)
        tiles = []
        for t in range(c // 8):
            rows = slice(8 * t, 8 * t + 8)
            a_t = diag[rows]
            for blk, p in zip(levels, products):
                mask = tile_mask(blk, t)
                if mask is not None:
                    a_t = jnp.where(mask, p[rows], a_t)
            tiles.append(a_t)
        a = jnp.concatenate(tiles, axis=0)
        o_ref[:, hs] = _dot(a.astype(BF16), v[:, hs]) + inter

    return issue, finish


def _hgrn_meta_state(v_ref, k_ref, lf_ref, st_ref):
    c = CHUNK
    k = k_ref[...]
    v = v_ref[...]
    r_i = lax.broadcasted_iota(jnp.int32, (c, c), 0)
    c_i = lax.broadcasted_iota(jnp.int32, (c, c), 1)
    tri = (c_i <= r_i).astype(BF16)
    hi, lo = _split2(lf_ref[...])
    b = _dot(jnp.concatenate([tri, tri], axis=1), jnp.concatenate([hi, lo], axis=0))
    k_state = (k * jnp.exp2(b[c - 1:c, :] - b)).astype(BF16)
    for h in range(HG_HEADS):
        hs = slice(h * HG_K, (h + 1) * HG_K)
        st_ref[h] = _dot_tn(v[:, hs], k_state[:, hs])


def _hgrn_kernel(qf_ref, vf_ref, kf_ref, lff_ref, qb_ref, vb_ref, kb_ref, lfb_ref, vm_ref, kfm_ref, lfm_ref,
                 of_ref, ob_ref, sf_ref, sb_ref):
    @pl.when(pl.program_id(0) == 0)
    def _():
        sb_ref[...] = jnp.zeros_like(sb_ref)
        _hgrn_meta_state(vm_ref, kfm_ref, lfm_ref, sf_ref)

    def sub(ref, j):
        return ref.at[pl.ds(j * CHUNK, CHUNK)]

    units = []
    for j in range(HGRN_STEP_CHUNKS):
        jb = HGRN_STEP_CHUNKS - 1 - j
        fwd = (sub(qf_ref, j), sub(vf_ref, j), sub(kf_ref, j), sub(lff_ref, j), sub(of_ref, j), sf_ref, False)
        bwd = (sub(qb_ref, jb), sub(vb_ref, jb), sub(kb_ref, jb), sub(lfb_ref, jb), sub(ob_ref, jb), sb_ref, True)
        for args in (fwd, bwd):
            issue, finish = _hgrn_direction(*args)
            units += [(issue, finish, h) for h in range(HG_HEADS)]

    pending = None
    for issue, finish, h in units:
        issued = issue(h)
        if pending is not None:
            pending[0](pending[1], pending[2])
        pending = (finish, h, issued)
    pending[0](pending[1], pending[2])


def _hgrn_call(hq, hv, kf, lff, kb, lfb, n_real):
    nb = n_real // CHUNK
    steps = nb // HGRN_STEP_CHUNKS
    fwd = lambda s: (s, 0)
    bwd = lambda s: (steps - 1 - s, 0)
    spec = lambda m: pl.BlockSpec((HGRN_STEP_CHUNKS * CHUNK, HG_W), m)
    meta = pl.BlockSpec((CHUNK, HG_W), lambda s: (nb, 0))
    out = jax.ShapeDtypeStruct((n_real, HG_W), F32)
    return pl.pallas_call(
        _hgrn_kernel,
        out_shape=[out, out],
        grid=(steps,),
        in_specs=[spec(fwd), spec(fwd), spec(fwd), spec(fwd), spec(bwd), spec(bwd), spec(bwd), spec(bwd),
                  meta, meta, meta],
        out_specs=[spec(fwd), spec(bwd)],
        scratch_shapes=[pltpu.VMEM((HG_HEADS, HG_K, HG_K), F32), pltpu.VMEM((HG_HEADS, HG_K, HG_K), F32)],
        compiler_params=pltpu.CompilerParams(
            dimension_semantics=("arbitrary",), vmem_limit_bytes=VMEM_LIMIT),
        name="hgrn",
    )(hq, hv, kf, lff, hq, hv, kb, lfb, hv, kf, lff)


def _flash_kernel(qT_ref, k_ref, vT_ref, kt_ref, vTt_ref, o_ref, qp_ref, m_ref, acc_ref, s_ref, mc_ref,
                  st_ref, mct_ref, *,
                  n_kv, n_tail_valid):
    g = pl.program_id(0)
    tq = qT_ref.shape[1]

    half = lax.broadcasted_iota(jnp.int32, (AT_KVW, tq), 0) >> 6
    for h in range(AT_GROUP):
        qh = qT_ref[h * AT_HD:(h + 1) * AT_HD, :].astype(F32)
        q2 = jnp.concatenate([qh, qh], axis=0)
        qp_ref[h] = jnp.where(half == g, q2, 0.0).astype(BF16)

    m_ref[...] = jnp.full(m_ref.shape, -jnp.inf, F32)
    acc_ref[...] = jnp.zeros(acc_ref.shape, F32)

    def store_scores(sT, s_dst, mc_dst, h):
        s_dst[h] = sT
        mc_dst[h] = jnp.max(sT, axis=0, keepdims=True)

    def unit_scores(chunk, h, slot):
        kc = k_ref[pl.ds(pl.multiple_of(chunk * KV_TILE, KV_TILE), KV_TILE), :]
        store_scores(_dot(kc, qp_ref[h]), s_ref.at[slot], mc_ref.at[slot], h)

    def unit_consume(vc, s_src, mc_src, h):
        m_prev = m_ref[h]
        m_new = jnp.maximum(m_prev, mc_src[h])
        alpha = jnp.exp2(m_prev - m_new)
        pT = jnp.exp2(s_src[h] - m_new).astype(BF16)
        acc_ref[h] = alpha * acc_ref[h] + _dot(vc, pT)
        m_ref[h] = m_new

    krow = lax.broadcasted_iota(jnp.int32, (128, 1), 0)
    for h in range(AT_GROUP):
        sT = jnp.where(krow < n_tail_valid, _dot(kt_ref[...], qp_ref[h]), -jnp.inf)
        store_scores(sT, st_ref, mct_ref, h)
    for u in range(FLASH_LEAD):
        unit_scores(u // AT_GROUP, u % AT_GROUP, (u // AT_GROUP) % 2)
    for h in range(AT_GROUP):
        unit_consume(vTt_ref[...], st_ref, mct_ref, h)

    def body(i, carry):
        for j in range(2 * AT_GROUP):
            uq = j + FLASH_LEAD
            cq = jnp.minimum(2 * i + uq // AT_GROUP, n_kv - 1)
            unit_scores(cq, uq % AT_GROUP, (uq // AT_GROUP) % 2)
            cp = 2 * i + j // AT_GROUP
            vc = vT_ref[:, pl.ds(pl.multiple_of(cp * KV_TILE, KV_TILE), KV_TILE)]
            unit_consume(vc, s_ref.at[(j // AT_GROUP) % 2], mc_ref.at[(j // AT_GROUP) % 2], j % AT_GROUP)
        return carry

    lax.fori_loop(0, n_kv // 2, body, 0, unroll=FLASH_UNROLL)

    outs = []
    for h in range(AT_GROUP):
        acc = acc_ref[h]
        outs.append(acc[0:AT_HD, :] / acc[AT_HD:AT_HD + 1, :])
    o_ref[...] = jnp.transpose(jnp.concatenate(outs, axis=0)).astype(BF16)


def _flash_call(qT, k, vT, n_real):
    n_kv = n_real // KV_TILE
    tail_blk = n_real // 128
    gw = AT_GROUP * AT_HD
    return pl.pallas_call(
        functools.partial(_flash_kernel, n_kv=n_kv, n_tail_valid=N_META),
        out_shape=jax.ShapeDtypeStruct((n_real, AT_W), BF16),
        grid=(AT_KV_HEADS, n_real // Q_TILE),
        in_specs=[
            pl.BlockSpec((gw, Q_TILE), lambda g, i: (g, i)),
            pl.BlockSpec((n_real, AT_KVW), lambda g, i: (0, 0)),
            pl.BlockSpec((None, V_ROWS, n_real), lambda g, i: (g, 0, 0)),
            pl.BlockSpec((128, AT_KVW), lambda g, i: (tail_blk, 0)),
            pl.BlockSpec((None, V_ROWS, 128), lambda g, i: (g, 0, tail_blk)),
        ],
        out_specs=pl.BlockSpec((Q_TILE, gw), lambda g, i: (i, g)),
        scratch_shapes=[
            pltpu.VMEM((AT_GROUP, AT_KVW, Q_TILE), BF16),
            pltpu.VMEM((AT_GROUP, 1, Q_TILE), F32),
            pltpu.VMEM((AT_GROUP, V_ROWS, Q_TILE), F32),
            pltpu.VMEM((2, AT_GROUP, KV_TILE, Q_TILE), F32),
            pltpu.VMEM((2, AT_GROUP, 1, Q_TILE), F32),
            pltpu.VMEM((AT_GROUP, 128, Q_TILE), F32),
            pltpu.VMEM((AT_GROUP, 1, Q_TILE), F32),
        ],
        compiler_params=pltpu.CompilerParams(
            dimension_semantics=("arbitrary", "arbitrary"), vmem_limit_bytes=VMEM_LIMIT),
        name="flash",
    )(qT, k, vT, k, vT)


def _merge_ffn_kernel(h1_ref, of_ref, ob_ref, gs_ref, yb_ref, ga_ref, gb_ref, hgw_ref,
                      wua_ref, wub_ref, wout_ref, nw_ref, wg_ref, wu_ref, wd_ref, o_ref):
    o = of_ref[...] + ob_ref[...]
    normed = []
    for h in range(HG_HEADS):
        oh = o[:, h * HG_K:(h + 1) * HG_K]
        normed.append(oh * lax.rsqrt(jnp.mean(oh * oh, axis=-1, keepdims=True) + EPS))
    ya = (jnp.concatenate(normed, axis=1) * hgw_ref[...] * gs_ref[...]).astype(BF16)
    mixed = ga_ref[...] * _dot(ya, wua_ref[...]) + gb_ref[...] * _dot(yb_ref[...], wub_ref[...])
    h2 = h1_ref[...] + _dot(mixed.astype(BF16), wout_ref[...])
    o_ref[...] = _swiglu_half_step(h2, nw_ref[...], wg_ref, wu_ref, wd_ref)


def _merge_ffn_call(h1, o_f, o_b, gs, yb, ga, gb, hgw, wua, wub, wout, nw, wg, wu, wd, n_real):
    row_spec = lambda w: pl.BlockSpec((ROW_TILE, w), lambda i: (i, 0))
    return pl.pallas_call(
        _merge_ffn_kernel,
        out_shape=jax.ShapeDtypeStruct((n_real, D_MODEL), F32),
        grid=(n_real // ROW_TILE,),
        in_specs=[
            row_spec(D_MODEL), row_spec(HG_W), row_spec(HG_W), row_spec(HG_W), row_spec(AT_W),
            row_spec(D_MODEL), row_spec(D_MODEL),
            _const_spec((1, HG_W)),
            _const_spec((HG_W, D_MODEL)),
            _const_spec((AT_W, D_MODEL)),
            _const_spec((D_MODEL, D_MODEL)),
            _const_spec((1, D_MODEL)),
            _const_spec((D_MODEL, D_FF)),
            _const_spec((D_MODEL, D_FF)),
            _const_spec((D_FF, D_MODEL)),
        ],
        out_specs=row_spec(D_MODEL),
        compiler_params=pltpu.CompilerParams(
            dimension_semantics=("arbitrary",), vmem_limit_bytes=VMEM_LIMIT),
        name="merge_ffn2",
    )(h1, o_f, o_b, gs, yb, ga, gb, hgw, wua, wub, wout, nw, wg, wu, wd)


def _rope_tables(n_real):
    half = AT_HD // 2
    inv = ROPE_THETA ** (-jnp.arange(0, half, 2, dtype=F32) / half)
    sign = jnp.tile(jnp.array([-1.0, 1.0], F32), half)

    def lanes(ang, first_half):
        cos, sin = jnp.cos(ang), jnp.sin(ang)
        one, zero = jnp.ones_like(cos), jnp.zeros_like(sin)
        cos = jnp.concatenate([cos, one] if first_half else [one, cos], axis=-1)
        sin = jnp.concatenate([sin, zero] if first_half else [zero, sin], axis=-1)
        cos = jnp.repeat(cos, 2, axis=-1)
        sin = jnp.repeat(sin, 2, axis=-1) * sign
        reps = 128 // AT_HD
        return jnp.tile(cos, (1, reps)), jnp.tile(sin, (1, reps))

    r = jnp.arange(n_real // GRID_W, dtype=F32)
    c = jnp.arange(GRID_W, dtype=F32)
    return lanes(r[:, None] * inv, True) + lanes(c[:, None] * inv, False)


def _head_mean_matrix(width):
    heads = width // AT_HD
    return jnp.kron(jnp.eye(heads, dtype=F32), jnp.full((AT_HD, AT_HD), 1.0 / AT_HD, F32)).astype(BF16)


def kernel(x, meta_tokens, ffn1_norm, ffn1_w_gate, ffn1_w_up, ffn1_w_down, mix_norm, w_in, hg_lb_fwd, hg_lb_bwd, hg_out_norm, q_norm, k_norm, w_up_a, w_up_b, w_out, ffn2_norm, ffn2_w_gate, ffn2_w_up, ffn2_w_down):
    batch, n_real, _ = x.shape
    assert batch == 1 and n_real % ROW_TILE == 0 and n_real % GRID_W == 0
    assert n_real % (HGRN_STEP_CHUNKS * CHUNK) == 0 and n_real % Q_TILE == 0 and n_real % (2 * KV_TILE) == 0
    assert N_META <= CHUNK <= ROW_TILE and ROW_TILE % GRID_W == 0
    rows = n_real + ROW_TILE
    n_valid = n_real + N_META

    tail = jnp.concatenate(
        [meta_tokens.astype(x.dtype), jnp.zeros((ROW_TILE - N_META, D_MODEL), x.dtype)], axis=0)
    row = lambda w: w.reshape(1, -1).astype(F32)
    (w1g, w1u, w1d, win, wua, wub, wout, w2g, w2u, w2d) = _cast_weights(
        [w[0].astype(F32) for w in (ffn1_w_gate, ffn1_w_up, ffn1_w_down, w_in, w_up_a, w_up_b, w_out,
                                    ffn2_w_gate, ffn2_w_up, ffn2_w_down)])

    h1 = _ffn_call(x[0], tail, row(ffn1_norm[0]), w1g, w1u, w1d)

    rope = _rope_tables(n_real)
    hq, hv, kf, lff, kb, lfb, gs, qT, k, vT, ga, gb = _mix_proj_call(
        h1, row(mix_norm[0]), win, hg_lb_fwd.astype(F32), hg_lb_bwd.astype(F32),
        row(jnp.tile(q_norm[0], AT_HEADS)), row(jnp.tile(k_norm[0], AT_KV_HEADS)), rope,
        _head_mean_matrix(AT_W), _head_mean_matrix(AT_KVW), n_valid)

    o_f, o_b = _hgrn_call(hq, hv, kf, lff, kb, lfb, n_real)
    yb = _flash_call(qT, k, vT, n_real)

    out = _merge_ffn_call(
        h1, o_f, o_b, gs, yb, ga, gb, row(hg_out_norm[0]), wua, wub, wout,
        row(ffn2_norm[0]), w2g, w2u, w2d, n_real)
    return out.reshape(batch, n_real, D_MODEL)
```

```python
import functools

import jax
import jax.numpy as jnp
from jax import lax
from jax.experimental import pallas as pl
from jax.experimental.pallas import tpu as pltpu

F32 = jnp.float32
BF16 = jnp.bfloat16

D_MODEL = 1024
D_FF = 2816
N_META = 16
GRID_W = 64
EPS = 1e-6
HG_HEADS = 4
HG_K = 128
HG_W = HG_HEADS * HG_K
AT_HEADS = 8
AT_KV_HEADS = 2
AT_HD = 64
AT_GROUP = AT_HEADS // AT_KV_HEADS
AT_W = AT_HEADS * AT_HD
AT_KVW = AT_KV_HEADS * AT_HD
ROPE_THETA = 10000.0
IN_SIZES = (HG_W, HG_W, HG_W, HG_W, HG_W, AT_W, AT_KVW, AT_KVW, D_MODEL, D_MODEL)
IN_OFFS = tuple(sum(IN_SIZES[:i]) for i in range(len(IN_SIZES)))
D_IN = sum(IN_SIZES)

ROW_TILE = 512
FF_CHUNK = 256
CHUNK = 128
HGRN_STEP_CHUNKS = 8
Q_TILE = 256
KV_TILE = 256
V_ROWS = AT_HD + 16
LOG2E = 1.4426950408889634
Q_SCALE = AT_HD ** -0.5 * LOG2E
FLASH_UNROLL = 16
FLASH_LEAD = AT_GROUP + 1
VMEM_LIMIT = 56 * 1024 * 1024


def _dot(a, b):
    return jnp.dot(a, b, preferred_element_type=F32)


def _dot_nt(a, b):
    return lax.dot_general(a, b, (((1,), (1,)), ((), ())), preferred_element_type=F32)


def _dot_tn(a, b):
    return lax.dot_general(a, b, (((0,), (0,)), ((), ())), preferred_element_type=F32)


def _rms(x, w):
    ms = jnp.mean(x * x, axis=-1, keepdims=True)
    return x * lax.rsqrt(ms + EPS) * w


def _sigmoid(x):
    return 0.5 * jnp.tanh(0.5 * x) + 0.5


def _silu(x):
    h = 0.5 * x
    return h * jnp.tanh(h) + h


def _swiglu_half_step(h, norm_w, wg_ref, wu_ref, wd_ref):
    hn = _rms(h, norm_w).astype(BF16)
    acc = None
    pending = None
    for c in range(D_FF // FF_CHUNK + 1):
        if c < D_FF // FF_CHUNK:
            sl = slice(c * FF_CHUNK, (c + 1) * FF_CHUNK)
            g = _dot(hn, wg_ref[:, sl])
            u = _dot(hn, wu_ref[:, sl])
        if pending is not None:
            d = _dot(pending[0], wd_ref[pending[1], :])
            acc = d if acc is None else acc + d
        if c < D_FF // FF_CHUNK:
            pending = ((_silu(g) * u).astype(BF16), sl)
    return h + 0.5 * acc


CAST_STEPS = 16


def _cast_kernel(*refs):
    n = len(refs) // 2
    for src, dst in zip(refs[:n], refs[n:]):
        dst[...] = src[...].astype(BF16)


def _cast_weights(ws):
    specs = [pl.BlockSpec((w.shape[0] // CAST_STEPS, w.shape[1]), lambda i: (i, 0)) for w in ws]
    return pl.pallas_call(
        _cast_kernel,
        out_shape=[jax.ShapeDtypeStruct(w.shape, BF16) for w in ws],
        grid=(CAST_STEPS,),
        in_specs=specs,
        out_specs=specs,
        compiler_params=pltpu.CompilerParams(
            dimension_semantics=("arbitrary",), vmem_limit_bytes=VMEM_LIMIT),
        name="cast_weights",
    )(*ws)


def _ffn_kernel(x_ref, tail_ref, nw_ref, wg_ref, wu_ref, wd_ref, o_ref, *, n_x_tiles):
    h = jnp.where(pl.program_id(0) < n_x_tiles, x_ref[...], tail_ref[...])
    o_ref[...] = _swiglu_half_step(h, nw_ref[...], wg_ref, wu_ref, wd_ref)


def _const_spec(shape):
    nd = len(shape)
    return pl.BlockSpec(shape, lambda *_: (0,) * nd, pipeline_mode=pl.Buffered(1))


def _ffn_call(x, tail, nw, wg, wu, wd):
    n_x_tiles = x.shape[0] // ROW_TILE
    rows = x.shape[0] + tail.shape[0]
    return pl.pallas_call(
        functools.partial(_ffn_kernel, n_x_tiles=n_x_tiles),
        out_shape=jax.ShapeDtypeStruct((rows, D_MODEL), F32),
        grid=(n_x_tiles + 1,),
        in_specs=[
            pl.BlockSpec((ROW_TILE, D_MODEL), lambda i: (jnp.minimum(i, n_x_tiles - 1), 0)),
            _const_spec((ROW_TILE, D_MODEL)),
            _const_spec((1, D_MODEL)),
            _const_spec((D_MODEL, D_FF)),
            _const_spec((D_MODEL, D_FF)),
            _const_spec((D_FF, D_MODEL)),
        ],
        out_specs=pl.BlockSpec((ROW_TILE, D_MODEL), lambda i: (i, 0)),
        compiler_params=pltpu.CompilerParams(
            dimension_semantics=("arbitrary",), vmem_limit_bytes=VMEM_LIMIT),
        name="ffn1",
    )(x, tail, nw, wg, wu, wd)


def _lower_bound_gate(z, lbp, valid):
    m = jnp.max(lbp, axis=0, keepdims=True)
    e = jnp.exp(lbp - m)
    lb = e[0:1, :] / jnp.sum(e, axis=0, keepdims=True)
    kk = (1.0 - lb) * _sigmoid(-z)
    kk = jnp.where(valid, kk, 0.0)
    return kk, jnp.log2(1.0 - kk)


def _head_rms_rope(z, ms, w, cos, sin):
    zn = z * lax.rsqrt(ms + EPS) * w
    width = z.shape[1]
    lane = lax.broadcasted_iota(jnp.int32, z.shape, 1)
    partner = jnp.where((lane & 1) == 0, pltpu.roll(zn, width - 1, 1), pltpu.roll(zn, 1, 1))
    return zn * cos + partner * sin


def _rope_tile(by_row_ref, by_col_ref, is_tail, tail_value):
    groups = ROW_TILE // GRID_W
    rp = by_row_ref[...]
    by_row = jnp.concatenate([jnp.broadcast_to(rp[j:j + 1, :], (GRID_W, 128)) for j in range(groups)], axis=0)
    by_col = jnp.concatenate([by_col_ref[...]] * groups, axis=0)
    lane = lax.broadcasted_iota(jnp.int32, (1, 128), 1)
    t = jnp.where((lane & (AT_HD - 1)) < AT_HD // 2, by_row, by_col)
    return jnp.where(is_tail, tail_value, t)


def _mix_proj_kernel(h_ref, nw_ref, win_ref, lbf_ref, lbb_ref, qnw_ref, knw_ref, rcos_ref, rsin_ref,
                     ccos_ref, csin_ref, bdq_ref, bdk_ref,
                     hq_ref, hv_ref, kf_ref, lff_ref, kb_ref, lfb_ref, gs_ref, qT_ref, k_ref, vT_ref,
                     ga_ref, gb_ref, *, n_valid, n_x_tiles):
    i = pl.program_id(0)
    un = _rms(h_ref[...], nw_ref[...]).astype(BF16)

    def proj(piece):
        off, size = IN_OFFS[piece], IN_SIZES[piece]
        return _dot(un, win_ref[:, off:off + size])

    row = i * ROW_TILE + lax.broadcasted_iota(jnp.int32, (ROW_TILE, 1), 0)
    valid = row < n_valid

    zq = proj(5)
    zk = proj(6)
    cos = _rope_tile(rcos_ref, ccos_ref, i >= n_x_tiles, 1.0)
    sin = _rope_tile(rsin_ref, csin_ref, i >= n_x_tiles, 0.0)
    cos_q = jnp.concatenate([cos] * (AT_W // 128), axis=1)
    sin_q = jnp.concatenate([sin] * (AT_W // 128), axis=1)
    ms_q = _dot((zq * zq).astype(BF16), bdq_ref[...])
    ms_k = _dot((zk * zk).astype(BF16), bdk_ref[...])
    q = _head_rms_rope(zq, ms_q, qnw_ref[...], cos_q, sin_q) * Q_SCALE
    qT_ref[...] = jnp.transpose(q).astype(BF16)
    k_ref[...] = _head_rms_rope(zk, ms_k, knw_ref[...], cos, sin).astype(BF16)

    hq_ref[...] = _silu(proj(0))
    hv_ref[...] = proj(1).astype(BF16)
    kf, lff = _lower_bound_gate(proj(2), lbf_ref[...], valid)
    kf_ref[...] = kf
    lff_ref[...] = lff
    kb, lfb = _lower_bound_gate(proj(3), lbb_ref[...], valid)
    kb_ref[...] = kb
    lfb_ref[...] = lfb
    gs_ref[...] = _silu(proj(4))

    vT = jnp.transpose(proj(7)).astype(BF16)
    ones = jnp.ones((V_ROWS - AT_HD, ROW_TILE), BF16)
    for g in range(AT_KV_HEADS):
        vT_ref[g, 0:AT_HD, :] = vT[g * AT_HD:(g + 1) * AT_HD, :]
        vT_ref[g, AT_HD:V_ROWS, :] = ones

    ga_ref[...] = _sigmoid(proj(8))
    gb_ref[...] = _sigmoid(proj(9))


def _mix_proj_call(h1, nw, win, lbf, lbb, qnw, knw, rope, bdq, bdk, n_valid):
    rows = h1.shape[0]
    n_x_tiles = rows // ROW_TILE - 1
    groups = ROW_TILE // GRID_W
    rope_row_spec = pl.BlockSpec((groups, 128), lambda i: (jnp.minimum(i, n_x_tiles - 1), 0))
    row_spec = lambda w: pl.BlockSpec((ROW_TILE, w), lambda i: (i, 0))
    out_shape = [
        jax.ShapeDtypeStruct((rows, HG_W), F32),
        jax.ShapeDtypeStruct((rows, HG_W), BF16),
        jax.ShapeDtypeStruct((rows, HG_W), F32),
        jax.ShapeDtypeStruct((rows, HG_W), F32),
        jax.ShapeDtypeStruct((rows, HG_W), F32),
        jax.ShapeDtypeStruct((rows, HG_W), F32),
        jax.ShapeDtypeStruct((rows, HG_W), F32),
        jax.ShapeDtypeStruct((AT_W, rows), BF16),
        jax.ShapeDtypeStruct((rows, AT_KVW), BF16),
        jax.ShapeDtypeStruct((AT_KV_HEADS, V_ROWS, rows), BF16),
        jax.ShapeDtypeStruct((rows, D_MODEL), F32),
        jax.ShapeDtypeStruct((rows, D_MODEL), F32),
    ]
    out_specs = [
        row_spec(HG_W), row_spec(HG_W), row_spec(HG_W), row_spec(HG_W), row_spec(HG_W), row_spec(HG_W),
        row_spec(HG_W),
        pl.BlockSpec((AT_W, ROW_TILE), lambda i: (0, i)),
        row_spec(AT_KVW),
        pl.BlockSpec((AT_KV_HEADS, V_ROWS, ROW_TILE), lambda i: (0, 0, i)),
        row_spec(D_MODEL), row_spec(D_MODEL),
    ]
    in_specs = [
        row_spec(D_MODEL),
        _const_spec((1, D_MODEL)),
        _const_spec((D_MODEL, D_IN)),
        _const_spec(lbf.shape),
        _const_spec(lbb.shape),
        _const_spec((1, AT_W)),
        _const_spec((1, AT_KVW)),
        rope_row_spec,
        rope_row_spec,
        _const_spec((GRID_W, 128)),
        _const_spec((GRID_W, 128)),
        _const_spec((AT_W, AT_W)),
        _const_spec((AT_KVW, AT_KVW)),
    ]
    return pl.pallas_call(
        functools.partial(_mix_proj_kernel, n_valid=n_valid, n_x_tiles=n_x_tiles),
        out_shape=out_shape,
        grid=(rows // ROW_TILE,),
        in_specs=in_specs,
        out_specs=out_specs,
        compiler_params=pltpu.CompilerParams(
            dimension_semantics=("arbitrary",), vmem_limit_bytes=VMEM_LIMIT),
        name="mix_proj",
    )(h1, nw, win, lbf, lbb, qnw, knw, *rope, bdq, bdk)


def _split2(x):
    hi = x.astype(BF16)
    lo = (x - hi.astype(F32)).astype(BF16)
    return hi, lo


def _level_operand(q, k, b, lf, blk, reverse):
    c = CHUNK
    if blk >= 8:
        pieces = []
        for g in range(c // (2 * blk)):
            r0 = g * 2 * blk
            lo, hi = slice(r0, r0 + blk), slice(r0 + blk, r0 + 2 * blk)
            if reverse:
                ref = b[r0 + blk:r0 + blk + 1, :]
                pieces.append(q[lo] * jnp.exp2(b[lo] - ref))
                pieces.append(k[hi] * jnp.exp2(ref - b[hi]))
            else:
                ref = b[r0 + blk - 1:r0 + blk, :]
                pieces.append(k[lo] * jnp.exp2(ref - b[lo]))
                pieces.append(q[hi] * jnp.exp2(b[hi] - ref))
        return jnp.concatenate(pieces, axis=0).astype(BF16)

    q3, k3, b3, lf3 = (x.reshape(c // 8, 8, HG_W) for x in (q, k, b, lf))
    sub = lax.broadcasted_iota(jnp.int32, (1, 8, 1), 1)
    later = ((sub >> (blk.bit_length() - 1)) & 1) == 1
    q_role = jnp.logical_not(later) if reverse else later
    if blk == 4:
        ref = b3[:, 4:5, :] if reverse else b3[:, 3:4, :]
        gl = -jnp.abs(b3 - ref)
    elif blk == 2:
        up = pltpu.roll(lf3, 7, 1)
        dn = pltpu.roll(lf3, 1, 1)
        m4 = sub & 3
        if reverse:
            gl = jnp.where(m4 == 0, lf3 + up, jnp.where(m4 == 1, lf3, jnp.where(m4 == 2, 0.0, dn)))
        else:
            gl = jnp.where(m4 == 0, up, jnp.where(m4 == 1, 0.0, jnp.where(m4 == 2, lf3, dn + lf3)))
    else:
        odd = (sub & 1) == 1
        gl = jnp.where(odd, 0.0, lf3) if reverse else jnp.where(odd, lf3, 0.0)
    y3 = jnp.where(q_role, q3, k3) * jnp.exp2(gl)
    return y3.reshape(c, HG_W).astype(BF16)


def _hgrn_direction(q_ref, v_ref, k_ref, lf_ref, o_ref, st_ref, reverse):
    c = CHUNK
    q = q_ref[...]
    k = k_ref[...]
    v = v_ref[...]
    lf = lf_ref[...]

    r_i = lax.broadcasted_iota(jnp.int32, (c, c), 0)
    c_i = lax.broadcasted_iota(jnp.int32, (c, c), 1)
    tri = ((c_i >= r_i) if reverse else (c_i <= r_i)).astype(BF16)
    hi, lo = _split2(lf)
    b = _dot(jnp.concatenate([tri, tri], axis=1), jnp.concatenate([hi, lo], axis=0))

    levels = (64, 32, 16, 8, 4, 2, 1)
    ys = [_level_operand(q, k, b, lf, blk, reverse) for blk in levels]

    lane = lax.broadcasted_iota(jnp.int32, (1, c), 1)
    small_valid = {}
    for blk in (4, 2, 1):
        sh = blk.bit_length() - 1
        same = (r_i >> (sh + 1)) == (c_i >> (sh + 1))
        t_later = ((r_i >> sh) & 1) == 1
        s_later = ((c_i >> sh) & 1) == 1
        small_valid[blk] = (same & jnp.logical_not(t_later) & s_later) if reverse else (
            same & t_later & jnp.logical_not(s_later))

    def tile_mask(blk, tile):
        if blk < 8:
            return small_valid[blk][8 * tile:8 * tile + 8, :]
        block = (8 * tile) // blk
        if (block % 2 == 1) == reverse:
            return None
        col0 = (block + 1) * blk if reverse else (block - 1) * blk
        return (lane >= col0) & (lane < col0 + blk)

    b_last = b[0:1, :] if reverse else b[c - 1:c, :]
    q_inter = (q * jnp.exp2(b)).astype(BF16)
    k_state = (k * jnp.exp2(b_last - b)).astype(BF16)
    e_last = jnp.exp2(b_last)
    qk = q * k
    eye = r_i == c_i

    def issue(h):
        hs = slice(h * HG_K, (h + 1) * HG_K)
        products = [_dot_nt(y[:, hs], y[:, hs]) for y in ys]
        st = st_ref[h]
        inter = _dot_nt(q_inter[:, hs], st.astype(BF16))
        st_ref[h] = st * e_last[:, hs] + _dot_tn(v[:, hs], k_state[:, hs])
        return products, inter

    def finish(h, issued):
        products, inter = issued
        hs = slice(h * HG_K, (h + 1) * HG_K)
        diag = jnp.where(eye, jnp.sum(qk[:, hs], axis=1, keepdims=True), 0.0)
        tiles = []
        for t in range(c // 8):
            rows = slice(8 * t, 8 * t + 8)
            a_t = diag[rows]
            for blk, p in zip(levels, products):
                mask = tile_mask(blk, t)
                if mask is not None:
                    a_t = jnp.where(mask, p[rows], a_t)
            tiles.append(a_t)
        a = jnp.concatenate(tiles, axis=0)
        o_ref[:, hs] = _dot(a.astype(BF16), v[:, hs]) + inter

    return issue, finish


def _hgrn_meta_state(v_ref, k_ref, lf_ref, st_ref):
    c = CHUNK
    k = k_ref[...]
    v = v_ref[...]
    r_i = lax.broadcasted_iota(jnp.int32, (c, c), 0)
    c_i = lax.broadcasted_iota(jnp.int32, (c, c), 1)
    tri = (c_i <= r_i).astype(BF16)
    hi, lo = _split2(lf_ref[...])
    b = _dot(jnp.concatenate([tri, tri], axis=1), jnp.concatenate([hi, lo], axis=0))
    k_state = (k * jnp.exp2(b[c - 1:c, :] - b)).astype(BF16)
    for h in range(HG_HEADS):
        hs = slice(h * HG_K, (h + 1) * HG_K)
        st_ref[h] = _dot_tn(v[:, hs], k_state[:, hs])


def _hgrn_kernel(qf_ref, vf_ref, kf_ref, lff_ref, qb_ref, vb_ref, kb_ref, lfb_ref, vm_ref, kfm_ref, lfm_ref,
                 of_ref, ob_ref, sf_ref, sb_ref):
    @pl.when(pl.program_id(0) == 0)
    def _():
        sb_ref[...] = jnp.zeros_like(sb_ref)
        _hgrn_meta_state(vm_ref, kfm_ref, lfm_ref, sf_ref)

    def sub(ref, j):
        return ref.at[pl.ds(j * CHUNK, CHUNK)]

    units = []
    for j in range(HGRN_STEP_CHUNKS):
        jb = HGRN_STEP_CHUNKS - 1 - j
        fwd = (sub(qf_ref, j), sub(vf_ref, j), sub(kf_ref, j), sub(lff_ref, j), sub(of_ref, j), sf_ref, False)
        bwd = (sub(qb_ref, jb), sub(vb_ref, jb), sub(kb_ref, jb), sub(lfb_ref, jb), sub(ob_ref, jb), sb_ref, True)
        for args in (fwd, bwd):
            issue, finish = _hgrn_direction(*args)
            units += [(issue, finish, h) for h in range(HG_HEADS)]

    pending = None
    for issue, finish, h in units:
        issued = issue(h)
        if pending is not None:
            pending[0](pending[1], pending[2])
        pending = (finish, h, issued)
    pending[0](pending[1], pending[2])


def _hgrn_call(hq, hv, kf, lff, kb, lfb, n_real):
    nb = n_real // CHUNK
    steps = nb // HGRN_STEP_CHUNKS
    fwd = lambda s: (s, 0)
    bwd = lambda s: (steps - 1 - s, 0)
    spec = lambda m: pl.BlockSpec((HGRN_STEP_CHUNKS * CHUNK, HG_W), m)
    meta = pl.BlockSpec((CHUNK, HG_W), lambda s: (nb, 0))
    out = jax.ShapeDtypeStruct((n_real, HG_W), F32)
    return pl.pallas_call(
        _hgrn_kernel,
        out_shape=[out, out],
        grid=(steps,),
        in_specs=[spec(fwd), spec(fwd), spec(fwd), spec(fwd), spec(bwd), spec(bwd), spec(bwd), spec(bwd),
                  meta, meta, meta],
        out_specs=[spec(fwd), spec(bwd)],
        scratch_shapes=[pltpu.VMEM((HG_HEADS, HG_K, HG_K), F32), pltpu.VMEM((HG_HEADS, HG_K, HG_K), F32)],
        compiler_params=pltpu.CompilerParams(
            dimension_semantics=("arbitrary",), vmem_limit_bytes=VMEM_LIMIT),
        name="hgrn",
    )(hq, hv, kf, lff, hq, hv, kb, lfb, hv, kf, lff)


def _flash_kernel(qT_ref, k_ref, vT_ref, kt_ref, vTt_ref, o_ref, qp_ref, m_ref, acc_ref, s_ref, mc_ref,
                  st_ref, mct_ref, *,
                  n_kv, n_tail_valid):
    g = pl.program_id(0)
    tq = qT_ref.shape[1]

    half = lax.broadcasted_iota(jnp.int32, (AT_KVW, tq), 0) >> 6
    for h in range(AT_GROUP):
        qh = qT_ref[h * AT_HD:(h + 1) * AT_HD, :].astype(F32)
        q2 = jnp.concatenate([qh, qh], axis=0)
        qp_ref[h] = jnp.where(half == g, q2, 0.0).astype(BF16)

    m_ref[...] = jnp.full(m_ref.shape, -jnp.inf, F32)
    acc_ref[...] = jnp.zeros(acc_ref.shape, F32)

    def store_scores(sT, s_dst, mc_dst, h):
        s_dst[h] = sT
        mc_dst[h] = jnp.max(sT, axis=0, keepdims=True)

    def unit_scores(chunk, h, slot):
        kc = k_ref[pl.ds(pl.multiple_of(chunk * KV_TILE, KV_TILE), KV_TILE), :]
        store_scores(_dot(kc, qp_ref[h]), s_ref.at[slot], mc_ref.at[slot], h)

    def unit_consume(vc, s_src, mc_src, h):
        m_prev = m_ref[h]
        m_new = jnp.maximum(m_prev, mc_src[h])
        alpha = jnp.exp2(m_prev - m_new)
        pT = jnp.exp2(s_src[h] - m_new).astype(BF16)
        acc_ref[h] = alpha * acc_ref[h] + _dot(vc, pT)
        m_ref[h] = m_new

    krow = lax.broadcasted_iota(jnp.int32, (128, 1), 0)
    for h in range(AT_GROUP):
        sT = jnp.where(krow < n_tail_valid, _dot(kt_ref[...], qp_ref[h]), -jnp.inf)
        store_scores(sT, st_ref, mct_ref, h)
    for u in range(FLASH_LEAD):
        unit_scores(u // AT_GROUP, u % AT_GROUP, (u // AT_GROUP) % 2)
    for h in range(AT_GROUP):
        unit_consume(vTt_ref[...], st_ref, mct_ref, h)

    def body(i, carry):
        for j in range(2 * AT_GROUP):
            uq = j + FLASH_LEAD
            cq = jnp.minimum(2 * i + uq // AT_GROUP, n_kv - 1)
            unit_scores(cq, uq % AT_GROUP, (uq // AT_GROUP) % 2)
            cp = 2 * i + j // AT_GROUP
            vc = vT_ref[:, pl.ds(pl.multiple_of(cp * KV_TILE, KV_TILE), KV_TILE)]
            unit_consume(vc, s_ref.at[(j // AT_GROUP) % 2], mc_ref.at[(j // AT_GROUP) % 2], j % AT_GROUP)
        return carry

    lax.fori_loop(0, n_kv // 2, body, 0, unroll=FLASH_UNROLL)

    outs = []
    for h in range(AT_GROUP):
        acc = acc_ref[h]
        outs.append(acc[0:AT_HD, :] / acc[AT_HD:AT_HD + 1, :])
    o_ref[...] = jnp.concatenate(outs, axis=0).astype(BF16)


def _flash_call(qT, k, vT, n_real):
    n_kv = n_real // KV_TILE
    tail_blk = n_real // 128
    gw = AT_GROUP * AT_HD
    return pl.pallas_call(
        functools.partial(_flash_kernel, n_kv=n_kv, n_tail_valid=N_META),
        out_shape=jax.ShapeDtypeStruct((AT_W, n_real), BF16),
        grid=(AT_KV_HEADS, n_real // Q_TILE),
        in_specs=[
            pl.BlockSpec((gw, Q_TILE), lambda g, i: (g, i)),
            pl.BlockSpec((n_real, AT_KVW), lambda g, i: (0, 0)),
            pl.BlockSpec((None, V_ROWS, n_real), lambda g, i: (g, 0, 0)),
            pl.BlockSpec((128, AT_KVW), lambda g, i: (tail_blk, 0)),
            pl.BlockSpec((None, V_ROWS, 128), lambda g, i: (g, 0, tail_blk)),
        ],
        out_specs=pl.BlockSpec((gw, Q_TILE), lambda g, i: (g, i)),
        scratch_shapes=[
            pltpu.VMEM((AT_GROUP, AT_KVW, Q_TILE), BF16),
            pltpu.VMEM((AT_GROUP, 1, Q_TILE), F32),
            pltpu.VMEM((AT_GROUP, V_ROWS, Q_TILE), F32),
            pltpu.VMEM((2, AT_GROUP, KV_TILE, Q_TILE), F32),
            pltpu.VMEM((2, AT_GROUP, 1, Q_TILE), F32),
            pltpu.VMEM((AT_GROUP, 128, Q_TILE), F32),
            pltpu.VMEM((AT_GROUP, 1, Q_TILE), F32),
        ],
        compiler_params=pltpu.CompilerParams(
            dimension_semantics=("arbitrary", "arbitrary"), vmem_limit_bytes=VMEM_LIMIT),
        name="flash",
    )(qT, k, vT, k, vT)


def _merge_ffn_kernel(h1_ref, of_ref, ob_ref, gs_ref, ybT_ref, ga_ref, gb_ref, hgw_ref,
                      wua_ref, wub_ref, wout_ref, nw_ref, wg_ref, wu_ref, wd_ref, o_ref):
    o = of_ref[...] + ob_ref[...]
    normed = []
    for h in range(HG_HEADS):
        oh = o[:, h * HG_K:(h + 1) * HG_K]
        normed.append(oh * lax.rsqrt(jnp.mean(oh * oh, axis=-1, keepdims=True) + EPS))
    ya = (jnp.concatenate(normed, axis=1) * hgw_ref[...] * gs_ref[...]).astype(BF16)
    mixed = ga_ref[...] * _dot(ya, wua_ref[...]) + gb_ref[...] * _dot_tn(ybT_ref[...], wub_ref[...])
    h2 = h1_ref[...] + _dot(mixed.astype(BF16), wout_ref[...])
    o_ref[...] = _swiglu_half_step(h2, nw_ref[...], wg_ref, wu_ref, wd_ref)


def _merge_ffn_call(h1, o_f, o_b, gs, yb, ga, gb, hgw, wua, wub, wout, nw, wg, wu, wd, n_real):
    row_spec = lambda w: pl.BlockSpec((ROW_TILE, w), lambda i: (i, 0))
    return pl.pallas_call(
        _merge_ffn_kernel,
        out_shape=jax.ShapeDtypeStruct((n_real, D_MODEL), F32),
        grid=(n_real // ROW_TILE,),
        in_specs=[
            row_spec(D_MODEL), row_spec(HG_W), row_spec(HG_W), row_spec(HG_W),
            pl.BlockSpec((AT_W, ROW_TILE), lambda i: (0, i)),
            row_spec(D_MODEL), row_spec(D_MODEL),
            _const_spec((1, HG_W)),
            _const_spec((HG_W, D_MODEL)),
            _const_spec((AT_W, D_MODEL)),
            _const_spec((D_MODEL, D_MODEL)),
            _const_spec((1, D_MODEL)),
            _const_spec((D_MODEL, D_FF)),
            _const_spec((D_MODEL, D_FF)),
            _const_spec((D_FF, D_MODEL)),
        ],
        out_specs=row_spec(D_MODEL),
        compiler_params=pltpu.CompilerParams(
            dimension_semantics=("arbitrary",), vmem_limit_bytes=VMEM_LIMIT),
        name="merge_ffn2",
    )(h1, o_f, o_b, gs, yb, ga, gb, hgw, wua, wub, wout, nw, wg, wu, wd)


def _rope_tables(n_real):
    half = AT_HD // 2
    inv = ROPE_THETA ** (-jnp.arange(0, half, 2, dtype=F32) / half)
    sign = jnp.tile(jnp.array([-1.0, 1.0], F32), half)

    def lanes(ang, first_half):
        cos, sin = jnp.cos(ang), jnp.sin(ang)
        one, zero = jnp.ones_like(cos), jnp.zeros_like(sin)
        cos = jnp.concatenate([cos, one] if first_half else [one, cos], axis=-1)
        sin = jnp.concatenate([sin, zero] if first_half else [zero, sin], axis=-1)
        cos = jnp.repeat(cos, 2, axis=-1)
        sin = jnp.repeat(sin, 2, axis=-1) * sign
        reps = 128 // AT_HD
        return jnp.tile(cos, (1, reps)), jnp.tile(sin, (1, reps))

    r = jnp.arange(n_real // GRID_W, dtype=F32)
    c = jnp.arange(GRID_W, dtype=F32)
    return lanes(r[:, None] * inv, True) + lanes(c[:, None] * inv, False)


def _head_mean_matrix(width):
    heads = width // AT_HD
    return jnp.kron(jnp.eye(heads, dtype=F32), jnp.full((AT_HD, AT_HD), 1.0 / AT_HD, F32)).astype(BF16)


def kernel(x, meta_tokens, ffn1_norm, ffn1_w_gate, ffn1_w_up, ffn1_w_down, mix_norm, w_in, hg_lb_fwd, hg_lb_bwd, hg_out_norm, q_norm, k_norm, w_up_a, w_up_b, w_out, ffn2_norm, ffn2_w_gate, ffn2_w_up, ffn2_w_down):
    batch, n_real, _ = x.shape
    assert batch == 1 and n_real % ROW_TILE == 0 and n_real % GRID_W == 0
    assert n_real % (HGRN_STEP_CHUNKS * CHUNK) == 0 and n_real % Q_TILE == 0 and n_real % (2 * KV_TILE) == 0
    assert N_META <= CHUNK <= ROW_TILE and ROW_TILE % GRID_W == 0
    rows = n_real + ROW_TILE
    n_valid = n_real + N_META

    tail = jnp.concatenate(
        [meta_tokens.astype(x.dtype), jnp.zeros((ROW_TILE - N_META, D_MODEL), x.dtype)], axis=0)
    row = lambda w: w.reshape(1, -1).astype(F32)
    (w1g, w1u, w1d, win, wua, wub, wout, w2g, w2u, w2d) = _cast_weights(
        [w[0].astype(F32) for w in (ffn1_w_gate, ffn1_w_up, ffn1_w_down, w_in, w_up_a, w_up_b, w_out,
                                    ffn2_w_gate, ffn2_w_up, ffn2_w_down)])

    h1 = _ffn_call(x[0], tail, row(ffn1_norm[0]), w1g, w1u, w1d)

    rope = _rope_tables(n_real)
    hq, hv, kf, lff, kb, lfb, gs, qT, k, vT, ga, gb = _mix_proj_call(
        h1, row(mix_norm[0]), win, hg_lb_fwd.astype(F32), hg_lb_bwd.astype(F32),
        row(jnp.tile(q_norm[0], AT_HEADS)), row(jnp.tile(k_norm[0], AT_KV_HEADS)), rope,
        _head_mean_matrix(AT_W), _head_mean_matrix(AT_KVW), n_valid)

    o_f, o_b = _hgrn_call(hq, hv, kf, lff, kb, lfb, n_real)
    yb = _flash_call(qT, k, vT, n_real)

    out = _merge_ffn_call(
        h1, o_f, o_b, gs, yb, ga, gb, row(hg_out_norm[0]), wua, wub, wout,
        row(ffn2_norm[0]), w2g, w2u, w2d, n_real)
    return out.reshape(batch, n_real, D_MODEL)
```
